```python
import math, functools
import jax, jax.numpy as jnp
from jax import lax
import numpy as np

D_MODEL = 1024
BATCH = 32
SEQ = 256
DEPTH = 4
DEC_BATCH = 8
DEC_SEQ = 1024
PAST_LEN = 256

GRID_W = 64
D_RWKV = 512
RWKV_HEAD = 64
N_RWKV_HEADS = D_RWKV // RWKV_HEAD
LORA_W = 64
LORA_A = 64
LORA_G = 128
RWKV_COLS = (D_RWKV, D_RWKV, D_RWKV, LORA_W, LORA_W, LORA_A, LORA_A, LORA_G)
D_RWKV_PROJ = 3 * D_RWKV + 2 * LORA_W + 2 * LORA_A + LORA_G
DECAY_SCALE = math.exp(-0.5)
GN_EPS = 64e-5
D_HYENA = 512
HYENA_ORDER = 2
HYENA_EMB = 33
HYENA_BANDS = (HYENA_EMB - 1) // 2
HYENA_HIDDEN = 64
HYENA_SHORT = 3
HYENA_FAST_DECAY = 0.3
HYENA_SLOW_DECAY = 1.5
HYENA_TARGET = 1e-2
D_HYENA_PROJ = (HYENA_ORDER + 1) * D_HYENA
N_BRANCH = 2
D_IN_PROJ = D_RWKV_PROJ + D_HYENA_PROJ + N_BRANCH * D_MODEL
D_FF = 2816
N_EXPERTS = 8
TOP_K = 2
D_FF_EXPERT = 1408
N_DENSE = (DEPTH + 1) // 2
N_MOE = DEPTH // 2
N_MOD = 6
DEEPNORM_ALPHA = (2 * DEPTH) ** 0.25
DEEPNORM_BETA = (8 * DEPTH) ** -0.25
LN_EPS = 1e-5
POS_BASE = 10000.0
N_KEYS = 48

kernel_name = 'hybrid_rwkv7_hyena_dit_step'


def _layer_norm(x, g, b):
    xf = x.astype(jnp.float32)
    mean = jnp.mean(xf, axis=-1, keepdims=True)
    var = jnp.mean(jnp.square(xf - mean), axis=-1, keepdims=True)
    return ((xf - mean) * lax.rsqrt(var + LN_EPS)).astype(x.dtype) * g + b


def _prev(x):
    return jnp.pad(x, ((0, 0), (1, 0), (0, 0)))[:, :-1]


def _next(x):
    return jnp.pad(x, ((0, 0), (0, 1), (0, 0)))[:, 1:]


def _split_cols(x, sizes):
    out, start = [], 0
    for s in sizes:
        out.append(x[..., start:start + s])
        start += s
    return out


def _heads(t):
    return t.reshape(t.shape[:-1] + (N_RWKV_HEADS, RWKV_HEAD))


def _adaln(cond, w, b):
    return jax.nn.silu(cond) @ w + b


def _grid_pos_embed(n_tokens, dtype):
    rows = n_tokens // GRID_W
    row = jnp.repeat(jnp.arange(rows, dtype=jnp.float32), GRID_W)
    col = jnp.tile(jnp.arange(GRID_W, dtype=jnp.float32), rows)
    quarter = D_MODEL // 4
    omega = 1.0 / (POS_BASE ** (jnp.arange(quarter, dtype=jnp.float32) / quarter))

    def enc(pos):
        ang = pos[:, None] * omega
        return jnp.concatenate([jnp.sin(ang), jnp.cos(ang)], axis=-1)

    return jnp.concatenate([enc(row), enc(col)], axis=-1).astype(dtype)


def _wkv_scan(r, w, k, v, kk, a, s0, reverse):
    def step(s, inp):
        r_t, w_t, k_t, v_t, kk_t, a_t = inp
        sa = jnp.einsum('bhvk,bhk->bhv', s, -kk_t)
        s = (s * w_t[:, :, None, :] + sa[..., :, None] * (kk_t * a_t)[..., None, :]
             + v_t[..., :, None] * k_t[..., None, :])
        return s, jnp.einsum('bhvk,bhk->bhv', s, r_t)

    xs = tuple(jnp.moveaxis(t.astype(jnp.float32), 1, 0) for t in (r, w, k, v, kk, a))
    s_fin, y = lax.scan(step, s0.astype(jnp.float32), xs, reverse=reverse)
    return jnp.moveaxis(y, 0, 1).astype(r.dtype), s_fin.astype(s0.dtype)


def _rwkv7(p, lp, s0):
    B, L, _ = p.shape
    r, k, v, w_f, w_b, a_f, a_b, g_low = _split_cols(p, RWKV_COLS)
    kk = _heads(k * lp['rwkv_k_k']).astype(jnp.float32)
    kk = (kk * lax.rsqrt(jnp.maximum(jnp.sum(kk * kk, axis=-1, keepdims=True), 1e-24))).astype(p.dtype)
    r_h, v_h = _heads(r), _heads(v)
    ys, ks, finals = [], [], []
    for d, (w_low, a_low) in enumerate(((w_f, a_f), (w_b, a_b))):
        log_w = -DECAY_SCALE * jax.nn.sigmoid(lp['rwkv_w0'][d] + jnp.tanh(w_low) @ lp['rwkv_w2'][d])
        decay = jnp.exp(log_w.astype(jnp.float32))
        a = jax.nn.sigmoid(lp['rwkv_a0'][d] + a_low @ lp['rwkv_a2'][d])
        k_d = _heads(k * (1 + (a - 1) * lp['rwkv_k_a']))
        y_d, s_d = _wkv_scan(r_h, _heads(decay), k_d, v_h, kk, _heads(a), s0[:, d], reverse=(d == 1))
        ys.append(y_d)
        ks.append(k_d)
        finals.append(s_d)
    y32 = (ys[0] + ys[1]).astype(jnp.float32)
    mean = jnp.mean(y32, axis=-1, keepdims=True)
    var = jnp.mean(jnp.square(y32 - mean), axis=-1, keepdims=True)
    y_n = ((y32 - mean) * lax.rsqrt(var + GN_EPS)).astype(p.dtype).reshape(B, L, D_RWKV)
    y_n = y_n * lp['rwkv_gn_w'] + lp['rwkv_gn_b']
    bonus = jnp.sum(r_h * (0.5 * (ks[0] + ks[1])) * lp['rwkv_r_k'], axis=-1, keepdims=True) * v_h
    g = jax.nn.sigmoid(g_low) @ lp['rwkv_g2']
    y = (y_n + bonus.reshape(B, L, D_RWKV)) * g
    return y, jnp.stack(finals, axis=1)


def _hyena_filters(L, lp, dtype):
    t = jnp.linspace(0.0, 1.0, L, dtype=jnp.float32)[:, None]
    f = jnp.linspace(1e-4, HYENA_BANDS - 1, HYENA_BANDS, dtype=jnp.float32)
    ang = (2.0 * math.pi / L) * jnp.arange(L, dtype=jnp.float32)[:, None] * f
    feats = jnp.concatenate([t, jnp.cos(ang), -jnp.sin(ang)], axis=-1).astype(dtype)
    hid = jnp.sin(feats @ lp['hy_f1'] + lp['hy_fb1'])
    hid = jnp.sin(hid @ lp['hy_f2'] + lp['hy_fb2'])
    h = (hid @ lp['hy_f3']).astype(jnp.float32).reshape(L, HYENA_ORDER, 2, D_HYENA)
    deltas = jnp.abs(jnp.linspace(math.log(HYENA_TARGET) / HYENA_SLOW_DECAY,
                                  math.log(HYENA_TARGET) / HYENA_FAST_DECAY, D_HYENA, dtype=jnp.float32))
    h = h * jnp.exp(-t * deltas)[:, None, None, :]
    h_fwd, h_bwd = h[:, :, 0], h[:, :, 1]
    kern = jnp.concatenate([h_fwd, jnp.zeros((1, HYENA_ORDER, D_HYENA), jnp.float32), h_bwd[:0:-1]], axis=0)
    return jnp.fft.rfft(kern, axis=0)


def _hyena(p, lp):
    B, L, _ = p.shape
    cw = lp['hy_conv_w']
    u = cw[0] * _prev(p) + cw[1] * p + cw[2] * _next(p) + lp['hy_conv_b']
    z, *gates = jnp.split(u, HYENA_ORDER + 1, axis=-1)
    kf = _hyena_filters(L, lp, p.dtype)
    for n in range(HYENA_ORDER):
        zf = jnp.fft.rfft(z.astype(jnp.float32), n=2 * L, axis=1)
        conv = jnp.fft.irfft(zf * kf[None, :, n], n=2 * L, axis=1)[:, :L].astype(p.dtype)
        z = gates[n] * (conv + lp['hy_skip'][n] * z)
    return z


def _mixer(h, lp, s0):
    proj = h @ lp['w_in']
    p_r = proj[..., :D_RWKV_PROJ]
    p_h = proj[..., D_RWKV_PROJ:D_RWKV_PROJ + D_HYENA_PROJ]
    g_a, g_b = jnp.split(proj[..., D_RWKV_PROJ + D_HYENA_PROJ:], N_BRANCH, axis=-1)
    mu = lp['rwkv_mu']
    p_r = p_r + mu[0] * (_prev(p_r) - p_r) + mu[1] * (_next(p_r) - p_r)
    y_a, s_fin = _rwkv7(p_r, lp, s0)
    y_b = _hyena(p_h, lp)
    merged = jax.nn.sigmoid(g_a) * (y_a @ lp['w_pa']) + jax.nn.sigmoid(g_b) * (y_b @ lp['w_pb'])
    return merged @ lp['w_o'], s_fin


def _dense_swiglu(h, w_in, w_out):
    gate, up = jnp.split(h @ w_in, 2, axis=-1)
    return (jax.nn.silu(gate) * up) @ w_out


def _moe_swiglu(h, router_w, router_b, w_in, w_out):
    probs = jax.nn.softmax((h @ router_w + router_b).astype(jnp.float32), axis=-1)
    top_p, top_i = lax.top_k(probs, TOP_K)
    top_p = top_p / jnp.sum(top_p, axis=-1, keepdims=True)
    combine = jnp.sum(jax.nn.one_hot(top_i, N_EXPERTS, dtype=jnp.float32) * top_p[..., None], axis=-2)
    gate, up = jnp.split(jnp.einsum('bld,edf->blef', h, w_in), 2, axis=-1)
    act = jax.nn.silu(gate) * up * combine.astype(h.dtype)[..., None]
    return jnp.einsum('blef,efd->bld', act, w_out)


def _block(x, mod, s0, lp, ffn):
    shift1, scale1, gate1, shift2, scale2, gate2 = jnp.split(mod[:, None, :], N_MOD, axis=-1)
    m, s_fin = _mixer(x * (1 + scale1) + shift1, lp, s0)
    x = _layer_norm(DEEPNORM_ALPHA * x + gate1 * m, lp['ln_g'][0], lp['ln_b'][0])
    f = ffn(x * (1 + scale2) + shift2)
    x = _layer_norm(DEEPNORM_ALPHA * x + gate2 * f, lp['ln_g'][1], lp['ln_b'][1])
    return x, s_fin


def setup_inputs(seed: int = 0) -> dict:
    key = jax.random.key(seed)
    keys = jax.random.split(key, N_KEYS)
    it = iter([keys[i] for i in range(N_KEYS)])

    def nrm(shape, scale):
        return jax.random.normal(next(it), shape, jnp.float32) * scale

    def unif(shape, lo, hi):
        return jax.random.uniform(next(it), shape, jnp.float32, lo, hi)

    d, H, N = D_MODEL, N_RWKV_HEADS, RWKV_HEAD
    return {
        'x_prompt': nrm((BATCH, SEQ, d), 1.0),
        'x_sample': nrm((DEC_BATCH, DEC_SEQ, d), 1.0),
        'c': nrm((DEC_BATCH, d), 1.0),
        'state_wkv': nrm((DEC_BATCH, DEPTH, 2, H, N, N), 0.5),
        'c_ctx': nrm((d,), 1.0),
        'w_ada': nrm((DEPTH, d, N_MOD * d), 0.5 * d ** -0.5),
        'b_ada': nrm((DEPTH, N_MOD * d), 0.02),
        'w_in': nrm((DEPTH, d, D_IN_PROJ), d ** -0.5),
        'rwkv_mu': unif((DEPTH, 2, D_RWKV_PROJ), 0.0, 0.5),
        'rwkv_w0': nrm((DEPTH, 2, D_RWKV), 1.0),
        'rwkv_w2': nrm((DEPTH, 2, LORA_W, D_RWKV), LORA_W ** -0.5),
        'rwkv_a0': nrm((DEPTH, 2, D_RWKV), 0.5),
        'rwkv_a2': nrm((DEPTH, 2, LORA_A, D_RWKV), LORA_A ** -0.5),
        'rwkv_g2': nrm((DEPTH, LORA_G, D_RWKV), LORA_G ** -0.5),
        'rwkv_k_k': 0.85 + nrm((DEPTH, D_RWKV), 0.05),
        'rwkv_k_a': 1.0 + nrm((DEPTH, D_RWKV), 0.05),
        'rwkv_r_k': nrm((DEPTH, H, N), 0.1),
        'rwkv_gn_w': 1.0 + nrm((DEPTH, D_RWKV), 0.05),
        'rwkv_gn_b': nrm((DEPTH, D_RWKV), 0.01),
        'hy_conv_w': nrm((DEPTH, HYENA_SHORT, D_HYENA_PROJ), HYENA_SHORT ** -0.5),
        'hy_conv_b': nrm((DEPTH, D_HYENA_PROJ), 0.01),
        'hy_f1': nrm((DEPTH, HYENA_EMB, HYENA_HIDDEN), 2.0 * HYENA_EMB ** -0.5),
        'hy_fb1': nrm((DEPTH, HYENA_HIDDEN), 0.1),
        'hy_f2': nrm((DEPTH, HYENA_HIDDEN, HYENA_HIDDEN), 2.0 * HYENA_HIDDEN ** -0.5),
        'hy_fb2': nrm((DEPTH, HYENA_HIDDEN), 0.1),
        'hy_f3': nrm((DEPTH, HYENA_HIDDEN, HYENA_ORDER * 2 * D_HYENA), 0.1 * HYENA_HIDDEN ** -0.5),
        'hy_skip': nrm((DEPTH, HYENA_ORDER, D_HYENA), 1.0),
        'w_pa': nrm((DEPTH, D_RWKV, d), D_RWKV ** -0.5),
        'w_pb': nrm((DEPTH, D_HYENA, d), D_HYENA ** -0.5),
        'w_o': nrm((DEPTH, d, d), DEEPNORM_BETA * d ** -0.5),
        'ln_g': 1.0 + nrm((DEPTH, 2, d), 0.05),
        'ln_b': nrm((DEPTH, 2, d), 0.01),
        'ffn_w_in': nrm((N_DENSE, d, 2 * D_FF), d ** -0.5),
        'ffn_w_out': nrm((N_DENSE, D_FF, d), DEEPNORM_BETA * D_FF ** -0.5),
        'router_w': nrm((N_MOE, d, N_EXPERTS), d ** -0.5),
        'router_b': nrm((N_MOE, N_EXPERTS), 0.01),
        'exp_w_in': nrm((N_MOE, N_EXPERTS, d, 2 * D_FF_EXPERT), d ** -0.5),
        'exp_w_out': nrm((N_MOE, N_EXPERTS, D_FF_EXPERT, d), DEEPNORM_BETA * D_FF_EXPERT ** -0.5),
    }


def reference(x_prompt, x_sample, c, state_wkv, c_ctx, w_ada, b_ada, w_in, rwkv_mu, rwkv_w0, rwkv_w2,
              rwkv_a0, rwkv_a2, rwkv_g2, rwkv_k_k, rwkv_k_a, rwkv_r_k, rwkv_gn_w, rwkv_gn_b, hy_conv_w,
              hy_conv_b, hy_f1, hy_fb1, hy_f2, hy_fb2, hy_f3, hy_skip, w_pa, w_pb, w_o, ln_g, ln_b,
              ffn_w_in, ffn_w_out, router_w, router_b, exp_w_in, exp_w_out):
    xp = x_prompt
    xs = x_sample + _grid_pos_embed(x_sample.shape[1], x_sample.dtype)[None]
    s0_ctx = jnp.zeros((x_prompt.shape[0], 2, N_RWKV_HEADS, RWKV_HEAD, RWKV_HEAD), state_wkv.dtype)
    ctx_states = []
    for l in range(DEPTH):
        lp = {
            'w_in': w_in[l], 'rwkv_mu': rwkv_mu[l], 'rwkv_w0': rwkv_w0[l], 'rwkv_w2': rwkv_w2[l],
            'rwkv_a0': rwkv_a0[l], 'rwkv_a2': rwkv_a2[l], 'rwkv_g2': rwkv_g2[l], 'rwkv_k_k': rwkv_k_k[l],
            'rwkv_k_a': rwkv_k_a[l], 'rwkv_r_k': rwkv_r_k[l], 'rwkv_gn_w': rwkv_gn_w[l],
            'rwkv_gn_b': rwkv_gn_b[l], 'hy_conv_w': hy_conv_w[l], 'hy_conv_b': hy_conv_b[l],
            'hy_f1': hy_f1[l], 'hy_fb1': hy_fb1[l], 'hy_f2': hy_f2[l], 'hy_fb2': hy_fb2[l],
            'hy_f3': hy_f3[l], 'hy_skip': hy_skip[l], 'w_pa': w_pa[l], 'w_pb': w_pb[l], 'w_o': w_o[l],
            'ln_g': ln_g[l], 'ln_b': ln_b[l],
        }
        if l % 2 == 0:
            ffn = functools.partial(_dense_swiglu, w_in=ffn_w_in[l // 2], w_out=ffn_w_out[l // 2])
        else:
            ffn = functools.partial(_moe_swiglu, router_w=router_w[l // 2], router_b=router_b[l // 2],
                                    w_in=exp_w_in[l // 2], w_out=exp_w_out[l // 2])
        mod_ctx = _adaln(c_ctx[None], w_ada[l], b_ada[l])
        mod_lat = _adaln(c, w_ada[l], b_ada[l])
        xp, s_ctx = _block(xp, mod_ctx, s0_ctx, lp, ffn)
        xs, _ = _block(xs, mod_lat, state_wkv[:, l], lp, ffn)
        ctx_states.append(s_ctx)
    new_state_wkv = jnp.stack(ctx_states, axis=1)
    return (xp, xs, new_state_wkv)
```

```python
import functools
import math

import jax
import jax.numpy as jnp
import ml_dtypes
import numpy as np
from jax.experimental import pallas as pl
from jax.experimental.pallas import tpu as pltpu

F32 = jnp.float32
BF16 = jnp.bfloat16

D_MODEL = 1024
GRID_W = 64
D_RWKV = 512
RWKV_HEAD = 64
N_RWKV_HEADS = D_RWKV // RWKV_HEAD
LORA_W = 64
LORA_A = 64
LORA_G = 128
D_RWKV_PROJ = 3 * D_RWKV + 2 * LORA_W + 2 * LORA_A + LORA_G
DECAY_SCALE = math.exp(-0.5)
GN_EPS = 64e-5
D_HYENA = 512
HYENA_ORDER = 2
HYENA_EMB = 33
HYENA_BANDS = (HYENA_EMB - 1) // 2
HYENA_HIDDEN = 64
HYENA_FAST_DECAY = 0.3
HYENA_SLOW_DECAY = 1.5
HYENA_TARGET = 1e-2
D_HYENA_PROJ = (HYENA_ORDER + 1) * D_HYENA
D_IN_PROJ = D_RWKV_PROJ + D_HYENA_PROJ + 2 * D_MODEL
D_FF = 2816
N_EXPERTS = 8
TOP_K = 2
D_FF_EXPERT = 1408
N_MOD = 6
LN_EPS = 1e-5
POS_BASE = 10000.0

LANES = 128
SUBLANES = 8
VMEM_LIMIT = 56 * 1024 * 1024

ROW_TILE = 256
HY_COL_TILE = 256
SCAN_ROWS = 4
SCAN_TBLOCK = 32
PAD = LANES


def _cparams(*sem):
    return pltpu.CompilerParams(dimension_semantics=sem, vmem_limit_bytes=VMEM_LIMIT)


def _dot(a, b):
    return jnp.dot(a, b, preferred_element_type=F32)


def _split2(x):
    hi = x.astype(BF16)
    lo = (x - hi.astype(F32)).astype(BF16)
    return hi, lo


def _dot3(a, b):
    ah, al = _split2(a)
    bh, bl = _split2(b)
    return _dot(ah, bh) + _dot(al, bh) + _dot(ah, bl)


def _dot3_lhs_split(ah, al, b):
    bh, bl = _split2(b)
    return _dot(ah, bh) + _dot(al, bh) + _dot(ah, bl)


def _sigmoid(x):
    return 1.0 / (1.0 + jnp.exp(-x))


def _silu(x):
    return x * _sigmoid(x)


def _head_ones():
    i = jax.lax.broadcasted_iota(jnp.int32, (D_RWKV, D_RWKV), 0) // RWKV_HEAD
    j = jax.lax.broadcasted_iota(jnp.int32, (D_RWKV, D_RWKV), 1) // RWKV_HEAD
    return jnp.where(i == j, 1.0, 0.0).astype(BF16)


def _head_sum(x, ones):
    h1 = x.astype(BF16)
    r1 = x - h1.astype(F32)
    h2 = r1.astype(BF16)
    h3 = (r1 - h2.astype(F32)).astype(BF16)
    return _dot(h1, ones) + _dot(h2, ones) + _dot(h3, ones)


def _layer_norm(z, g, b):
    mean = jnp.mean(z, axis=-1, keepdims=True)
    d = z - mean
    var = jnp.mean(d * d, axis=-1, keepdims=True)
    return d * jax.lax.rsqrt(var + LN_EPS) * g + b


def _ada_kernel(c_ref, w_ref, b_ref, o_ref):
    o_ref[...] = _dot3(_silu(c_ref[...]), w_ref[...]) + b_ref[...]


def _adaln(cond, w_ada, b_ada):
    depth, d, n = w_ada.shape
    rows = cond.shape[0]
    tn = 1536
    return pl.pallas_call(
        _ada_kernel,
        grid=(depth, n // tn),
        in_specs=[
            pl.BlockSpec((rows, d), lambda l, j: (0, 0)),
            pl.BlockSpec((None, d, tn), lambda l, j: (l, 0, j)),
            pl.BlockSpec((None, 1, tn), lambda l, j: (l, 0, j)),
        ],
        out_specs=pl.BlockSpec((None, rows, tn), lambda l, j: (l, 0, j)),
        out_shape=jax.ShapeDtypeStruct((depth, rows, n), F32),
        compiler_params=_cparams("arbitrary", "arbitrary"),
        name="adaln",
    )(cond, w_ada, b_ada.reshape(depth, 1, n))


def _mod_row_map(n_ctx_tiles, tiles_per_seq, ctx_row):
    def index_map(i, *_):
        return (jnp.where(i < n_ctx_tiles, ctx_row, (i - n_ctx_tiles) // tiles_per_seq), 0, 0)

    return index_map


def _inproj_kernel(x_ref, mod_ref, w_ref, pr_ref, ph_ref, g_ref):
    h = (x_ref[...] * (1.0 + mod_ref[1:2, :]) + mod_ref[0:1, :]).astype(BF16)
    a, b = D_RWKV_PROJ, D_RWKV_PROJ + D_HYENA_PROJ
    pr_ref[...] = _dot(h, w_ref[:, 0:a])
    ph_ref[...] = _dot(h, w_ref[:, a:b])
    g_ref[...] = _dot(h, w_ref[:, b:D_IN_PROJ])


def _inproj(x, mod, w_in_bf, mod_map):
    n = x.shape[0]
    tm = ROW_TILE
    row = lambda width: pl.BlockSpec((tm, width), lambda i: (i, 0))
    return pl.pallas_call(
        _inproj_kernel,
        grid=(n // tm,),
        in_specs=[
            row(D_MODEL),
            pl.BlockSpec((None, N_MOD, D_MODEL), mod_map),
            pl.BlockSpec((D_MODEL, D_IN_PROJ), lambda i: (0, 0)),
        ],
        out_specs=[row(D_RWKV_PROJ), row(D_HYENA_PROJ), row(2 * D_MODEL)],
        out_shape=[
            jax.ShapeDtypeStruct((n, D_RWKV_PROJ), F32),
            jax.ShapeDtypeStruct((n, D_HYENA_PROJ), F32),
            jax.ShapeDtypeStruct((n, 2 * D_MODEL), F32),
        ],
        compiler_params=_cparams("arbitrary"),
        name="inproj",
    )(x, mod, w_in_bf)


def _prep_kernel(tiles_ctx, tiles_lat, n_ctx_tiles,
                 p_ref, prev_ref, next_ref, mu_ref, w0_ref, w2_ref, a0_ref, a2_ref, g2_ref,
                 kk_w_ref, ka_ref, rk_ref,
                 r_ref, v_ref, kk_ref, wf_ref, wb_ref, kf_ref, kb_ref, bf_ref, bb_ref,
                 bonus_ref, g_ref):
    i = pl.program_id(0)
    tm = p_ref.shape[0]
    j = jnp.where(i < n_ctx_tiles, i % tiles_ctx, (i - n_ctx_tiles) % tiles_lat)
    per_seq = jnp.where(i < n_ctx_tiles, tiles_ctx, tiles_lat)
    x = p_ref[...]
    rowid = jax.lax.broadcasted_iota(jnp.int32, (tm, 1), 0)
    prev_row = jnp.where(j == 0, 0.0, prev_ref[SUBLANES - 1:SUBLANES, :])
    next_row = jnp.where(j == per_seq - 1, 0.0, next_ref[0:1, :])
    prev = jnp.where(rowid == 0, prev_row, pltpu.roll(x, 1, axis=0))
    nxt = jnp.where(rowid == tm - 1, next_row, pltpu.roll(x, tm - 1, axis=0))
    p = x + mu_ref[0:1, :] * (prev - x) + mu_ref[1:2, :] * (nxt - x)

    d = D_RWKV
    r, k, v = p[:, 0:d], p[:, d:2 * d], p[:, 2 * d:3 * d]
    low_w = p[:, 3 * d:3 * d + LANES]
    low_a = p[:, 3 * d + LANES:3 * d + 2 * LANES]
    low_g = p[:, 3 * d + 2 * LANES:3 * d + 3 * LANES]
    ones = _head_ones()

    kk = k * kk_w_ref[...]
    kk = kk * jax.lax.rsqrt(jnp.maximum(_head_sum(kk * kk, ones), 1e-24))
    lw = _dot3(jnp.tanh(low_w), w2_ref[...])
    la = _dot3(low_a, a2_ref[...])
    ksum = jnp.zeros_like(k)
    for dirn, (w_out, k_out, b_out) in enumerate(((wf_ref, kf_ref, bf_ref), (wb_ref, kb_ref, bb_ref))):
        log_w = -DECAY_SCALE * _sigmoid(w0_ref[dirn:dirn + 1, :] + lw[:, dirn * d:(dirn + 1) * d])
        a = _sigmoid(a0_ref[dirn:dirn + 1, :] + la[:, dirn * d:(dirn + 1) * d])
        k_d = k * (1.0 + (a - 1.0) * ka_ref[...])
        w_out[...] = jnp.exp(log_w)
        k_out[...] = k_d
        b_out[...] = kk * a
        ksum = ksum + k_d
    r_ref[...] = r
    v_ref[...] = v
    kk_ref[...] = kk
    bonus_ref[...] = _head_sum(r * (0.5 * ksum) * rk_ref[...], ones) * v
    g_ref[...] = _dot3(_sigmoid(low_g), g2_ref[...])


def _prep(p_r, lp, tiles_ctx, tiles_lat, n_ctx_tiles):
    n = p_r.shape[0]
    tm = ROW_TILE
    halo = tm // SUBLANES
    n_halo = n // SUBLANES
    full = lambda a: pl.BlockSpec(a.shape, lambda i: (0,) * a.ndim)
    params = [lp[k] for k in ("mu", "w0", "w2cat", "a0", "a2cat", "g2", "k_k", "k_a", "r_k")]
    out = jax.ShapeDtypeStruct((n, D_RWKV), F32)
    return pl.pallas_call(
        functools.partial(_prep_kernel, tiles_ctx, tiles_lat, n_ctx_tiles),
        grid=(n // tm,),
        in_specs=[
            pl.BlockSpec((tm, D_RWKV_PROJ), lambda i: (i, 0)),
            pl.BlockSpec((SUBLANES, D_RWKV_PROJ), lambda i: (jnp.maximum(i * halo - 1, 0), 0)),
            pl.BlockSpec((SUBLANES, D_RWKV_PROJ), lambda i: (jnp.minimum((i + 1) * halo, n_halo - 1), 0)),
        ] + [full(a) for a in params],
        out_specs=[pl.BlockSpec((tm, D_RWKV), lambda i: (i, 0))] * 11,
        out_shape=[out] * 11,
        compiler_params=_cparams("arbitrary"),
        name="rwkv_prep",
    )(p_r, p_r, p_r, *params)


def _tail_kernel(alpha, yf_ref, yb_ref, bonus_ref, g_ref, yh_ref, gate_ref, x_ref, mod_ref,
                 gnw_ref, gnb_ref, wpa_ref, wpb_ref, wo_ref, lng_ref, lnb_ref, o_ref):
    ones = _head_ones()
    y = yf_ref[...] + yb_ref[...]
    mean = _head_sum(y, ones) * (1.0 / RWKV_HEAD)
    d = y - mean
    var = _head_sum(d * d, ones) * (1.0 / RWKV_HEAD)
    y_n = d * jax.lax.rsqrt(var + GN_EPS) * gnw_ref[...] + gnb_ref[...]
    y_a = ((y_n + bonus_ref[...]) * g_ref[...]).astype(BF16)
    merged = (_sigmoid(gate_ref[:, 0:D_MODEL]) * _dot(y_a, wpa_ref[...])
              + _sigmoid(gate_ref[:, D_MODEL:2 * D_MODEL]) * _dot(yh_ref[...].astype(BF16), wpb_ref[...]))
    m = _dot(merged.astype(BF16), wo_ref[...])
    z = alpha * x_ref[...] + mod_ref[2:3, :] * m
    o_ref[...] = _layer_norm(z, lng_ref[...], lnb_ref[...])


def _tail(y_f, y_b, bonus, g, y_h, gates, x, mod, lp, mod_map, alpha):
    n = x.shape[0]
    tm = ROW_TILE
    row = lambda width: pl.BlockSpec((tm, width), lambda i: (i, 0))
    full = lambda a: pl.BlockSpec(a.shape, lambda i: (0,) * a.ndim)
    params = [lp[k] for k in ("gn_w", "gn_b", "w_pa", "w_pb", "w_o", "ln_g1", "ln_b1")]
    return pl.pallas_call(
        functools.partial(_tail_kernel, alpha),
        grid=(n // tm,),
        in_specs=[row(D_RWKV)] * 5 + [row(2 * D_MODEL), row(D_MODEL),
                                      pl.BlockSpec((None, N_MOD, D_MODEL), mod_map)]
        + [full(a) for a in params],
        out_specs=row(D_MODEL),
        out_shape=jax.ShapeDtypeStruct((n, D_MODEL), F32),
        compiler_params=_cparams("arbitrary"),
        name="mixer_tail",
    )(y_f, y_b, bonus, g, y_h, gates, x, mod, *params)


def _ffn_kernel(routed, alpha, x_ref, mod_ref, wg_ref, wu_ref, wd_ref, rw_ref, rb_ref,
                lng_ref, lnb_ref, o_ref, h_ref, acc_ref, comb_ref):
    e = pl.program_id(1)
    lane = jax.lax.broadcasted_iota(jnp.int32, comb_ref.shape, 1)

    @pl.when(e == 0)
    def _():
        h = x_ref[...] * (1.0 + mod_ref[4:5, :]) + mod_ref[3:4, :]
        h_ref[...] = h.astype(BF16)
        acc_ref[...] = jnp.zeros_like(acc_ref)
        if routed:
            logits = _dot3(h, rw_ref[...]) + rb_ref[...]
            logits = jnp.where(lane < N_EXPERTS, logits, -jnp.inf)
            ex = jnp.exp(logits - jnp.max(logits, axis=-1, keepdims=True))
            probs = ex / jnp.sum(ex, axis=-1, keepdims=True)
            p1 = jnp.max(probs, axis=-1, keepdims=True)
            i1 = jnp.min(jnp.where(probs == p1, lane, PAD), axis=-1, keepdims=True)
            rest = jnp.where(lane == i1, -1.0, probs)
            p2 = jnp.max(rest, axis=-1, keepdims=True)
            i2 = jnp.min(jnp.where(rest == p2, lane, PAD), axis=-1, keepdims=True)
            total = p1 + p2
            comb_ref[...] = jnp.where(lane == i1, p1 / total, 0.0) + jnp.where(lane == i2, p2 / total, 0.0)

    h = h_ref[...]
    act = _silu(_dot(h, wg_ref[...])) * _dot(h, wu_ref[...])
    if routed:
        act = act * jnp.sum(jnp.where(lane == e, comb_ref[...], 0.0), axis=-1, keepdims=True)
    acc_ref[...] += _dot(act.astype(BF16), wd_ref[...])

    @pl.when(e == pl.num_programs(1) - 1)
    def _():
        z = alpha * x_ref[...] + mod_ref[5:6, :] * acc_ref[...]
        o_ref[...] = _layer_norm(z, lng_ref[...], lnb_ref[...])


def _ffn(x, mod, fp, mod_map, alpha):
    n = x.shape[0]
    tm = 512
    scale = tm // ROW_TILE
    routed = fp["routed"]
    tf = D_FF_EXPERT
    if routed:
        groups = N_EXPERTS
        wg_spec = pl.BlockSpec((None, D_MODEL, tf), lambda i, e: (e, 0, 0))
        wu_spec = pl.BlockSpec((None, D_MODEL, tf), lambda i, e: (e, 0, 1))
        wd_spec = pl.BlockSpec((None, tf, D_MODEL), lambda i, e: (e, 0, 0))
    else:
        groups = D_FF // tf
        wg_spec = pl.BlockSpec((D_MODEL, tf), lambda i, e: (0, e))
        wu_spec = pl.BlockSpec((D_MODEL, tf), lambda i, e: (0, e + groups))
        wd_spec = pl.BlockSpec((tf, D_MODEL), lambda i, e: (e, 0))
    full = lambda a: pl.BlockSpec(a.shape, lambda i, e: (0,) * a.ndim)
    row = pl.BlockSpec((tm, D_MODEL), lambda i, e: (i, 0))
    mod_spec = pl.BlockSpec((None, N_MOD, D_MODEL), lambda i, e: mod_map(i * scale))
    return pl.pallas_call(
        functools.partial(_ffn_kernel, routed, alpha),
        grid=(n // tm, groups),
        in_specs=[row, mod_spec, wg_spec, wu_spec, wd_spec, full(fp["router_w"]), full(fp["router_b"]),
                  full(fp["ln_g"]), full(fp["ln_b"])],
        out_specs=row,
        out_shape=jax.ShapeDtypeStruct((n, D_MODEL), F32),
        scratch_shapes=[pltpu.VMEM((tm, D_MODEL), BF16), pltpu.VMEM((tm, D_MODEL), F32),
                        pltpu.VMEM((tm, PAD), F32)],
        compiler_params=_cparams("arbitrary", "arbitrary"),
        name="moe_ffn" if routed else "dense_ffn",
    )(x, mod, fp["w_in"], fp["w_in"], fp["w_out"], fp["router_w"], fp["router_b"], fp["ln_g"], fp["ln_b"])


def _all_sum(x):
    t = jnp.sum(x, axis=0)
    t = t + pltpu.roll(t, 4, axis=0)
    t = t + pltpu.roll(t, 2, axis=0)
    return t + pltpu.roll(t, 1, axis=0)


def _scan_kernel(r_ref, w_ref, k_ref, v_ref, kk_ref, b_ref, s0_ref, y_ref, sfin_ref, s_ref):
    tb = r_ref.shape[0]
    blk = (SUBLANES, SUBLANES, LANES)

    @pl.when(pl.program_id(1) == 0)
    def _():
        s_ref[...] = s0_ref[...]

    def rows_body(q, carry):
        base = q * SCAN_ROWS

        def step(t, state):
            kap = kk_ref[t].reshape(blk)
            w = w_ref[t].reshape(blk)
            b = b_ref[t].reshape(blk)
            kd = k_ref[t].reshape(blk)
            r = r_ref[t].reshape(blk)
            new = []
            for j in range(SCAN_ROWS):
                s = state[j]
                sa = _all_sum(s * kap)
                vrow = jnp.broadcast_to(v_ref[t, pl.ds(base + j, 1), :], (SUBLANES, LANES))
                s = s * w - sa[None] * b + vrow[None] * kd
                y = _all_sum(s * r)
                y_ref[t, pl.ds(base + j, 1), :] = y[0:1]
                new.append(s)
            return tuple(new)

        init = tuple(s_ref[base + j].reshape(blk) for j in range(SCAN_ROWS))
        fin = jax.lax.fori_loop(0, tb, step, init)
        for j in range(SCAN_ROWS):
            s_ref[base + j] = fin[j].reshape(RWKV_HEAD, LANES)
        return carry

    jax.lax.fori_loop(0, RWKV_HEAD // SCAN_ROWS, rows_body, 0)

    @pl.when(pl.program_id(1) == pl.num_programs(1) - 1)
    def _():
        sfin_ref[...] = s_ref[...]


def _scan(r, w, k, v, kk, b, s0):
    t_len, hd, n = r.shape
    tb = SCAN_TBLOCK
    seq = pl.BlockSpec((tb, hd, LANES), lambda g, t: (t, 0, g))
    st = pl.BlockSpec((hd, hd, LANES), lambda g, t: (0, 0, g))
    return pl.pallas_call(
        _scan_kernel,
        grid=(n // LANES, t_len // tb),
        in_specs=[seq] * 6 + [st],
        out_specs=[seq, st],
        out_shape=[jax.ShapeDtypeStruct((t_len, hd, n), F32), jax.ShapeDtypeStruct((hd, hd, n), F32)],
        scratch_shapes=[pltpu.VMEM((hd, hd, LANES), F32)],
        compiler_params=_cparams("arbitrary", "arbitrary"),
        name="wkv_scan",
    )(r, w, k, v, kk, b, s0)


def _to_chains(x, n_seq, seq_len):
    x = x.reshape(n_seq, seq_len, N_RWKV_HEADS, RWKV_HEAD)
    return jnp.transpose(x, (1, 3, 0, 2)).reshape(seq_len, RWKV_HEAD, n_seq * N_RWKV_HEADS)


def _from_chains(y, n_seq, seq_len):
    y = y.reshape(seq_len, RWKV_HEAD, n_seq, N_RWKV_HEADS)
    return jnp.transpose(y, (2, 0, 3, 1)).reshape(n_seq * seq_len, D_RWKV)


def _both_dirs(fwd, bwd):
    return jnp.concatenate([fwd, jnp.flip(bwd, axis=0)], axis=-1)


def _wkv_group(prep, s0, n_seq, seq_len):
    r, v, kk, w_f, w_b, k_f, k_b, b_f, b_b = [_to_chains(a, n_seq, seq_len) for a in prep]
    n_half = n_seq * N_RWKV_HEADS
    s0c = jnp.transpose(s0, (3, 4, 1, 0, 2)).reshape(RWKV_HEAD, RWKV_HEAD, 2 * n_half)
    y, s_fin = _scan(_both_dirs(r, r), _both_dirs(w_f, w_b), _both_dirs(k_f, k_b), _both_dirs(v, v),
                     _both_dirs(kk, kk), _both_dirs(b_f, b_b), s0c)
    y_f = _from_chains(y[..., :n_half], n_seq, seq_len)
    y_b = _from_chains(jnp.flip(y[..., n_half:], axis=0), n_seq, seq_len)
    s_fin = s_fin.reshape(RWKV_HEAD, RWKV_HEAD, 2, n_seq, N_RWKV_HEADS)
    return y_f, y_b, jnp.transpose(s_fin, (3, 2, 4, 0, 1))


@functools.lru_cache(maxsize=None)
def _dft_constants(seq_len):
    n = 2 * seq_len
    idx = np.arange(seq_len)
    ang = (2.0 * np.pi / n) * ((idx[:, None] * idx[None, :]) % n)
    alt = np.where(idx % 2 == 0, 1.0, -1.0)
    f_re = np.cos(ang)
    f_im = -np.sin(ang)
    f_im[0, :] = alt
    fwd = np.concatenate([f_re, f_im], axis=0)
    c = np.full((seq_len,), 2.0)
    c[0] = 1.0
    g_re = np.cos(ang.T) * c[None, :] / n
    g_im = -2.0 * np.sin(ang.T) / n
    g_im[:, 0] = alt / n
    inv = np.concatenate([g_re, g_im], axis=1)

    def split(m):
        hi = m.astype(ml_dtypes.bfloat16)
        lo = (m - hi.astype(np.float64)).astype(ml_dtypes.bfloat16)
        return hi, lo

    return split(fwd) + split(inv)


@functools.lru_cache(maxsize=None)
def _filter_constants(seq_len):
    t = np.linspace(0.0, 1.0, seq_len)[:, None]
    f = np.linspace(1e-4, HYENA_BANDS - 1, HYENA_BANDS)
    ang = (2.0 * np.pi / seq_len) * np.arange(seq_len)[:, None] * f
    feats = np.zeros((seq_len, PAD), np.float32)
    feats[:, :HYENA_EMB] = np.concatenate([t, np.cos(ang), -np.sin(ang)], axis=-1)
    deltas = np.abs(np.linspace(math.log(HYENA_TARGET) / HYENA_SLOW_DECAY,
                                math.log(HYENA_TARGET) / HYENA_FAST_DECAY, D_HYENA))
    decay = np.exp(-t * deltas).astype(np.float32)
    return feats, np.tile(decay, (1, 2 * HYENA_ORDER))


def _filter_kernel(feats_ref, decay_ref, f1_ref, b1_ref, f2_ref, b2_ref, f3_ref, h_ref):
    hid = jnp.sin(_dot3(feats_ref[...], f1_ref[...]) + b1_ref[...])
    hid = jnp.sin(_dot3(hid, f2_ref[...]) + b2_ref[...])
    h_ref[...] = _dot3(hid, f3_ref[...]) * decay_ref[...]


def _spectrum_kernel(fh_ref, fl_ref, hf_ref, hb_ref, kf_ref):
    seq_len = hf_ref.shape[0]
    row = jax.lax.broadcasted_iota(jnp.int32, (seq_len, 1), 0)
    h_f = hf_ref[...]
    h_b = jnp.where(row == 0, 0.0, hb_ref[...])
    fh, fl = fh_ref[...], fl_ref[...]
    a = _dot3_lhs_split(fh, fl, h_f)
    b = _dot3_lhs_split(fh, fl, h_b)
    kf_ref[0:seq_len, :] = a[0:seq_len] + b[0:seq_len]
    kf_ref[seq_len:, :] = jnp.where(row == 0, a[seq_len:] + b[seq_len:], a[seq_len:] - b[seq_len:])


def _hyena_filters(seq_len, lp):
    feats, decay = _filter_constants(seq_len)
    fh, fl, _, _ = _dft_constants(seq_len)
    args = [jnp.asarray(feats), jnp.asarray(decay), lp["hy_f1"], lp["hy_fb1"], lp["hy_f2"], lp["hy_fb2"],
            lp["hy_f3"]]
    h = pl.pallas_call(
        _filter_kernel,
        out_shape=jax.ShapeDtypeStruct((seq_len, 2 * HYENA_ORDER * D_HYENA), F32),
        compiler_params=_cparams(),
        name="hyena_filter_mlp",
    )(*args)
    tc = HY_COL_TILE
    per = D_HYENA // tc
    const = pl.BlockSpec((2 * seq_len, seq_len), lambda n, c: (0, 0), pipeline_mode=pl.Buffered(1))
    return pl.pallas_call(
        _spectrum_kernel,
        grid=(HYENA_ORDER, per),
        in_specs=[const, const,
                  pl.BlockSpec((seq_len, tc), lambda n, c: (0, (2 * n) * per + c)),
                  pl.BlockSpec((seq_len, tc), lambda n, c: (0, (2 * n + 1) * per + c))],
        out_specs=pl.BlockSpec((2 * seq_len, tc), lambda n, c: (0, n * per + c)),
        out_shape=jax.ShapeDtypeStruct((2 * seq_len, HYENA_ORDER * D_HYENA), F32),
        compiler_params=_cparams("arbitrary", "arbitrary"),
        name="hyena_filter_spectrum",
    )(jnp.asarray(fh), jnp.asarray(fl), h, h)


def _hyena_kernel(pz_ref, pg1_ref, pg2_ref, cwz_ref, cwg1_ref, cwg2_ref, cbz_ref, cbg1_ref, cbg2_ref,
                  kf0_ref, kf1_ref, skip_ref, fh_ref, fl_ref, gh_ref, gl_ref, o_ref):
    seq_len = pz_ref.shape[0]
    row = jax.lax.broadcasted_iota(jnp.int32, (seq_len, 1), 0)

    def short_conv(p_ref, cw_ref, cb_ref):
        x = p_ref[...]
        prev = jnp.where(row == 0, 0.0, pltpu.roll(x, 1, axis=0))
        nxt = jnp.where(row == seq_len - 1, 0.0, pltpu.roll(x, seq_len - 1, axis=0))
        return cw_ref[0:1, :] * prev + cw_ref[1:2, :] * x + cw_ref[2:3, :] * nxt + cb_ref[...]

    z = short_conv(pz_ref, cwz_ref, cbz_ref)
    gates = (short_conv(pg1_ref, cwg1_ref, cbg1_ref), short_conv(pg2_ref, cwg2_ref, cbg2_ref))
    for n, kf_ref in enumerate((kf0_ref, kf1_ref)):
        zf = _dot3_lhs_split(fh_ref[...], fl_ref[...], z)
        z_re, z_im = zf[0:seq_len], zf[seq_len:]
        k_re, k_im = kf_ref[0:seq_len, :], kf_ref[seq_len:, :]
        p_re = jnp.where(row == 0, z_re * k_re, z_re * k_re - z_im * k_im)
        p_im = jnp.where(row == 0, z_im * k_im, z_re * k_im + z_im * k_re)
        conv = (_dot3_lhs_split(gh_ref[:, 0:seq_len], gl_ref[:, 0:seq_len], p_re)
                + _dot3_lhs_split(gh_ref[:, seq_len:], gl_ref[:, seq_len:], p_im))
        z = gates[n] * (conv + skip_ref[n:n + 1, :] * z)
    o_ref[...] = z


def _hyena(p_h, kf, lp, n_seq, seq_len, seq_offset):
    tc = HY_COL_TILE
    per = D_HYENA // tc
    fh, fl, gh, gl = [jnp.asarray(a) for a in _dft_constants(seq_len)]
    seg = lambda s: pl.BlockSpec((seq_len, tc), lambda b, c: (seq_offset + b, s * per + c))
    par = lambda rows, s: pl.BlockSpec((rows, tc), lambda b, c: (0, s * per + c))
    fconst = pl.BlockSpec((2 * seq_len, seq_len), lambda b, c: (0, 0), pipeline_mode=pl.Buffered(1))
    gconst = pl.BlockSpec((seq_len, 2 * seq_len), lambda b, c: (0, 0), pipeline_mode=pl.Buffered(1))
    cw, cb = lp["hy_conv_w"], lp["hy_conv_b"]
    return pl.pallas_call(
        _hyena_kernel,
        grid=(n_seq, per),
        in_specs=[seg(0), seg(1), seg(2), par(3, 0), par(3, 1), par(3, 2), par(1, 0), par(1, 1), par(1, 2),
                  pl.BlockSpec((2 * seq_len, tc), lambda b, c: (0, c)),
                  pl.BlockSpec((2 * seq_len, tc), lambda b, c: (0, per + c)),
                  pl.BlockSpec((HYENA_ORDER, tc), lambda b, c: (0, c)),
                  fconst, fconst, gconst, gconst],
        out_specs=pl.BlockSpec((seq_len, tc), lambda b, c: (b, c)),
        out_shape=jax.ShapeDtypeStruct((n_seq * seq_len, D_HYENA), F32),
        compiler_params=_cparams("arbitrary", "arbitrary"),
        name="hyena_conv",
    )(p_h, p_h, p_h, cw, cw, cw, cb, cb, cb, kf, kf, lp["hy_skip"], fh, fl, gh, gl)


def _grid_pos_embed(n_tokens):
    rows = n_tokens // GRID_W
    row = jnp.repeat(jnp.arange(rows, dtype=F32), GRID_W)
    col = jnp.tile(jnp.arange(GRID_W, dtype=F32), rows)
    quarter = D_MODEL // 4
    omega = 1.0 / (POS_BASE ** (jnp.arange(quarter, dtype=F32) / quarter))

    def enc(pos):
        ang = pos[:, None] * omega
        return jnp.concatenate([jnp.sin(ang), jnp.cos(ang)], axis=-1)

    return jnp.concatenate([enc(row), enc(col)], axis=-1)


def _block_diag2(m):
    z = jnp.zeros_like(m[0])
    return jnp.concatenate([jnp.concatenate([m[0], z], axis=1), jnp.concatenate([z, m[1]], axis=1)], axis=0)


def _pad_to(a, rows, cols):
    return jnp.pad(a, ((0, rows - a.shape[0]), (0, cols - a.shape[1])))


def kernel(x_prompt, x_sample, c, state_wkv, c_ctx, w_ada, b_ada, w_in, rwkv_mu, rwkv_w0, rwkv_w2, rwkv_a0, rwkv_a2, rwkv_g2, rwkv_k_k, rwkv_k_a, rwkv_r_k, rwkv_gn_w, rwkv_gn_b, hy_conv_w, hy_conv_b, hy_f1, hy_fb1, hy_f2, hy_fb2, hy_f3, hy_skip, w_pa, w_pb, w_o, ln_g, ln_b, ffn_w_in, ffn_w_out, router_w, router_b, exp_w_in, exp_w_out):
    batch, seq, d = x_prompt.shape
    dec_batch, dec_seq, _ = x_sample.shape
    depth = w_in.shape[0]
    alpha = (2 * depth) ** 0.25
    n_ctx, n_lat = batch * seq, dec_batch * dec_seq
    assert seq % ROW_TILE == 0 and dec_seq % ROW_TILE == 0 and n_ctx % dec_seq == 0
    assert n_ctx % 512 == 0 and n_lat % 512 == 0 and dec_seq % 512 == 0
    tiles_ctx, tiles_lat, n_ctx_tiles = seq // ROW_TILE, dec_seq // ROW_TILE, n_ctx // ROW_TILE
    mod_map = _mod_row_map(n_ctx_tiles, tiles_lat, dec_batch)

    x = jnp.concatenate([x_prompt.reshape(n_ctx, d),
                         (x_sample + _grid_pos_embed(dec_seq)[None]).reshape(n_lat, d)], axis=0)
    cond_rows = -(-(dec_batch + 1) // SUBLANES) * SUBLANES
    cond = jnp.zeros((cond_rows, d), F32).at[:dec_batch].set(c).at[dec_batch].set(c_ctx)
    mods = _adaln(cond, w_ada, b_ada).reshape(depth, cond_rows, N_MOD, d)

    s0_ctx = jnp.zeros((batch, 2, N_RWKV_HEADS, RWKV_HEAD, RWKV_HEAD), F32)
    ctx_states = []
    for l in range(depth):
        lp = {
            "mu": rwkv_mu[l], "w0": rwkv_w0[l], "w2cat": _block_diag2(rwkv_w2[l]), "a0": rwkv_a0[l],
            "a2cat": _block_diag2(rwkv_a2[l]), "g2": rwkv_g2[l], "k_k": rwkv_k_k[l][None],
            "k_a": rwkv_k_a[l][None], "r_k": rwkv_r_k[l].reshape(1, D_RWKV),
            "gn_w": rwkv_gn_w[l][None], "gn_b": rwkv_gn_b[l][None],
            "w_pa": w_pa[l].astype(BF16), "w_pb": w_pb[l].astype(BF16), "w_o": w_o[l].astype(BF16),
            "ln_g1": ln_g[l, 0][None], "ln_b1": ln_b[l, 0][None],
            "hy_conv_w": hy_conv_w[l], "hy_conv_b": hy_conv_b[l][None], "hy_skip": hy_skip[l],
            "hy_f1": _pad_to(hy_f1[l], PAD, PAD), "hy_fb1": _pad_to(hy_fb1[l][None], 1, PAD),
            "hy_f2": _pad_to(hy_f2[l], PAD, PAD), "hy_fb2": _pad_to(hy_fb2[l][None], 1, PAD),
            "hy_f3": _pad_to(hy_f3[l], PAD, 2 * HYENA_ORDER * D_HYENA),
        }
        if l % 2 == 0:
            fp = {"routed": False, "w_in": ffn_w_in[l // 2].astype(BF16), "w_out": ffn_w_out[l // 2].astype(BF16),
                  "router_w": jnp.zeros((d, PAD), F32), "router_b": jnp.zeros((1, PAD), F32)}
        else:
            fp = {"routed": True, "w_in": exp_w_in[l // 2].astype(BF16), "w_out": exp_w_out[l // 2].astype(BF16),
                  "router_w": _pad_to(router_w[l // 2], d, PAD), "router_b": _pad_to(router_b[l // 2][None], 1, PAD)}
        fp["ln_g"], fp["ln_b"] = ln_g[l, 1][None], ln_b[l, 1][None]
        mod = mods[l]

        p_r, p_h, gates = _inproj(x, mod, w_in[l].astype(BF16), mod_map)
        prep = _prep(p_r, lp, tiles_ctx, tiles_lat, n_ctx_tiles)
        scan_ops, bonus, g = prep[:9], prep[9], prep[10]
        yf_c, yb_c, s_ctx = _wkv_group([a[:n_ctx] for a in scan_ops], s0_ctx, batch, seq)
        yf_l, yb_l, _ = _wkv_group([a[n_ctx:] for a in scan_ops], state_wkv[:, l], dec_batch, dec_seq)
        ctx_states.append(s_ctx)
        y_f = jnp.concatenate([yf_c, yf_l], axis=0)
        y_b = jnp.concatenate([yb_c, yb_l], axis=0)

        kf_ctx = _hyena_filters(seq, lp)
        kf_lat = kf_ctx if dec_seq == seq else _hyena_filters(dec_seq, lp)
        y_h = jnp.concatenate([_hyena(p_h, kf_ctx, lp, batch, seq, 0),
                               _hyena(p_h, kf_lat, lp, dec_batch, dec_seq, n_ctx // dec_seq)], axis=0)

        x = _tail(y_f, y_b, bonus, g, y_h, gates, x, mod, lp, mod_map, alpha)
        x = _ffn(x, mod, fp, mod_map, alpha)

    y_prompt = x[:n_ctx].reshape(batch, seq, d)
    y_sample = x[n_ctx:].reshape(dec_batch, dec_seq, d)
    return y_prompt, y_sample, jnp.stack(ctx_states, axis=1)
```

```python
import functools
import math

import jax
import jax.numpy as jnp
import ml_dtypes
import numpy as np
from jax.experimental import pallas as pl
from jax.experimental.pallas import tpu as pltpu

F32 = jnp.float32
BF16 = jnp.bfloat16

D_MODEL = 1024
GRID_W = 64
D_RWKV = 512
RWKV_HEAD = 64
N_RWKV_HEADS = D_RWKV // RWKV_HEAD
LORA_W = 64
LORA_A = 64
LORA_G = 128
D_RWKV_PROJ = 3 * D_RWKV + 2 * LORA_W + 2 * LORA_A + LORA_G
DECAY_SCALE = math.exp(-0.5)
GN_EPS = 64e-5
D_HYENA = 512
HYENA_ORDER = 2
HYENA_EMB = 33
HYENA_BANDS = (HYENA_EMB - 1) // 2
HYENA_HIDDEN = 64
HYENA_FAST_DECAY = 0.3
HYENA_SLOW_DECAY = 1.5
HYENA_TARGET = 1e-2
D_HYENA_PROJ = (HYENA_ORDER + 1) * D_HYENA
D_IN_PROJ = D_RWKV_PROJ + D_HYENA_PROJ + 2 * D_MODEL
D_FF = 2816
N_EXPERTS = 8
TOP_K = 2
D_FF_EXPERT = 1408
N_MOD = 6
LN_EPS = 1e-5
POS_BASE = 10000.0

LANES = 128
SUBLANES = 8
VMEM_LIMIT = 56 * 1024 * 1024

ROW_TILE = 256
HY_COL_TILE = 256
SCAN_VBLOCKS = 4
SCAN_TBLOCK = 32
N_SCAN_SRC = 9
RELAYOUT_TBLOCK = LANES
PAD = LANES


def _cparams(*sem):
    return pltpu.CompilerParams(dimension_semantics=sem, vmem_limit_bytes=VMEM_LIMIT)


def _dot(a, b):
    return jnp.dot(a, b, preferred_element_type=F32)


def _split2(x):
    hi = x.astype(BF16)
    lo = (x - hi.astype(F32)).astype(BF16)
    return hi, lo


def _dot3(a, b):
    ah, al = _split2(a)
    bh, bl = _split2(b)
    return _dot(ah, bh) + _dot(al, bh) + _dot(ah, bl)


def _dot3_lhs_split(ah, al, b):
    bh, bl = _split2(b)
    return _dot(ah, bh) + _dot(al, bh) + _dot(ah, bl)


def _sigmoid(x):
    return 1.0 / (1.0 + jnp.exp(-x))


def _silu(x):
    return x * _sigmoid(x)


def _head_ones():
    i = jax.lax.broadcasted_iota(jnp.int32, (D_RWKV, D_RWKV), 0) // RWKV_HEAD
    j = jax.lax.broadcasted_iota(jnp.int32, (D_RWKV, D_RWKV), 1) // RWKV_HEAD
    return jnp.where(i == j, 1.0, 0.0).astype(BF16)


def _head_sum(x, ones):
    h1 = x.astype(BF16)
    r1 = x - h1.astype(F32)
    h2 = r1.astype(BF16)
    h3 = (r1 - h2.astype(F32)).astype(BF16)
    return _dot(h1, ones) + _dot(h2, ones) + _dot(h3, ones)


def _layer_norm(z, g, b):
    mean = jnp.mean(z, axis=-1, keepdims=True)
    d = z - mean
    var = jnp.mean(d * d, axis=-1, keepdims=True)
    return d * jax.lax.rsqrt(var + LN_EPS) * g + b


def _ada_kernel(c_ref, w_ref, b_ref, o_ref):
    o_ref[...] = _dot3(_silu(c_ref[...]), w_ref[...]) + b_ref[...]


def _adaln(cond, w_ada, b_ada):
    depth, d, n = w_ada.shape
    rows = cond.shape[0]
    tn = 1536
    return pl.pallas_call(
        _ada_kernel,
        grid=(depth, n // tn),
        in_specs=[
            pl.BlockSpec((rows, d), lambda l, j: (0, 0)),
            pl.BlockSpec((None, d, tn), lambda l, j: (l, 0, j)),
            pl.BlockSpec((None, 1, tn), lambda l, j: (l, 0, j)),
        ],
        out_specs=pl.BlockSpec((None, rows, tn), lambda l, j: (l, 0, j)),
        out_shape=jax.ShapeDtypeStruct((depth, rows, n), F32),
        compiler_params=_cparams("arbitrary", "arbitrary"),
        name="adaln",
    )(cond, w_ada, b_ada.reshape(depth, 1, n))


def _mod_row_map(n_ctx_tiles, tiles_per_seq, ctx_row):
    def index_map(i, *_):
        return (jnp.where(i < n_ctx_tiles, ctx_row, (i - n_ctx_tiles) // tiles_per_seq), 0, 0)

    return index_map


def _inproj_kernel(x_ref, mod_ref, w_ref, pr_ref, ph_ref, g_ref):
    h = (x_ref[...] * (1.0 + mod_ref[1:2, :]) + mod_ref[0:1, :]).astype(BF16)
    a, b = D_RWKV_PROJ, D_RWKV_PROJ + D_HYENA_PROJ
    pr_ref[...] = _dot(h, w_ref[:, 0:a])
    ph_ref[...] = _dot(h, w_ref[:, a:b])
    g_ref[...] = _dot(h, w_ref[:, b:D_IN_PROJ])


def _inproj(x, mod, w_in_bf, mod_map):
    n = x.shape[0]
    tm = ROW_TILE
    row = lambda width: pl.BlockSpec((tm, width), lambda i: (i, 0))
    return pl.pallas_call(
        _inproj_kernel,
        grid=(n // tm,),
        in_specs=[
            row(D_MODEL),
            pl.BlockSpec((None, N_MOD, D_MODEL), mod_map),
            pl.BlockSpec((D_MODEL, D_IN_PROJ), lambda i: (0, 0)),
        ],
        out_specs=[row(D_RWKV_PROJ), row(D_HYENA_PROJ), row(2 * D_MODEL)],
        out_shape=[
            jax.ShapeDtypeStruct((n, D_RWKV_PROJ), F32),
            jax.ShapeDtypeStruct((n, D_HYENA_PROJ), F32),
            jax.ShapeDtypeStruct((n, 2 * D_MODEL), F32),
        ],
        compiler_params=_cparams("arbitrary"),
        name="inproj",
    )(x, mod, w_in_bf)


def _prep_kernel(tiles_ctx, tiles_lat, n_ctx_tiles,
                 p_ref, prev_ref, next_ref, mu_ref, w0_ref, w2_ref, a0_ref, a2_ref, g2_ref,
                 kk_w_ref, ka_ref, rk_ref,
                 ops_ref, bonus_ref, g_ref):
    i = pl.program_id(0)
    tm = p_ref.shape[0]
    j = jnp.where(i < n_ctx_tiles, i % tiles_ctx, (i - n_ctx_tiles) % tiles_lat)
    per_seq = jnp.where(i < n_ctx_tiles, tiles_ctx, tiles_lat)
    x = p_ref[...]
    rowid = jax.lax.broadcasted_iota(jnp.int32, (tm, 1), 0)
    prev_row = jnp.where(j == 0, 0.0, prev_ref[SUBLANES - 1:SUBLANES, :])
    next_row = jnp.where(j == per_seq - 1, 0.0, next_ref[0:1, :])
    prev = jnp.where(rowid == 0, prev_row, pltpu.roll(x, 1, axis=0))
    nxt = jnp.where(rowid == tm - 1, next_row, pltpu.roll(x, tm - 1, axis=0))
    p = x + mu_ref[0:1, :] * (prev - x) + mu_ref[1:2, :] * (nxt - x)

    d = D_RWKV
    r, k, v = p[:, 0:d], p[:, d:2 * d], p[:, 2 * d:3 * d]
    low_w = p[:, 3 * d:3 * d + LANES]
    low_a = p[:, 3 * d + LANES:3 * d + 2 * LANES]
    low_g = p[:, 3 * d + 2 * LANES:3 * d + 3 * LANES]
    ones = _head_ones()

    kk = k * kk_w_ref[...]
    kk = kk * jax.lax.rsqrt(jnp.maximum(_head_sum(kk * kk, ones), 1e-24))
    lw = _dot3(jnp.tanh(low_w), w2_ref[...])
    la = _dot3(low_a, a2_ref[...])
    ksum = jnp.zeros_like(k)
    for dirn in range(2):
        log_w = -DECAY_SCALE * _sigmoid(w0_ref[dirn:dirn + 1, :] + lw[:, dirn * d:(dirn + 1) * d])
        a = _sigmoid(a0_ref[dirn:dirn + 1, :] + la[:, dirn * d:(dirn + 1) * d])
        k_d = k * (1.0 + (a - 1.0) * ka_ref[...])
        ops_ref[3 + 3 * dirn] = jnp.exp(log_w)
        ops_ref[4 + 3 * dirn] = k_d
        ops_ref[5 + 3 * dirn] = kk * a
        ksum = ksum + k_d
    ops_ref[0] = r
    ops_ref[1] = v
    ops_ref[2] = kk
    bonus_ref[...] = _head_sum(r * (0.5 * ksum) * rk_ref[...], ones) * v
    g_ref[...] = _dot3(_sigmoid(low_g), g2_ref[...])


def _prep(p_r, lp, tiles_ctx, tiles_lat, n_ctx_tiles):
    n = p_r.shape[0]
    tm = ROW_TILE
    halo = tm // SUBLANES
    n_halo = n // SUBLANES
    full = lambda a: pl.BlockSpec(a.shape, lambda i: (0,) * a.ndim)
    params = [lp[k] for k in ("mu", "w0", "w2cat", "a0", "a2cat", "g2", "k_k", "k_a", "r_k")]
    out = jax.ShapeDtypeStruct((n, D_RWKV), F32)
    return pl.pallas_call(
        functools.partial(_prep_kernel, tiles_ctx, tiles_lat, n_ctx_tiles),
        grid=(n // tm,),
        in_specs=[
            pl.BlockSpec((tm, D_RWKV_PROJ), lambda i: (i, 0)),
            pl.BlockSpec((SUBLANES, D_RWKV_PROJ), lambda i: (jnp.maximum(i * halo - 1, 0), 0)),
            pl.BlockSpec((SUBLANES, D_RWKV_PROJ), lambda i: (jnp.minimum((i + 1) * halo, n_halo - 1), 0)),
        ] + [full(a) for a in params],
        out_specs=[pl.BlockSpec((N_SCAN_SRC, tm, D_RWKV), lambda i: (0, i, 0))]
        + [pl.BlockSpec((tm, D_RWKV), lambda i: (i, 0))] * 2,
        out_shape=[jax.ShapeDtypeStruct((N_SCAN_SRC, n, D_RWKV), F32), out, out],
        compiler_params=_cparams("arbitrary"),
        name="rwkv_prep",
    )(p_r, p_r, p_r, *params)


def _tail_kernel(alpha, yf_ref, yb_ref, bonus_ref, g_ref, yh_ref, gate_ref, x_ref, mod_ref,
                 gnw_ref, gnb_ref, wpa_ref, wpb_ref, wo_ref, lng_ref, lnb_ref, o_ref):
    ones = _head_ones()
    y = yf_ref[...] + yb_ref[...]
    mean = _head_sum(y, ones) * (1.0 / RWKV_HEAD)
    d = y - mean
    var = _head_sum(d * d, ones) * (1.0 / RWKV_HEAD)
    y_n = d * jax.lax.rsqrt(var + GN_EPS) * gnw_ref[...] + gnb_ref[...]
    y_a = ((y_n + bonus_ref[...]) * g_ref[...]).astype(BF16)
    merged = (_sigmoid(gate_ref[:, 0:D_MODEL]) * _dot(y_a, wpa_ref[...])
              + _sigmoid(gate_ref[:, D_MODEL:2 * D_MODEL]) * _dot(yh_ref[...].astype(BF16), wpb_ref[...]))
    m = _dot(merged.astype(BF16), wo_ref[...])
    z = alpha * x_ref[...] + mod_ref[2:3, :] * m
    o_ref[...] = _layer_norm(z, lng_ref[...], lnb_ref[...])


def _tail(y_f, y_b, bonus, g, y_h, gates, x, mod, lp, mod_map, alpha):
    n = x.shape[0]
    tm = ROW_TILE
    row = lambda width: pl.BlockSpec((tm, width), lambda i: (i, 0))
    full = lambda a: pl.BlockSpec(a.shape, lambda i: (0,) * a.ndim)
    params = [lp[k] for k in ("gn_w", "gn_b", "w_pa", "w_pb", "w_o", "ln_g1", "ln_b1")]
    return pl.pallas_call(
        functools.partial(_tail_kernel, alpha),
        grid=(n // tm,),
        in_specs=[row(D_RWKV)] * 5 + [row(2 * D_MODEL), row(D_MODEL),
                                      pl.BlockSpec((None, N_MOD, D_MODEL), mod_map)]
        + [full(a) for a in params],
        out_specs=row(D_MODEL),
        out_shape=jax.ShapeDtypeStruct((n, D_MODEL), F32),
        compiler_params=_cparams("arbitrary"),
        name="mixer_tail",
    )(y_f, y_b, bonus, g, y_h, gates, x, mod, *params)


def _ffn_kernel(routed, alpha, x_ref, mod_ref, wg_ref, wu_ref, wd_ref, rw_ref, rb_ref,
                lng_ref, lnb_ref, o_ref, h_ref, acc_ref, comb_ref):
    e = pl.program_id(1)
    lane = jax.lax.broadcasted_iota(jnp.int32, comb_ref.shape, 1)

    @pl.when(e == 0)
    def _():
        h = x_ref[...] * (1.0 + mod_ref[4:5, :]) + mod_ref[3:4, :]
        h_ref[...] = h.astype(BF16)
        acc_ref[...] = jnp.zeros_like(acc_ref)
        if routed:
            logits = _dot3(h, rw_ref[...]) + rb_ref[...]
            logits = jnp.where(lane < N_EXPERTS, logits, -jnp.inf)
            ex = jnp.exp(logits - jnp.max(logits, axis=-1, keepdims=True))
            probs = ex / jnp.sum(ex, axis=-1, keepdims=True)
            p1 = jnp.max(probs, axis=-1, keepdims=True)
            i1 = jnp.min(jnp.where(probs == p1, lane, PAD), axis=-1, keepdims=True)
            rest = jnp.where(lane == i1, -1.0, probs)
            p2 = jnp.max(rest, axis=-1, keepdims=True)
            i2 = jnp.min(jnp.where(rest == p2, lane, PAD), axis=-1, keepdims=True)
            total = p1 + p2
            comb_ref[...] = jnp.where(lane == i1, p1 / total, 0.0) + jnp.where(lane == i2, p2 / total, 0.0)

    h = h_ref[...]
    act = _silu(_dot(h, wg_ref[...])) * _dot(h, wu_ref[...])
    if routed:
        act = act * jnp.sum(jnp.where(lane == e, comb_ref[...], 0.0), axis=-1, keepdims=True)
    acc_ref[...] += _dot(act.astype(BF16), wd_ref[...])

    @pl.when(e == pl.num_programs(1) - 1)
    def _():
        z = alpha * x_ref[...] + mod_ref[5:6, :] * acc_ref[...]
        o_ref[...] = _layer_norm(z, lng_ref[...], lnb_ref[...])


def _ffn(x, mod, fp, mod_map, alpha):
    n = x.shape[0]
    tm = 512
    scale = tm // ROW_TILE
    routed = fp["routed"]
    tf = D_FF_EXPERT
    if routed:
        groups = N_EXPERTS
        wg_spec = pl.BlockSpec((None, D_MODEL, tf), lambda i, e: (e, 0, 0))
        wu_spec = pl.BlockSpec((None, D_MODEL, tf), lambda i, e: (e, 0, 1))
        wd_spec = pl.BlockSpec((None, tf, D_MODEL), lambda i, e: (e, 0, 0))
    else:
        groups = D_FF // tf
        wg_spec = pl.BlockSpec((D_MODEL, tf), lambda i, e: (0, e))
        wu_spec = pl.BlockSpec((D_MODEL, tf), lambda i, e: (0, e + groups))
        wd_spec = pl.BlockSpec((tf, D_MODEL), lambda i, e: (e, 0))
    full = lambda a: pl.BlockSpec(a.shape, lambda i, e: (0,) * a.ndim)
    row = pl.BlockSpec((tm, D_MODEL), lambda i, e: (i, 0))
    mod_spec = pl.BlockSpec((None, N_MOD, D_MODEL), lambda i, e: mod_map(i * scale))
    return pl.pallas_call(
        functools.partial(_ffn_kernel, routed, alpha),
        grid=(n // tm, groups),
        in_specs=[row, mod_spec, wg_spec, wu_spec, wd_spec, full(fp["router_w"]), full(fp["router_b"]),
                  full(fp["ln_g"]), full(fp["ln_b"])],
        out_specs=row,
        out_shape=jax.ShapeDtypeStruct((n, D_MODEL), F32),
        scratch_shapes=[pltpu.VMEM((tm, D_MODEL), BF16), pltpu.VMEM((tm, D_MODEL), F32),
                        pltpu.VMEM((tm, PAD), F32)],
        compiler_params=_cparams("arbitrary", "arbitrary"),
        name="moe_ffn" if routed else "dense_ffn",
    )(x, mod, fp["w_in"], fp["w_in"], fp["w_out"], fp["router_w"], fp["router_b"], fp["ln_g"], fp["ln_b"])


def _scan_kernel(r_ref, kk_ref, w_ref, k_ref, b_ref, v_ref, s0_ref, y_ref, sfin_ref, s_ref):
    tb = r_ref.shape[1]
    n_vblocks = s_ref.shape[0]
    tile = (SUBLANES, LANES)
    backward = pl.program_id(0) % 2 == 1

    @pl.when(pl.program_id(1) == 0)
    def _():
        s_ref[...] = s0_ref[...]

    def time_of(i):
        return jnp.where(backward, tb - 1 - i, i)

    def row(ref, k, t):
        return jnp.broadcast_to(ref[k, pl.ds(t, 1), :], tile)

    for part in range(n_vblocks // SCAN_VBLOCKS):
        vbs = [part * SCAN_VBLOCKS + j for j in range(SCAN_VBLOCKS)]

        def step(i, sa, vbs=vbs):
            t = time_of(i)
            t_next = time_of(jnp.minimum(i + 1, tb - 1))
            v8 = [jnp.concatenate([v_ref[vb * SUBLANES + j, pl.ds(t, 1), :] for j in range(SUBLANES)], axis=0)
                  for vb in vbs]
            y = [jnp.zeros(tile, F32) for _ in vbs]
            sa_next = [jnp.zeros(tile, F32) for _ in vbs]
            for k in range(RWKV_HEAD):
                w, b, kd, r = row(w_ref, k, t), row(b_ref, k, t), row(k_ref, k, t), row(r_ref, k, t)
                kap = row(kk_ref, k, t_next)
                for j, vb in enumerate(vbs):
                    s = s_ref[vb, k] * w - sa[j] * b + v8[j] * kd
                    s_ref[vb, k] = s
                    y[j] = y[j] + s * r
                    sa_next[j] = sa_next[j] + s * kap
            for j, vb in enumerate(vbs):
                y_ref[t, pl.ds(vb * SUBLANES, SUBLANES), :] = y[j]
            return tuple(sa_next)

        t0 = time_of(0)
        sa0 = [jnp.zeros(tile, F32) for _ in vbs]
        for k in range(RWKV_HEAD):
            kap = row(kk_ref, k, t0)
            for j, vb in enumerate(vbs):
                sa0[j] = sa0[j] + s_ref[vb, k] * kap
        jax.lax.fori_loop(0, tb, step, tuple(sa0))

    @pl.when(pl.program_id(1) == pl.num_programs(1) - 1)
    def _():
        sfin_ref[...] = s_ref[...]


def _scan(xk, vk, s0):
    _, n_sg, hd, t_len, _ = xk.shape
    rows = vk.shape[1]
    tb = SCAN_TBLOCK
    n_t = t_len // tb
    t_of = lambda g, t: jnp.where(g % 2 == 1, n_t - 1 - t, t)
    shared = lambda o: pl.BlockSpec((None, None, hd, tb, LANES), lambda g, t: (o, g // 2, 0, t_of(g, t), 0))
    per_dir = lambda o: pl.BlockSpec((None, None, hd, tb, LANES),
                                     lambda g, t: (o + 3 * (g % 2), g // 2, 0, t_of(g, t), 0))
    st = pl.BlockSpec((None, rows // SUBLANES, hd, SUBLANES, LANES), lambda g, t: (g, 0, 0, 0, 0))
    return pl.pallas_call(
        _scan_kernel,
        grid=(2 * n_sg, n_t),
        in_specs=[shared(0), shared(1), per_dir(2), per_dir(3), per_dir(4),
                  pl.BlockSpec((None, rows, tb, LANES), lambda g, t: (g // 2, 0, t_of(g, t), 0)), st],
        out_specs=[pl.BlockSpec((None, tb, rows, LANES), lambda g, t: (g, t_of(g, t), 0, 0)), st],
        out_shape=[jax.ShapeDtypeStruct((2 * n_sg, t_len, rows, LANES), F32),
                   jax.ShapeDtypeStruct(s0.shape, F32)],
        scratch_shapes=[pltpu.VMEM(s0.shape[1:], F32)],
        compiler_params=_cparams("arbitrary", "arbitrary"),
        name="wkv_scan",
    )(xk, xk, xk, xk, xk, vk, s0)


def _to_chains_kernel(lane_parts, src_ref, o_ref, z_ref):
    n_seq_blk = src_ref.shape[0]
    chains = n_seq_blk * N_RWKV_HEADS
    for s in range(n_seq_blk):
        z_ref[pl.ds(s * D_RWKV, D_RWKV), :] = src_ref[s].T

    def per_channel(c, carry):
        parts = {off: z_ref[pl.ds(off + c, chains, stride=RWKV_HEAD), :] for off in set(lane_parts)}
        o_ref[c] = jnp.concatenate([parts[off] for off in lane_parts], axis=0).T
        return carry

    jax.lax.fori_loop(0, o_ref.shape[0], per_channel, 0, unroll=4)


def _to_chains(src, stream_map, n_streams, lane_parts, first_blk, n_blk):
    _, _, n_seq_blk, seq_len, d = src.shape
    tb = RELAYOUT_TBLOCK
    channels = RWKV_HEAD // len(set(lane_parts))
    return pl.pallas_call(
        functools.partial(_to_chains_kernel, lane_parts),
        grid=(n_streams, n_blk, seq_len // tb),
        in_specs=[pl.BlockSpec((None, None, n_seq_blk, tb, d),
                               lambda s, g, t: (stream_map(s), first_blk + g, 0, t, 0))],
        out_specs=pl.BlockSpec((None, None, channels, tb, LANES), lambda s, g, t: (s, g, 0, t, 0)),
        out_shape=jax.ShapeDtypeStruct((n_streams, n_blk, channels, seq_len, LANES), F32),
        scratch_shapes=[pltpu.VMEM((n_seq_blk * d, tb), F32)],
        compiler_params=_cparams("arbitrary", "arbitrary", "arbitrary"),
        name="to_chains",
    )(src)


def _from_chains_kernel(n_vsplit, yf_ref, yb_ref, *refs):
    of_ref, ob_ref, z_ref = refs[-3:]
    n_seq_blk, tb, _ = of_ref.shape
    rows = RWKV_HEAD // n_vsplit
    chains = LANES // n_vsplit
    for y_ref, dst in ((yf_ref, of_ref), (yb_ref, ob_ref)):
        def per_row(v, carry, y_ref=y_ref):
            z_ref[pl.ds(pl.multiple_of(v * LANES, LANES), LANES), :] = y_ref[pl.ds(v, tb, stride=rows), :].T
            return carry

        jax.lax.fori_loop(0, rows, per_row, 0, unroll=4)
        for s in range(n_seq_blk):
            pieces = [z_ref[pl.ds(part * chains + s * N_RWKV_HEADS + h, rows, stride=LANES), :]
                      for h in range(N_RWKV_HEADS) for part in range(n_vsplit)]
            dst[s] = jnp.concatenate(pieces, axis=0).T


def _from_chains(y, n_vsplit, n_blocks, first_blk, prev):
    n_lg, seq_len, rows, _ = y.shape
    tb = RELAYOUT_TBLOCK
    n_seq_blk = LANES // n_vsplit // N_RWKV_HEADS
    y2 = y.reshape(n_lg, seq_len * rows, LANES)
    blk = (None, n_seq_blk, tb, D_RWKV)
    shape = jax.ShapeDtypeStruct((n_blocks, n_seq_blk, seq_len, D_RWKV), F32)
    extra = [] if prev is None else [a.reshape(shape.shape) for a in prev]
    y_spec = lambda d: pl.BlockSpec((None, tb * rows, LANES), lambda g, t: (2 * g + d, t, 0))
    return pl.pallas_call(
        functools.partial(_from_chains_kernel, n_vsplit),
        grid=(n_lg // 2, seq_len // tb),
        in_specs=[y_spec(0), y_spec(1)] + [pl.BlockSpec(memory_space=pl.ANY)] * len(extra),
        out_specs=[pl.BlockSpec(blk, lambda g, t: (first_blk + g, 0, t, 0))] * 2,
        out_shape=[shape, shape],
        input_output_aliases={2 + i: i for i in range(len(extra))},
        scratch_shapes=[pltpu.VMEM((rows * LANES, tb), F32)],
        compiler_params=_cparams("arbitrary", "arbitrary"),
        name="from_chains",
    )(y2, y2, *extra)


def _wkv_group(ops, s0, seq_len, first_seq, n_seq, n_vsplit, prev_y):
    n_src, n_tok, d = ops.shape
    n_seq_blk = LANES // n_vsplit // N_RWKV_HEADS
    n_sg = n_seq // n_seq_blk
    rows = RWKV_HEAD // n_vsplit
    src = ops.reshape(n_src, n_tok // seq_len // n_seq_blk, n_seq_blk, seq_len, d)
    first_blk = first_seq // n_seq_blk
    xk = _to_chains(src, lambda s: s + jnp.where(s >= 1, 1, 0), N_SCAN_SRC - 1, (0,) * n_vsplit, first_blk, n_sg)
    vk = _to_chains(src, lambda s: 1, 1, tuple(p * rows for p in range(n_vsplit)), first_blk, n_sg)[0]
    s0c = s0.reshape(n_sg, n_seq_blk, 2, N_RWKV_HEADS, n_vsplit, rows // SUBLANES, SUBLANES, RWKV_HEAD)
    s0c = jnp.transpose(s0c, (0, 2, 5, 7, 6, 4, 1, 3)).reshape(2 * n_sg, rows // SUBLANES, RWKV_HEAD, SUBLANES, LANES)
    y, s_fin = _scan(xk, vk, s0c)
    ys = _from_chains(y, n_vsplit, n_tok // seq_len // n_seq_blk, first_blk, prev_y)
    s_fin = s_fin.reshape(n_sg, 2, rows // SUBLANES, RWKV_HEAD, SUBLANES, n_vsplit, n_seq_blk, N_RWKV_HEADS)
    s_fin = jnp.transpose(s_fin, (0, 6, 1, 7, 5, 2, 4, 3)).reshape(n_seq, 2, N_RWKV_HEADS, RWKV_HEAD, RWKV_HEAD)
    return [a.reshape(n_tok, d) for a in ys], s_fin


@functools.lru_cache(maxsize=None)
def _dft_constants(seq_len):
    n = 2 * seq_len
    idx = np.arange(seq_len)
    ang = (2.0 * np.pi / n) * ((idx[:, None] * idx[None, :]) % n)
    alt = np.where(idx % 2 == 0, 1.0, -1.0)
    f_re = np.cos(ang)
    f_im = -np.sin(ang)
    f_im[0, :] = alt
    fwd = np.concatenate([f_re, f_im], axis=0)
    c = np.full((seq_len,), 2.0)
    c[0] = 1.0
    g_re = np.cos(ang.T) * c[None, :] / n
    g_im = -2.0 * np.sin(ang.T) / n
    g_im[:, 0] = alt / n
    inv = np.concatenate([g_re, g_im], axis=1)

    def split(m):
        hi = m.astype(ml_dtypes.bfloat16)
        lo = (m - hi.astype(np.float64)).astype(ml_dtypes.bfloat16)
        return hi, lo

    return split(fwd) + split(inv)


@functools.lru_cache(maxsize=None)
def _filter_constants(seq_len):
    t = np.linspace(0.0, 1.0, seq_len)[:, None]
    f = np.linspace(1e-4, HYENA_BANDS - 1, HYENA_BANDS)
    ang = (2.0 * np.pi / seq_len) * np.arange(seq_len)[:, None] * f
    feats = np.zeros((seq_len, PAD), np.float32)
    feats[:, :HYENA_EMB] = np.concatenate([t, np.cos(ang), -np.sin(ang)], axis=-1)
    deltas = np.abs(np.linspace(math.log(HYENA_TARGET) / HYENA_SLOW_DECAY,
                                math.log(HYENA_TARGET) / HYENA_FAST_DECAY, D_HYENA))
    decay = np.exp(-t * deltas).astype(np.float32)
    return feats, np.tile(decay, (1, 2 * HYENA_ORDER))


def _filter_kernel(feats_ref, decay_ref, f1_ref, b1_ref, f2_ref, b2_ref, f3_ref, h_ref):
    hid = jnp.sin(_dot3(feats_ref[...], f1_ref[...]) + b1_ref[...])
    hid = jnp.sin(_dot3(hid, f2_ref[...]) + b2_ref[...])
    h_ref[...] = _dot3(hid, f3_ref[...]) * decay_ref[...]


def _spectrum_kernel(fh_ref, fl_ref, hf_ref, hb_ref, kf_ref):
    seq_len = hf_ref.shape[0]
    row = jax.lax.broadcasted_iota(jnp.int32, (seq_len, 1), 0)
    h_f = hf_ref[...]
    h_b = jnp.where(row == 0, 0.0, hb_ref[...])
    fh, fl = fh_ref[...], fl_ref[...]
    a = _dot3_lhs_split(fh, fl, h_f)
    b = _dot3_lhs_split(fh, fl, h_b)
    kf_ref[0:seq_len, :] = a[0:seq_len] + b[0:seq_len]
    kf_ref[seq_len:, :] = jnp.where(row == 0, a[seq_len:] + b[seq_len:], a[seq_len:] - b[seq_len:])


def _hyena_filters(seq_len, lp):
    feats, decay = _filter_constants(seq_len)
    fh, fl, _, _ = _dft_constants(seq_len)
    args = [jnp.asarray(feats), jnp.asarray(decay), lp["hy_f1"], lp["hy_fb1"], lp["hy_f2"], lp["hy_fb2"],
            lp["hy_f3"]]
    h = pl.pallas_call(
        _filter_kernel,
        out_shape=jax.ShapeDtypeStruct((seq_len, 2 * HYENA_ORDER * D_HYENA), F32),
        compiler_params=_cparams(),
        name="hyena_filter_mlp",
    )(*args)
    tc = HY_COL_TILE
    per = D_HYENA // tc
    const = pl.BlockSpec((2 * seq_len, seq_len), lambda n, c: (0, 0), pipeline_mode=pl.Buffered(1))
    return pl.pallas_call(
        _spectrum_kernel,
        grid=(HYENA_ORDER, per),
        in_specs=[const, const,
                  pl.BlockSpec((seq_len, tc), lambda n, c: (0, (2 * n) * per + c)),
                  pl.BlockSpec((seq_len, tc), lambda n, c: (0, (2 * n + 1) * per + c))],
        out_specs=pl.BlockSpec((2 * seq_len, tc), lambda n, c: (0, n * per + c)),
        out_shape=jax.ShapeDtypeStruct((2 * seq_len, HYENA_ORDER * D_HYENA), F32),
        compiler_params=_cparams("arbitrary", "arbitrary"),
        name="hyena_filter_spectrum",
    )(jnp.asarray(fh), jnp.asarray(fl), h, h)


def _hyena_kernel(pz_ref, pg1_ref, pg2_ref, cwz_ref, cwg1_ref, cwg2_ref, cbz_ref, cbg1_ref, cbg2_ref,
                  kf0_ref, kf1_ref, skip_ref, fh_ref, fl_ref, gh_ref, gl_ref, o_ref):
    seq_len = pz_ref.shape[0]
    row = jax.lax.broadcasted_iota(jnp.int32, (seq_len, 1), 0)

    def short_conv(p_ref, cw_ref, cb_ref):
        x = p_ref[...]
        prev = jnp.where(row == 0, 0.0, pltpu.roll(x, 1, axis=0))
        nxt = jnp.where(row == seq_len - 1, 0.0, pltpu.roll(x, seq_len - 1, axis=0))
        return cw_ref[0:1, :] * prev + cw_ref[1:2, :] * x + cw_ref[2:3, :] * nxt + cb_ref[...]

    z = short_conv(pz_ref, cwz_ref, cbz_ref)
    gates = (short_conv(pg1_ref, cwg1_ref, cbg1_ref), short_conv(pg2_ref, cwg2_ref, cbg2_ref))
    for n, kf_ref in enumerate((kf0_ref, kf1_ref)):
        zf = _dot3_lhs_split(fh_ref[...], fl_ref[...], z)
        z_re, z_im = zf[0:seq_len], zf[seq_len:]
        k_re, k_im = kf_ref[0:seq_len, :], kf_ref[seq_len:, :]
        p_re = jnp.where(row == 0, z_re * k_re, z_re * k_re - z_im * k_im)
        p_im = jnp.where(row == 0, z_im * k_im, z_re * k_im + z_im * k_re)
        conv = (_dot3_lhs_split(gh_ref[:, 0:seq_len], gl_ref[:, 0:seq_len], p_re)
                + _dot3_lhs_split(gh_ref[:, seq_len:], gl_ref[:, seq_len:], p_im))
        z = gates[n] * (conv + skip_ref[n:n + 1, :] * z)
    o_ref[...] = z


def _hyena(p_h, kf, lp, n_seq, seq_len, seq_offset):
    tc = HY_COL_TILE
    per = D_HYENA // tc
    fh, fl, gh, gl = [jnp.asarray(a) for a in _dft_constants(seq_len)]
    seg = lambda s: pl.BlockSpec((seq_len, tc), lambda b, c: (seq_offset + b, s * per + c))
    par = lambda rows, s: pl.BlockSpec((rows, tc), lambda b, c: (0, s * per + c))
    fconst = pl.BlockSpec((2 * seq_len, seq_len), lambda b, c: (0, 0), pipeline_mode=pl.Buffered(1))
    gconst = pl.BlockSpec((seq_len, 2 * seq_len), lambda b, c: (0, 0), pipeline_mode=pl.Buffered(1))
    cw, cb = lp["hy_conv_w"], lp["hy_conv_b"]
    return pl.pallas_call(
        _hyena_kernel,
        grid=(n_seq, per),
        in_specs=[seg(0), seg(1), seg(2), par(3, 0), par(3, 1), par(3, 2), par(1, 0), par(1, 1), par(1, 2),
                  pl.BlockSpec((2 * seq_len, tc), lambda b, c: (0, c)),
                  pl.BlockSpec((2 * seq_len, tc), lambda b, c: (0, per + c)),
                  pl.BlockSpec((HYENA_ORDER, tc), lambda b, c: (0, c)),
                  fconst, fconst, gconst, gconst],
        out_specs=pl.BlockSpec((seq_len, tc), lambda b, c: (b, c)),
        out_shape=jax.ShapeDtypeStruct((n_seq * seq_len, D_HYENA), F32),
        compiler_params=_cparams("arbitrary", "arbitrary"),
        name="hyena_conv",
    )(p_h, p_h, p_h, cw, cw, cw, cb, cb, cb, kf, kf, lp["hy_skip"], fh, fl, gh, gl)


def _grid_pos_embed(n_tokens):
    rows = n_tokens // GRID_W
    row = jnp.repeat(jnp.arange(rows, dtype=F32), GRID_W)
    col = jnp.tile(jnp.arange(GRID_W, dtype=F32), rows)
    quarter = D_MODEL // 4
    omega = 1.0 / (POS_BASE ** (jnp.arange(quarter, dtype=F32) / quarter))

    def enc(pos):
        ang = pos[:, None] * omega
        return jnp.concatenate([jnp.sin(ang), jnp.cos(ang)], axis=-1)

    return jnp.concatenate([enc(row), enc(col)], axis=-1)


def _block_diag2(m):
    z = jnp.zeros_like(m[0])
    return jnp.concatenate([jnp.concatenate([m[0], z], axis=1), jnp.concatenate([z, m[1]], axis=1)], axis=0)


def _pad_to(a, rows, cols):
    return jnp.pad(a, ((0, rows - a.shape[0]), (0, cols - a.shape[1])))


def kernel(x_prompt, x_sample, c, state_wkv, c_ctx, w_ada, b_ada, w_in, rwkv_mu, rwkv_w0, rwkv_w2, rwkv_a0, rwkv_a2, rwkv_g2, rwkv_k_k, rwkv_k_a, rwkv_r_k, rwkv_gn_w, rwkv_gn_b, hy_conv_w, hy_conv_b, hy_f1, hy_fb1, hy_f2, hy_fb2, hy_f3, hy_skip, w_pa, w_pb, w_o, ln_g, ln_b, ffn_w_in, ffn_w_out, router_w, router_b, exp_w_in, exp_w_out):
    batch, seq, d = x_prompt.shape
    dec_batch, dec_seq, _ = x_sample.shape
    depth = w_in.shape[0]
    alpha = (2 * depth) ** 0.25
    n_ctx, n_lat = batch * seq, dec_batch * dec_seq
    assert seq % ROW_TILE == 0 and dec_seq % ROW_TILE == 0 and n_ctx % dec_seq == 0
    assert n_ctx % 512 == 0 and n_lat % 512 == 0 and dec_seq % 512 == 0
    assert seq % RELAYOUT_TBLOCK == 0 and dec_seq % RELAYOUT_TBLOCK == 0
    vsplit_ctx, vsplit_lat = (1 if b % 16 == 0 else 2 for b in (batch, dec_batch))
    assert batch % (16 // vsplit_ctx) == 0 and dec_batch % (16 // vsplit_lat) == 0
    assert (n_ctx // dec_seq) % (16 // vsplit_lat) == 0
    tiles_ctx, tiles_lat, n_ctx_tiles = seq // ROW_TILE, dec_seq // ROW_TILE, n_ctx // ROW_TILE
    mod_map = _mod_row_map(n_ctx_tiles, tiles_lat, dec_batch)

    x = jnp.concatenate([x_prompt.reshape(n_ctx, d),
                         (x_sample + _grid_pos_embed(dec_seq)[None]).reshape(n_lat, d)], axis=0)
    cond_rows = -(-(dec_batch + 1) // SUBLANES) * SUBLANES
    cond = jnp.zeros((cond_rows, d), F32).at[:dec_batch].set(c).at[dec_batch].set(c_ctx)
    mods = _adaln(cond, w_ada, b_ada).reshape(depth, cond_rows, N_MOD, d)

    s0_ctx = jnp.zeros((batch, 2, N_RWKV_HEADS, RWKV_HEAD, RWKV_HEAD), F32)
    ctx_states = []
    for l in range(depth):
        lp = {
            "mu": rwkv_mu[l], "w0": rwkv_w0[l], "w2cat": _block_diag2(rwkv_w2[l]), "a0": rwkv_a0[l],
            "a2cat": _block_diag2(rwkv_a2[l]), "g2": rwkv_g2[l], "k_k": rwkv_k_k[l][None],
            "k_a": rwkv_k_a[l][None], "r_k": rwkv_r_k[l].reshape(1, D_RWKV),
            "gn_w": rwkv_gn_w[l][None], "gn_b": rwkv_gn_b[l][None],
            "w_pa": w_pa[l].astype(BF16), "w_pb": w_pb[l].astype(BF16), "w_o": w_o[l].astype(BF16),
            "ln_g1": ln_g[l, 0][None], "ln_b1": ln_b[l, 0][None],
            "hy_conv_w": hy_conv_w[l], "hy_conv_b": hy_conv_b[l][None], "hy_skip": hy_skip[l],
            "hy_f1": _pad_to(hy_f1[l], PAD, PAD), "hy_fb1": _pad_to(hy_fb1[l][None], 1, PAD),
            "hy_f2": _pad_to(hy_f2[l], PAD, PAD), "hy_fb2": _pad_to(hy_fb2[l][None], 1, PAD),
            "hy_f3": _pad_to(hy_f3[l], PAD, 2 * HYENA_ORDER * D_HYENA),
        }
        if l % 2 == 0:
            fp = {"routed": False, "w_in": ffn_w_in[l // 2].astype(BF16), "w_out": ffn_w_out[l // 2].astype(BF16),
                  "router_w": jnp.zeros((d, PAD), F32), "router_b": jnp.zeros((1, PAD), F32)}
        else:
            fp = {"routed": True, "w_in": exp_w_in[l // 2].astype(BF16), "w_out": exp_w_out[l // 2].astype(BF16),
                  "router_w": _pad_to(router_w[l // 2], d, PAD), "router_b": _pad_to(router_b[l // 2][None], 1, PAD)}
        fp["ln_g"], fp["ln_b"] = ln_g[l, 1][None], ln_b[l, 1][None]
        mod = mods[l]

        p_r, p_h, gates = _inproj(x, mod, w_in[l].astype(BF16), mod_map)
        scan_ops, bonus, g = _prep(p_r, lp, tiles_ctx, tiles_lat, n_ctx_tiles)
        ys, s_ctx = _wkv_group(scan_ops, s0_ctx, seq, 0, batch, vsplit_ctx, None)
        (y_f, y_b), _ = _wkv_group(scan_ops, state_wkv[:, l], dec_seq, n_ctx // dec_seq, dec_batch, vsplit_lat, ys)
        ctx_states.append(s_ctx)

        kf_ctx = _hyena_filters(seq, lp)
        kf_lat = kf_ctx if dec_seq == seq else _hyena_filters(dec_seq, lp)
        y_h = jnp.concatenate([_hyena(p_h, kf_ctx, lp, batch, seq, 0),
                               _hyena(p_h, kf_lat, lp, dec_batch, dec_seq, n_ctx // dec_seq)], axis=0)

        x = _tail(y_f, y_b, bonus, g, y_h, gates, x, mod, lp, mod_map, alpha)
        x = _ffn(x, mod, fp, mod_map, alpha)

    y_prompt = x[:n_ctx].reshape(batch, seq, d)
    y_sample = x[n_ctx:].reshape(dec_batch, dec_seq, d)
    return y_prompt, y_sample, jnp.stack(ctx_states, axis=1)
```

```python
import functools
import math

import jax
import jax.numpy as jnp
import ml_dtypes
import numpy as np
from jax.experimental import pallas as pl
from jax.experimental.pallas import tpu as pltpu

F32 = jnp.float32
BF16 = jnp.bfloat16

D_MODEL = 1024
GRID_W = 64
D_RWKV = 512
RWKV_HEAD = 64
N_RWKV_HEADS = D_RWKV // RWKV_HEAD
LORA_W = 64
LORA_A = 64
LORA_G = 128
D_RWKV_PROJ = 3 * D_RWKV + 2 * LORA_W + 2 * LORA_A + LORA_G
DECAY_SCALE = math.exp(-0.5)
GN_EPS = 64e-5
D_HYENA = 512
HYENA_ORDER = 2
HYENA_EMB = 33
HYENA_BANDS = (HYENA_EMB - 1) // 2
HYENA_HIDDEN = 64
HYENA_FAST_DECAY = 0.3
HYENA_SLOW_DECAY = 1.5
HYENA_TARGET = 1e-2
D_HYENA_PROJ = (HYENA_ORDER + 1) * D_HYENA
D_IN_PROJ = D_RWKV_PROJ + D_HYENA_PROJ + 2 * D_MODEL
D_FF = 2816
N_EXPERTS = 8
TOP_K = 2
D_FF_EXPERT = 1408
N_MOD = 6
LN_EPS = 1e-5
POS_BASE = 10000.0

LANES = 128
SUBLANES = 8
VMEM_LIMIT = 56 * 1024 * 1024

ROW_TILE = 256
HY_COL_TILE = 256
SCAN_VBLOCKS = 4
SCAN_TBLOCK = 64
N_SCAN_SRC = 9
RELAYOUT_TBLOCK = LANES
PAD = LANES


def _cparams(*sem):
    return pltpu.CompilerParams(dimension_semantics=sem, vmem_limit_bytes=VMEM_LIMIT)


def _dot(a, b):
    return jnp.dot(a, b, preferred_element_type=F32)


def _split2(x):
    hi = x.astype(BF16)
    lo = (x - hi.astype(F32)).astype(BF16)
    return hi, lo


def _dot3(a, b):
    ah, al = _split2(a)
    bh, bl = _split2(b)
    return _dot(ah, bh) + _dot(al, bh) + _dot(ah, bl)


def _dot3_lhs_split(ah, al, b):
    bh, bl = _split2(b)
    return _dot(ah, bh) + _dot(al, bh) + _dot(ah, bl)


def _sigmoid(x):
    return 1.0 / (1.0 + jnp.exp(-x))


def _silu(x):
    return x * _sigmoid(x)


def _head_ones():
    i = jax.lax.broadcasted_iota(jnp.int32, (D_RWKV, D_RWKV), 0) % N_RWKV_HEADS
    j = jax.lax.broadcasted_iota(jnp.int32, (D_RWKV, D_RWKV), 1) % N_RWKV_HEADS
    return jnp.where(i == j, 1.0, 0.0).astype(BF16)


def _head_sum(x, ones):
    h1 = x.astype(BF16)
    r1 = x - h1.astype(F32)
    h2 = r1.astype(BF16)
    h3 = (r1 - h2.astype(F32)).astype(BF16)
    return _dot(h1, ones) + _dot(h2, ones) + _dot(h3, ones)


def _layer_norm(z, g, b):
    mean = jnp.mean(z, axis=-1, keepdims=True)
    d = z - mean
    var = jnp.mean(d * d, axis=-1, keepdims=True)
    return d * jax.lax.rsqrt(var + LN_EPS) * g + b


def _ada_kernel(c_ref, w_ref, b_ref, o_ref):
    o_ref[...] = _dot3(_silu(c_ref[...]), w_ref[...]) + b_ref[...]


def _adaln(cond, w_ada, b_ada):
    depth, d, n = w_ada.shape
    rows = cond.shape[0]
    tn = 1536
    return pl.pallas_call(
        _ada_kernel,
        grid=(depth, n // tn),
        in_specs=[
            pl.BlockSpec((rows, d), lambda l, j: (0, 0)),
            pl.BlockSpec((None, d, tn), lambda l, j: (l, 0, j)),
            pl.BlockSpec((None, 1, tn), lambda l, j: (l, 0, j)),
        ],
        out_specs=pl.BlockSpec((None, rows, tn), lambda l, j: (l, 0, j)),
        out_shape=jax.ShapeDtypeStruct((depth, rows, n), F32),
        compiler_params=_cparams("arbitrary", "arbitrary"),
        name="adaln",
    )(cond, w_ada, b_ada.reshape(depth, 1, n))


def _mod_row_map(n_ctx_tiles, tiles_per_seq, ctx_row):
    def index_map(i, *_):
        return (jnp.where(i < n_ctx_tiles, ctx_row, (i - n_ctx_tiles) // tiles_per_seq), 0, 0)

    return index_map


def _inproj_kernel(x_ref, mod_ref, w_ref, pr_ref, ph_ref, g_ref):
    h = (x_ref[...] * (1.0 + mod_ref[1:2, :]) + mod_ref[0:1, :]).astype(BF16)
    a, b = D_RWKV_PROJ, D_RWKV_PROJ + D_HYENA_PROJ
    pr_ref[...] = _dot(h, w_ref[:, 0:a])
    ph_ref[...] = _dot(h, w_ref[:, a:b])
    g_ref[...] = _dot(h, w_ref[:, b:D_IN_PROJ])


def _inproj(x, mod, w_in_bf, mod_map):
    n = x.shape[0]
    tm = ROW_TILE
    row = lambda width: pl.BlockSpec((tm, width), lambda i: (i, 0))
    return pl.pallas_call(
        _inproj_kernel,
        grid=(n // tm,),
        in_specs=[
            row(D_MODEL),
            pl.BlockSpec((None, N_MOD, D_MODEL), mod_map),
            pl.BlockSpec((D_MODEL, D_IN_PROJ), lambda i: (0, 0)),
        ],
        out_specs=[row(D_RWKV_PROJ), row(D_HYENA_PROJ), row(2 * D_MODEL)],
        out_shape=[
            jax.ShapeDtypeStruct((n, D_RWKV_PROJ), F32),
            jax.ShapeDtypeStruct((n, D_HYENA_PROJ), F32),
            jax.ShapeDtypeStruct((n, 2 * D_MODEL), F32),
        ],
        compiler_params=_cparams("arbitrary"),
        name="inproj",
    )(x, mod, w_in_bf)


def _prep_kernel(tiles_ctx, tiles_lat, n_ctx_tiles,
                 p_ref, prev_ref, next_ref, mu_ref, w0_ref, w2_ref, a0_ref, a2_ref, g2_ref,
                 kk_w_ref, ka_ref, rk_ref,
                 ops_ref, bonus_ref, g_ref):
    i = pl.program_id(0)
    tm = p_ref.shape[0]
    j = jnp.where(i < n_ctx_tiles, i % tiles_ctx, (i - n_ctx_tiles) % tiles_lat)
    per_seq = jnp.where(i < n_ctx_tiles, tiles_ctx, tiles_lat)
    x = p_ref[...]
    rowid = jax.lax.broadcasted_iota(jnp.int32, (tm, 1), 0)
    prev_row = jnp.where(j == 0, 0.0, prev_ref[SUBLANES - 1:SUBLANES, :])
    next_row = jnp.where(j == per_seq - 1, 0.0, next_ref[0:1, :])
    prev = jnp.where(rowid == 0, prev_row, pltpu.roll(x, 1, axis=0))
    nxt = jnp.where(rowid == tm - 1, next_row, pltpu.roll(x, tm - 1, axis=0))
    p = x + mu_ref[0:1, :] * (prev - x) + mu_ref[1:2, :] * (nxt - x)

    d = D_RWKV
    r, k, v = p[:, 0:d], p[:, d:2 * d], p[:, 2 * d:3 * d]
    low_w = p[:, 3 * d:3 * d + LANES]
    low_a = p[:, 3 * d + LANES:3 * d + 2 * LANES]
    low_g = p[:, 3 * d + 2 * LANES:3 * d + 3 * LANES]
    ones = _head_ones()

    kk = k * kk_w_ref[...]
    kk = kk * jax.lax.rsqrt(jnp.maximum(_head_sum(kk * kk, ones), 1e-24))
    lw = _dot3(jnp.tanh(low_w), w2_ref[...])
    la = _dot3(low_a, a2_ref[...])
    ksum = jnp.zeros_like(k)
    for dirn in range(2):
        log_w = -DECAY_SCALE * _sigmoid(w0_ref[dirn:dirn + 1, :] + lw[:, dirn * d:(dirn + 1) * d])
        a = _sigmoid(a0_ref[dirn:dirn + 1, :] + la[:, dirn * d:(dirn + 1) * d])
        k_d = k * (1.0 + (a - 1.0) * ka_ref[...])
        ops_ref[3 + 3 * dirn] = jnp.exp(log_w)
        ops_ref[4 + 3 * dirn] = k_d
        ops_ref[5 + 3 * dirn] = kk * a
        ksum = ksum + k_d
    ops_ref[0] = r
    ops_ref[1] = v
    ops_ref[2] = kk
    bonus_ref[...] = _head_sum(r * (0.5 * ksum) * rk_ref[...], ones) * v
    g_ref[...] = _dot3(_sigmoid(low_g), g2_ref[...])


def _prep(p_r, lp, tiles_ctx, tiles_lat, n_ctx_tiles):
    n = p_r.shape[0]
    tm = ROW_TILE
    halo = tm // SUBLANES
    n_halo = n // SUBLANES
    full = lambda a: pl.BlockSpec(a.shape, lambda i: (0,) * a.ndim)
    params = [lp[k] for k in ("mu", "w0", "w2cat", "a0", "a2cat", "g2", "k_k", "k_a", "r_k")]
    out = jax.ShapeDtypeStruct((n, D_RWKV), F32)
    return pl.pallas_call(
        functools.partial(_prep_kernel, tiles_ctx, tiles_lat, n_ctx_tiles),
        grid=(n // tm,),
        in_specs=[
            pl.BlockSpec((tm, D_RWKV_PROJ), lambda i: (i, 0)),
            pl.BlockSpec((SUBLANES, D_RWKV_PROJ), lambda i: (jnp.maximum(i * halo - 1, 0), 0)),
            pl.BlockSpec((SUBLANES, D_RWKV_PROJ), lambda i: (jnp.minimum((i + 1) * halo, n_halo - 1), 0)),
        ] + [full(a) for a in params],
        out_specs=[pl.BlockSpec((N_SCAN_SRC, tm, D_RWKV), lambda i: (0, i, 0))]
        + [pl.BlockSpec((tm, D_RWKV), lambda i: (i, 0))] * 2,
        out_shape=[jax.ShapeDtypeStruct((N_SCAN_SRC, n, D_RWKV), F32), out, out],
        compiler_params=_cparams("arbitrary"),
        name="rwkv_prep",
    )(p_r, p_r, p_r, *params)


def _tail_kernel(alpha, yf_ref, yb_ref, bonus_ref, g_ref, yh_ref, gate_ref, x_ref, mod_ref,
                 gnw_ref, gnb_ref, wpa_ref, wpb_ref, wo_ref, lng_ref, lnb_ref, o_ref):
    ones = _head_ones()
    y = yf_ref[...] + yb_ref[...]
    mean = _head_sum(y, ones) * (1.0 / RWKV_HEAD)
    d = y - mean
    var = _head_sum(d * d, ones) * (1.0 / RWKV_HEAD)
    y_n = d * jax.lax.rsqrt(var + GN_EPS) * gnw_ref[...] + gnb_ref[...]
    y_a = ((y_n + bonus_ref[...]) * g_ref[...]).astype(BF16)
    merged = (_sigmoid(gate_ref[:, 0:D_MODEL]) * _dot(y_a, wpa_ref[...])
              + _sigmoid(gate_ref[:, D_MODEL:2 * D_MODEL]) * _dot(yh_ref[...].astype(BF16), wpb_ref[...]))
    m = _dot(merged.astype(BF16), wo_ref[...])
    z = alpha * x_ref[...] + mod_ref[2:3, :] * m
    o_ref[...] = _layer_norm(z, lng_ref[...], lnb_ref[...])


def _tail(y_f, y_b, bonus, g, y_h, gates, x, mod, lp, mod_map, alpha):
    n = x.shape[0]
    tm = ROW_TILE
    row = lambda width: pl.BlockSpec((tm, width), lambda i: (i, 0))
    full = lambda a: pl.BlockSpec(a.shape, lambda i: (0,) * a.ndim)
    params = [lp[k] for k in ("gn_w", "gn_b", "w_pa", "w_pb", "w_o", "ln_g1", "ln_b1")]
    return pl.pallas_call(
        functools.partial(_tail_kernel, alpha),
        grid=(n // tm,),
        in_specs=[row(D_RWKV)] * 5 + [row(2 * D_MODEL), row(D_MODEL),
                                      pl.BlockSpec((None, N_MOD, D_MODEL), mod_map)]
        + [full(a) for a in params],
        out_specs=row(D_MODEL),
        out_shape=jax.ShapeDtypeStruct((n, D_MODEL), F32),
        compiler_params=_cparams("arbitrary"),
        name="mixer_tail",
    )(y_f, y_b, bonus, g, y_h, gates, x, mod, *params)


def _ffn_kernel(routed, alpha, x_ref, mod_ref, wg_ref, wu_ref, wd_ref, rw_ref, rb_ref,
                lng_ref, lnb_ref, o_ref, h_ref, acc_ref, comb_ref):
    e = pl.program_id(1)
    lane = jax.lax.broadcasted_iota(jnp.int32, comb_ref.shape, 1)

    @pl.when(e == 0)
    def _():
        h = x_ref[...] * (1.0 + mod_ref[4:5, :]) + mod_ref[3:4, :]
        h_ref[...] = h.astype(BF16)
        acc_ref[...] = jnp.zeros_like(acc_ref)
        if routed:
            logits = _dot3(h, rw_ref[...]) + rb_ref[...]
            logits = jnp.where(lane < N_EXPERTS, logits, -jnp.inf)
            ex = jnp.exp(logits - jnp.max(logits, axis=-1, keepdims=True))
            probs = ex / jnp.sum(ex, axis=-1, keepdims=True)
            p1 = jnp.max(probs, axis=-1, keepdims=True)
            i1 = jnp.min(jnp.where(probs == p1, lane, PAD), axis=-1, keepdims=True)
            rest = jnp.where(lane == i1, -1.0, probs)
            p2 = jnp.max(rest, axis=-1, keepdims=True)
            i2 = jnp.min(jnp.where(rest == p2, lane, PAD), axis=-1, keepdims=True)
            total = p1 + p2
            comb_ref[...] = jnp.where(lane == i1, p1 / total, 0.0) + jnp.where(lane == i2, p2 / total, 0.0)

    h = h_ref[...]
    act = _silu(_dot(h, wg_ref[...])) * _dot(h, wu_ref[...])
    if routed:
        act = act * jnp.sum(jnp.where(lane == e, comb_ref[...], 0.0), axis=-1, keepdims=True)
    acc_ref[...] += _dot(act.astype(BF16), wd_ref[...])

    @pl.when(e == pl.num_programs(1) - 1)
    def _():
        z = alpha * x_ref[...] + mod_ref[5:6, :] * acc_ref[...]
        o_ref[...] = _layer_norm(z, lng_ref[...], lnb_ref[...])


def _ffn(x, mod, fp, mod_map, alpha):
    n = x.shape[0]
    tm = 512
    scale = tm // ROW_TILE
    routed = fp["routed"]
    tf = D_FF_EXPERT
    if routed:
        groups = N_EXPERTS
        wg_spec = pl.BlockSpec((None, D_MODEL, tf), lambda i, e: (e, 0, 0))
        wu_spec = pl.BlockSpec((None, D_MODEL, tf), lambda i, e: (e, 0, 1))
        wd_spec = pl.BlockSpec((None, tf, D_MODEL), lambda i, e: (e, 0, 0))
    else:
        groups = D_FF // tf
        wg_spec = pl.BlockSpec((D_MODEL, tf), lambda i, e: (0, e))
        wu_spec = pl.BlockSpec((D_MODEL, tf), lambda i, e: (0, e + groups))
        wd_spec = pl.BlockSpec((tf, D_MODEL), lambda i, e: (e, 0))
    full = lambda a: pl.BlockSpec(a.shape, lambda i, e: (0,) * a.ndim)
    row = pl.BlockSpec((tm, D_MODEL), lambda i, e: (i, 0))
    mod_spec = pl.BlockSpec((None, N_MOD, D_MODEL), lambda i, e: mod_map(i * scale))
    return pl.pallas_call(
        functools.partial(_ffn_kernel, routed, alpha),
        grid=(n // tm, groups),
        in_specs=[row, mod_spec, wg_spec, wu_spec, wd_spec, full(fp["router_w"]), full(fp["router_b"]),
                  full(fp["ln_g"]), full(fp["ln_b"])],
        out_specs=row,
        out_shape=jax.ShapeDtypeStruct((n, D_MODEL), F32),
        scratch_shapes=[pltpu.VMEM((tm, D_MODEL), BF16), pltpu.VMEM((tm, D_MODEL), F32),
                        pltpu.VMEM((tm, PAD), F32)],
        compiler_params=_cparams("arbitrary", "arbitrary"),
        name="moe_ffn" if routed else "dense_ffn",
    )(x, mod, fp["w_in"], fp["w_in"], fp["w_out"], fp["router_w"], fp["router_b"], fp["ln_g"], fp["ln_b"])


def _scan_kernel(r_ref, kk_ref, w_ref, k_ref, b_ref, v_ref, s0_ref, y_ref, sfin_ref, s_ref):
    tb = r_ref.shape[1]
    n_vblocks = s_ref.shape[0]
    tile = (SUBLANES, LANES)
    backward = pl.program_id(0) % 2 == 1

    @pl.when(pl.program_id(1) == 0)
    def _():
        s_ref[...] = s0_ref[...]

    def time_of(i):
        return jnp.where(backward, tb - 1 - i, i)

    def row(ref, k, t):
        return jnp.broadcast_to(ref[k, pl.ds(t, 1), :], tile)

    for part in range(n_vblocks // SCAN_VBLOCKS):
        vbs = [part * SCAN_VBLOCKS + j for j in range(SCAN_VBLOCKS)]

        def step(i, sa, vbs=vbs):
            t = time_of(i)
            t_next = time_of(jnp.minimum(i + 1, tb - 1))
            v8 = [jnp.concatenate([v_ref[vb * SUBLANES + j, pl.ds(t, 1), :] for j in range(SUBLANES)], axis=0)
                  for vb in vbs]
            y = [jnp.zeros(tile, F32) for _ in vbs]
            sa_next = [jnp.zeros(tile, F32) for _ in vbs]
            for k in range(RWKV_HEAD):
                w, b, kd, r = row(w_ref, k, t), row(b_ref, k, t), row(k_ref, k, t), row(r_ref, k, t)
                kap = row(kk_ref, k, t_next)
                for j, vb in enumerate(vbs):
                    s = s_ref[vb, k] * w - sa[j] * b + v8[j] * kd
                    s_ref[vb, k] = s
                    y[j] = y[j] + s * r
                    sa_next[j] = sa_next[j] + s * kap
            for j, vb in enumerate(vbs):
                y_ref[t, pl.ds(vb * SUBLANES, SUBLANES), :] = y[j]
            return tuple(sa_next)

        t0 = time_of(0)
        sa0 = [jnp.zeros(tile, F32) for _ in vbs]
        for k in range(RWKV_HEAD):
            kap = row(kk_ref, k, t0)
            for j, vb in enumerate(vbs):
                sa0[j] = sa0[j] + s_ref[vb, k] * kap
        jax.lax.fori_loop(0, tb, step, tuple(sa0))

    @pl.when(pl.program_id(1) == pl.num_programs(1) - 1)
    def _():
        sfin_ref[...] = s_ref[...]


def _scan(xk, vk, s0):
    _, n_sg, hd, t_len, _ = xk.shape
    rows = vk.shape[1]
    tb = SCAN_TBLOCK
    n_t = t_len // tb
    t_of = lambda g, t: jnp.where(g % 2 == 1, n_t - 1 - t, t)
    shared = lambda o: pl.BlockSpec((None, None, hd, tb, LANES), lambda g, t: (o, g // 2, 0, t_of(g, t), 0))
    per_dir = lambda o: pl.BlockSpec((None, None, hd, tb, LANES),
                                     lambda g, t: (o + 3 * (g % 2), g // 2, 0, t_of(g, t), 0))
    st = pl.BlockSpec((None, rows // SUBLANES, hd, SUBLANES, LANES), lambda g, t: (g, 0, 0, 0, 0))
    return pl.pallas_call(
        _scan_kernel,
        grid=(2 * n_sg, n_t),
        in_specs=[shared(0), shared(1), per_dir(2), per_dir(3), per_dir(4),
                  pl.BlockSpec((None, rows, tb, LANES), lambda g, t: (g // 2, 0, t_of(g, t), 0)), st],
        out_specs=[pl.BlockSpec((None, tb, rows, LANES), lambda g, t: (g, t_of(g, t), 0, 0)), st],
        out_shape=[jax.ShapeDtypeStruct((2 * n_sg, t_len, rows, LANES), F32),
                   jax.ShapeDtypeStruct(s0.shape, F32)],
        scratch_shapes=[pltpu.VMEM(s0.shape[1:], F32)],
        compiler_params=_cparams("arbitrary", "arbitrary"),
        name="wkv_scan",
    )(xk, xk, xk, xk, xk, vk, s0)


def _to_chains_kernel(lane_parts, src_ref, o_ref, z_ref):
    n_seq_blk = src_ref.shape[0]
    chains = n_seq_blk * N_RWKV_HEADS
    for s in range(n_seq_blk):
        z_ref[pl.ds(s * D_RWKV, D_RWKV), :] = src_ref[s].T

    def heads_of(c):
        rows = [z_ref[pl.ds(pl.multiple_of(s * D_RWKV + c * N_RWKV_HEADS, N_RWKV_HEADS), N_RWKV_HEADS), :]
                for s in range(n_seq_blk)]
        return jnp.concatenate(rows, axis=0)

    def per_channel(c, carry):
        parts = {off: heads_of(off + c) for off in set(lane_parts)}
        o_ref[c] = jnp.concatenate([parts[off] for off in lane_parts], axis=0).T
        return carry

    jax.lax.fori_loop(0, o_ref.shape[0], per_channel, 0, unroll=4)


def _to_chains(src, stream_map, n_streams, lane_parts, first_blk, n_blk):
    _, _, n_seq_blk, seq_len, d = src.shape
    tb = RELAYOUT_TBLOCK
    channels = RWKV_HEAD // len(set(lane_parts))
    return pl.pallas_call(
        functools.partial(_to_chains_kernel, lane_parts),
        grid=(n_streams, n_blk, seq_len // tb),
        in_specs=[pl.BlockSpec((None, None, n_seq_blk, tb, d),
                               lambda s, g, t: (stream_map(s), first_blk + g, 0, t, 0))],
        out_specs=pl.BlockSpec((None, None, channels, tb, LANES), lambda s, g, t: (s, g, 0, t, 0)),
        out_shape=jax.ShapeDtypeStruct((n_streams, n_blk, channels, seq_len, LANES), F32),
        scratch_shapes=[pltpu.VMEM((n_seq_blk * d, tb), F32)],
        compiler_params=_cparams("arbitrary", "arbitrary", "arbitrary"),
        name="to_chains",
    )(src)


def _from_chains_kernel(n_vsplit, yf_ref, yb_ref, *refs):
    of_ref, ob_ref, z_ref = refs[-3:]
    n_seq_blk, tb, _ = of_ref.shape
    rows = RWKV_HEAD // n_vsplit
    chains = LANES // n_vsplit
    for y_ref, dst in ((yf_ref, of_ref), (yb_ref, ob_ref)):
        def per_row(v, carry, y_ref=y_ref):
            z_ref[pl.ds(pl.multiple_of(v * LANES, LANES), LANES), :] = y_ref[pl.ds(v, tb, stride=rows), :].T
            return carry

        jax.lax.fori_loop(0, rows, per_row, 0, unroll=4)
        for s in range(n_seq_blk):
            pieces = [z_ref[pl.ds(v * LANES + part * chains + s * N_RWKV_HEADS, N_RWKV_HEADS), :]
                      for part in range(n_vsplit) for v in range(rows)]
            dst[s] = jnp.concatenate(pieces, axis=0).T


def _from_chains(y, n_vsplit, n_blocks, first_blk, prev):
    n_lg, seq_len, rows, _ = y.shape
    tb = RELAYOUT_TBLOCK
    n_seq_blk = LANES // n_vsplit // N_RWKV_HEADS
    y2 = y.reshape(n_lg, seq_len * rows, LANES)
    blk = (None, n_seq_blk, tb, D_RWKV)
    shape = jax.ShapeDtypeStruct((n_blocks, n_seq_blk, seq_len, D_RWKV), F32)
    extra = [] if prev is None else [a.reshape(shape.shape) for a in prev]
    y_spec = lambda d: pl.BlockSpec((None, tb * rows, LANES), lambda g, t: (2 * g + d, t, 0))
    return pl.pallas_call(
        functools.partial(_from_chains_kernel, n_vsplit),
        grid=(n_lg // 2, seq_len // tb),
        in_specs=[y_spec(0), y_spec(1)] + [pl.BlockSpec(memory_space=pl.ANY)] * len(extra),
        out_specs=[pl.BlockSpec(blk, lambda g, t: (first_blk + g, 0, t, 0))] * 2,
        out_shape=[shape, shape],
        input_output_aliases={2 + i: i for i in range(len(extra))},
        scratch_shapes=[pltpu.VMEM((rows * LANES, tb), F32)],
        compiler_params=_cparams("arbitrary", "arbitrary"),
        name="from_chains",
    )(y2, y2, *extra)


def _wkv_group(ops, s0, seq_len, first_seq, n_seq, n_vsplit, prev_y):
    n_src, n_tok, d = ops.shape
    n_seq_blk = LANES // n_vsplit // N_RWKV_HEADS
    n_sg = n_seq // n_seq_blk
    rows = RWKV_HEAD // n_vsplit
    src = ops.reshape(n_src, n_tok // seq_len // n_seq_blk, n_seq_blk, seq_len, d)
    first_blk = first_seq // n_seq_blk
    xk = _to_chains(src, lambda s: s + jnp.where(s >= 1, 1, 0), N_SCAN_SRC - 1, (0,) * n_vsplit, first_blk, n_sg)
    vk = _to_chains(src, lambda s: 1, 1, tuple(p * rows for p in range(n_vsplit)), first_blk, n_sg)[0]
    s0c = s0.reshape(n_sg, n_seq_blk, 2, N_RWKV_HEADS, n_vsplit, rows // SUBLANES, SUBLANES, RWKV_HEAD)
    s0c = jnp.transpose(s0c, (0, 2, 5, 7, 6, 4, 1, 3)).reshape(2 * n_sg, rows // SUBLANES, RWKV_HEAD, SUBLANES, LANES)
    y, s_fin = _scan(xk, vk, s0c)
    ys = _from_chains(y, n_vsplit, n_tok // seq_len // n_seq_blk, first_blk, prev_y)
    s_fin = s_fin.reshape(n_sg, 2, rows // SUBLANES, RWKV_HEAD, SUBLANES, n_vsplit, n_seq_blk, N_RWKV_HEADS)
    s_fin = jnp.transpose(s_fin, (0, 6, 1, 7, 5, 2, 4, 3)).reshape(n_seq, 2, N_RWKV_HEADS, RWKV_HEAD, RWKV_HEAD)
    return [a.reshape(n_tok, d) for a in ys], s_fin


@functools.lru_cache(maxsize=None)
def _dft_constants(seq_len):
    n = 2 * seq_len
    idx = np.arange(seq_len)
    ang = (2.0 * np.pi / n) * ((idx[:, None] * idx[None, :]) % n)
    alt = np.where(idx % 2 == 0, 1.0, -1.0)
    f_re = np.cos(ang)
    f_im = -np.sin(ang)
    f_im[0, :] = alt
    fwd = np.concatenate([f_re, f_im], axis=0)
    c = np.full((seq_len,), 2.0)
    c[0] = 1.0
    g_re = np.cos(ang.T) * c[None, :] / n
    g_im = -2.0 * np.sin(ang.T) / n
    g_im[:, 0] = alt / n
    inv = np.concatenate([g_re, g_im], axis=1)

    def split(m):
        hi = m.astype(ml_dtypes.bfloat16)
        lo = (m - hi.astype(np.float64)).astype(ml_dtypes.bfloat16)
        return hi, lo

    return split(fwd) + split(inv)


@functools.lru_cache(maxsize=None)
def _filter_constants(seq_len):
    t = np.linspace(0.0, 1.0, seq_len)[:, None]
    f = np.linspace(1e-4, HYENA_BANDS - 1, HYENA_BANDS)
    ang = (2.0 * np.pi / seq_len) * np.arange(seq_len)[:, None] * f
    feats = np.zeros((seq_len, PAD), np.float32)
    feats[:, :HYENA_EMB] = np.concatenate([t, np.cos(ang), -np.sin(ang)], axis=-1)
    deltas = np.abs(np.linspace(math.log(HYENA_TARGET) / HYENA_SLOW_DECAY,
                                math.log(HYENA_TARGET) / HYENA_FAST_DECAY, D_HYENA))
    decay = np.exp(-t * deltas).astype(np.float32)
    return feats, np.tile(decay, (1, 2 * HYENA_ORDER))


def _filter_kernel(feats_ref, decay_ref, f1_ref, b1_ref, f2_ref, b2_ref, f3_ref, h_ref):
    hid = jnp.sin(_dot3(feats_ref[...], f1_ref[...]) + b1_ref[...])
    hid = jnp.sin(_dot3(hid, f2_ref[...]) + b2_ref[...])
    h_ref[...] = _dot3(hid, f3_ref[...]) * decay_ref[...]


def _spectrum_kernel(fh_ref, fl_ref, hf_ref, hb_ref, kf_ref):
    seq_len = hf_ref.shape[0]
    row = jax.lax.broadcasted_iota(jnp.int32, (seq_len, 1), 0)
    h_f = hf_ref[...]
    h_b = jnp.where(row == 0, 0.0, hb_ref[...])
    fh, fl = fh_ref[...], fl_ref[...]
    a = _dot3_lhs_split(fh, fl, h_f)
    b = _dot3_lhs_split(fh, fl, h_b)
    kf_ref[0:seq_len, :] = a[0:seq_len] + b[0:seq_len]
    kf_ref[seq_len:, :] = jnp.where(row == 0, a[seq_len:] + b[seq_len:], a[seq_len:] - b[seq_len:])


def _hyena_filters(seq_len, lp):
    feats, decay = _filter_constants(seq_len)
    fh, fl, _, _ = _dft_constants(seq_len)
    args = [jnp.asarray(feats), jnp.asarray(decay), lp["hy_f1"], lp["hy_fb1"], lp["hy_f2"], lp["hy_fb2"],
            lp["hy_f3"]]
    h = pl.pallas_call(
        _filter_kernel,
        out_shape=jax.ShapeDtypeStruct((seq_len, 2 * HYENA_ORDER * D_HYENA), F32),
        compiler_params=_cparams(),
        name="hyena_filter_mlp",
    )(*args)
    tc = HY_COL_TILE
    per = D_HYENA // tc
    const = pl.BlockSpec((2 * seq_len, seq_len), lambda n, c: (0, 0), pipeline_mode=pl.Buffered(1))
    return pl.pallas_call(
        _spectrum_kernel,
        grid=(HYENA_ORDER, per),
        in_specs=[const, const,
                  pl.BlockSpec((seq_len, tc), lambda n, c: (0, (2 * n) * per + c)),
                  pl.BlockSpec((seq_len, tc), lambda n, c: (0, (2 * n + 1) * per + c))],
        out_specs=pl.BlockSpec((2 * seq_len, tc), lambda n, c: (0, n * per + c)),
        out_shape=jax.ShapeDtypeStruct((2 * seq_len, HYENA_ORDER * D_HYENA), F32),
        compiler_params=_cparams("arbitrary", "arbitrary"),
        name="hyena_filter_spectrum",
    )(jnp.asarray(fh), jnp.asarray(fl), h, h)


def _hyena_kernel(pz_ref, pg1_ref, pg2_ref, cwz_ref, cwg1_ref, cwg2_ref, cbz_ref, cbg1_ref, cbg2_ref,
                  kf0_ref, kf1_ref, skip_ref, f_ref, g_ref, o_ref):
    seq_len = pz_ref.shape[0]
    row = jax.lax.broadcasted_iota(jnp.int32, (seq_len, 1), 0)

    def short_conv(p_ref, cw_ref, cb_ref):
        x = p_ref[...]
        prev = jnp.where(row == 0, 0.0, pltpu.roll(x, 1, axis=0))
        nxt = jnp.where(row == seq_len - 1, 0.0, pltpu.roll(x, seq_len - 1, axis=0))
        return cw_ref[0:1, :] * prev + cw_ref[1:2, :] * x + cw_ref[2:3, :] * nxt + cb_ref[...]

    z = short_conv(pz_ref, cwz_ref, cbz_ref)
    gates = (short_conv(pg1_ref, cwg1_ref, cbg1_ref), short_conv(pg2_ref, cwg2_ref, cbg2_ref))
    for n, kf_ref in enumerate((kf0_ref, kf1_ref)):
        zf = _dot(f_ref[...], z.astype(BF16))
        z_re, z_im = zf[0:seq_len], zf[seq_len:]
        k_re, k_im = kf_ref[0:seq_len, :], kf_ref[seq_len:, :]
        p_re = jnp.where(row == 0, z_re * k_re, z_re * k_re - z_im * k_im)
        p_im = jnp.where(row == 0, z_im * k_im, z_re * k_im + z_im * k_re)
        conv = (_dot(g_ref[:, 0:seq_len], p_re.astype(BF16)) + _dot(g_ref[:, seq_len:], p_im.astype(BF16)))
        z = gates[n] * (conv + skip_ref[n:n + 1, :] * z)
    o_ref[...] = z


def _hyena(p_h, kf, lp, n_seq, seq_len, seq_offset):
    tc = HY_COL_TILE
    per = D_HYENA // tc
    fh, _, gh, _ = [jnp.asarray(a) for a in _dft_constants(seq_len)]
    seg = lambda s: pl.BlockSpec((seq_len, tc), lambda b, c: (seq_offset + b, s * per + c))
    par = lambda rows, s: pl.BlockSpec((rows, tc), lambda b, c: (0, s * per + c))
    fconst = pl.BlockSpec((2 * seq_len, seq_len), lambda b, c: (0, 0), pipeline_mode=pl.Buffered(1))
    gconst = pl.BlockSpec((seq_len, 2 * seq_len), lambda b, c: (0, 0), pipeline_mode=pl.Buffered(1))
    cw, cb = lp["hy_conv_w"], lp["hy_conv_b"]
    return pl.pallas_call(
        _hyena_kernel,
        grid=(n_seq, per),
        in_specs=[seg(0), seg(1), seg(2), par(3, 0), par(3, 1), par(3, 2), par(1, 0), par(1, 1), par(1, 2),
                  pl.BlockSpec((2 * seq_len, tc), lambda b, c: (0, c)),
                  pl.BlockSpec((2 * seq_len, tc), lambda b, c: (0, per + c)),
                  pl.BlockSpec((HYENA_ORDER, tc), lambda b, c: (0, c)),
                  fconst, gconst],
        out_specs=pl.BlockSpec((seq_len, tc), lambda b, c: (b, c)),
        out_shape=jax.ShapeDtypeStruct((n_seq * seq_len, D_HYENA), F32),
        compiler_params=_cparams("arbitrary", "arbitrary"),
        name="hyena_conv",
    )(p_h, p_h, p_h, cw, cw, cw, cb, cb, cb, kf, kf, lp["hy_skip"], fh, gh)


def _grid_pos_embed(n_tokens):
    rows = n_tokens // GRID_W
    row = jnp.repeat(jnp.arange(rows, dtype=F32), GRID_W)
    col = jnp.tile(jnp.arange(GRID_W, dtype=F32), rows)
    quarter = D_MODEL // 4
    omega = 1.0 / (POS_BASE ** (jnp.arange(quarter, dtype=F32) / quarter))

    def enc(pos):
        ang = pos[:, None] * omega
        return jnp.concatenate([jnp.sin(ang), jnp.cos(ang)], axis=-1)

    return jnp.concatenate([enc(row), enc(col)], axis=-1)


def _block_diag2(m):
    z = jnp.zeros_like(m[0])
    return jnp.concatenate([jnp.concatenate([m[0], z], axis=1), jnp.concatenate([z, m[1]], axis=1)], axis=0)


def _interleave_heads(a, axis=-1):
    a = jnp.moveaxis(a, axis, -1)
    lead = a.shape[:-1]
    a = a.reshape(lead + (-1, N_RWKV_HEADS, RWKV_HEAD))
    a = jnp.swapaxes(a, -1, -2).reshape(lead + (-1,))
    return jnp.moveaxis(a, -1, axis)


def _pad_to(a, rows, cols):
    return jnp.pad(a, ((0, rows - a.shape[0]), (0, cols - a.shape[1])))


def kernel(x_prompt, x_sample, c, state_wkv, c_ctx, w_ada, b_ada, w_in, rwkv_mu, rwkv_w0, rwkv_w2, rwkv_a0, rwkv_a2, rwkv_g2, rwkv_k_k, rwkv_k_a, rwkv_r_k, rwkv_gn_w, rwkv_gn_b, hy_conv_w, hy_conv_b, hy_f1, hy_fb1, hy_f2, hy_fb2, hy_f3, hy_skip, w_pa, w_pb, w_o, ln_g, ln_b, ffn_w_in, ffn_w_out, router_w, router_b, exp_w_in, exp_w_out):
    batch, seq, d = x_prompt.shape
    dec_batch, dec_seq, _ = x_sample.shape
    depth = w_in.shape[0]
    alpha = (2 * depth) ** 0.25
    n_ctx, n_lat = batch * seq, dec_batch * dec_seq
    assert seq % ROW_TILE == 0 and dec_seq % ROW_TILE == 0 and n_ctx % dec_seq == 0
    assert n_ctx % 512 == 0 and n_lat % 512 == 0 and dec_seq % 512 == 0
    assert seq % RELAYOUT_TBLOCK == 0 and dec_seq % RELAYOUT_TBLOCK == 0
    vsplit_ctx, vsplit_lat = (1 if b % 16 == 0 else 2 for b in (batch, dec_batch))
    assert batch % (16 // vsplit_ctx) == 0 and dec_batch % (16 // vsplit_lat) == 0
    assert (n_ctx // dec_seq) % (16 // vsplit_lat) == 0
    tiles_ctx, tiles_lat, n_ctx_tiles = seq // ROW_TILE, dec_seq // ROW_TILE, n_ctx // ROW_TILE
    mod_map = _mod_row_map(n_ctx_tiles, tiles_lat, dec_batch)

    x = jnp.concatenate([x_prompt.reshape(n_ctx, d),
                         (x_sample + _grid_pos_embed(dec_seq)[None]).reshape(n_lat, d)], axis=0)
    cond_rows = -(-(dec_batch + 1) // SUBLANES) * SUBLANES
    cond = jnp.zeros((cond_rows, d), F32).at[:dec_batch].set(c).at[dec_batch].set(c_ctx)
    mods = _adaln(cond, w_ada, b_ada).reshape(depth, cond_rows, N_MOD, d)

    il = _interleave_heads
    s0_ctx = jnp.zeros((batch, 2, N_RWKV_HEADS, RWKV_HEAD, RWKV_HEAD), F32)
    ctx_states = []
    for l in range(depth):
        lp = {
            "mu": jnp.concatenate([il(rwkv_mu[l][:, :3 * D_RWKV]), rwkv_mu[l][:, 3 * D_RWKV:]], axis=1),
            "w0": il(rwkv_w0[l]), "w2cat": _block_diag2(il(rwkv_w2[l])), "a0": il(rwkv_a0[l]),
            "a2cat": _block_diag2(il(rwkv_a2[l])), "g2": il(rwkv_g2[l]), "k_k": il(rwkv_k_k[l])[None],
            "k_a": il(rwkv_k_a[l])[None], "r_k": il(rwkv_r_k[l].reshape(1, D_RWKV)),
            "gn_w": il(rwkv_gn_w[l])[None], "gn_b": il(rwkv_gn_b[l])[None],
            "w_pa": il(w_pa[l], axis=0).astype(BF16), "w_pb": w_pb[l].astype(BF16), "w_o": w_o[l].astype(BF16),
            "ln_g1": ln_g[l, 0][None], "ln_b1": ln_b[l, 0][None],
            "hy_conv_w": hy_conv_w[l], "hy_conv_b": hy_conv_b[l][None], "hy_skip": hy_skip[l],
            "hy_f1": _pad_to(hy_f1[l], PAD, PAD), "hy_fb1": _pad_to(hy_fb1[l][None], 1, PAD),
            "hy_f2": _pad_to(hy_f2[l], PAD, PAD), "hy_fb2": _pad_to(hy_fb2[l][None], 1, PAD),
            "hy_f3": _pad_to(hy_f3[l], PAD, 2 * HYENA_ORDER * D_HYENA),
        }
        if l % 2 == 0:
            fp = {"routed": False, "w_in": ffn_w_in[l // 2].astype(BF16), "w_out": ffn_w_out[l // 2].astype(BF16),
                  "router_w": jnp.zeros((d, PAD), F32), "router_b": jnp.zeros((1, PAD), F32)}
        else:
            fp = {"routed": True, "w_in": exp_w_in[l // 2].astype(BF16), "w_out": exp_w_out[l // 2].astype(BF16),
                  "router_w": _pad_to(router_w[l // 2], d, PAD), "router_b": _pad_to(router_b[l // 2][None], 1, PAD)}
        fp["ln_g"], fp["ln_b"] = ln_g[l, 1][None], ln_b[l, 1][None]
        mod = mods[l]

        w_in_l = jnp.concatenate([il(w_in[l][:, :3 * D_RWKV]), w_in[l][:, 3 * D_RWKV:]], axis=1).astype(BF16)
        p_r, p_h, gates = _inproj(x, mod, w_in_l, mod_map)
        scan_ops, bonus, g = _prep(p_r, lp, tiles_ctx, tiles_lat, n_ctx_tiles)
        ys, s_ctx = _wkv_group(scan_ops, s0_ctx, seq, 0, batch, vsplit_ctx, None)
        (y_f, y_b), _ = _wkv_group(scan_ops, state_wkv[:, l], dec_seq, n_ctx // dec_seq, dec_batch, vsplit_lat, ys)
        ctx_states.append(s_ctx)

        kf_ctx = _hyena_filters(seq, lp)
        kf_lat = kf_ctx if dec_seq == seq else _hyena_filters(dec_seq, lp)
        y_h = jnp.concatenate([_hyena(p_h, kf_ctx, lp, batch, seq, 0),
                               _hyena(p_h, kf_lat, lp, dec_batch, dec_seq, n_ctx // dec_seq)], axis=0)

        x = _tail(y_f, y_b, bonus, g, y_h, gates, x, mod, lp, mod_map, alpha)
        x = _ffn(x, mod, fp, mod_map, alpha)

    y_prompt = x[:n_ctx].reshape(batch, seq, d)
    y_sample = x[n_ctx:].reshape(dec_batch, dec_seq, d)
    return y_prompt, y_sample, jnp.stack(ctx_states, axis=1)
```

```python
import functools
import math

import jax
import jax.numpy as jnp
import ml_dtypes
import numpy as np
from jax.experimental import pallas as pl
from jax.experimental.pallas import tpu as pltpu

F32 = jnp.float32
BF16 = jnp.bfloat16

D_MODEL = 1024
GRID_W = 64
D_RWKV = 512
RWKV_HEAD = 64
N_RWKV_HEADS = D_RWKV // RWKV_HEAD
LORA_W = 64
LORA_A = 64
LORA_G = 128
D_RWKV_PROJ = 3 * D_RWKV + 2 * LORA_W + 2 * LORA_A + LORA_G
DECAY_SCALE = math.exp(-0.5)
GN_EPS = 64e-5
D_HYENA = 512
HYENA_ORDER = 2
HYENA_EMB = 33
HYENA_BANDS = (HYENA_EMB - 1) // 2
HYENA_HIDDEN = 64
HYENA_FAST_DECAY = 0.3
HYENA_SLOW_DECAY = 1.5
HYENA_TARGET = 1e-2
D_HYENA_PROJ = (HYENA_ORDER + 1) * D_HYENA
D_IN_PROJ = D_RWKV_PROJ + D_HYENA_PROJ + 2 * D_MODEL
D_FF = 2816
N_EXPERTS = 8
TOP_K = 2
D_FF_EXPERT = 1408
N_MOD = 6
LN_EPS = 1e-5
POS_BASE = 10000.0

LANES = 128
SUBLANES = 8
VMEM_LIMIT = 56 * 1024 * 1024

ROW_TILE = 256
HALO = 16
HY_COL_TILE = 256
SCAN_VBLOCKS = 4
SCAN_TBLOCK = 64
N_SCAN_SRC = 9
RELAYOUT_TBLOCK = LANES
PAD = LANES


def _cparams(*sem):
    return pltpu.CompilerParams(dimension_semantics=sem, vmem_limit_bytes=VMEM_LIMIT)


def _dot(a, b):
    return jnp.dot(a, b, preferred_element_type=F32)


def _split2(x):
    hi = x.astype(BF16)
    lo = (x - hi.astype(F32)).astype(BF16)
    return hi, lo


def _dot3(a, b):
    ah, al = _split2(a)
    bh, bl = _split2(b)
    return _dot(ah, bh) + _dot(al, bh) + _dot(ah, bl)


def _dot3_lhs_split(ah, al, b):
    bh, bl = _split2(b)
    return _dot(ah, bh) + _dot(al, bh) + _dot(ah, bl)


def _sigmoid(x):
    return 1.0 / (1.0 + jnp.exp(-x))


def _silu(x):
    return x * _sigmoid(x)


def _head_sum(x):
    s = x[:, 0:LANES]
    for c in range(1, D_RWKV // LANES):
        s = s + x[:, c * LANES:(c + 1) * LANES]
    shift = LANES // 2
    while shift >= N_RWKV_HEADS:
        s = s + pltpu.roll(s, shift, axis=1)
        shift //= 2
    return jnp.concatenate([s] * (D_RWKV // LANES), axis=1)


def _layer_norm(z, g, b):
    mean = jnp.mean(z, axis=-1, keepdims=True)
    d = z - mean
    var = jnp.mean(d * d, axis=-1, keepdims=True)
    return d * jax.lax.rsqrt(var + LN_EPS) * g + b


def _ada_kernel(c_ref, w_ref, b_ref, o_ref):
    o_ref[...] = _dot3(_silu(c_ref[...]), w_ref[...]) + b_ref[...]


def _adaln(cond, w_ada, b_ada):
    depth, d, n = w_ada.shape
    rows = cond.shape[0]
    tn = 1536
    return pl.pallas_call(
        _ada_kernel,
        grid=(depth, n // tn),
        in_specs=[
            pl.BlockSpec((rows, d), lambda l, j: (0, 0)),
            pl.BlockSpec((None, d, tn), lambda l, j: (l, 0, j)),
            pl.BlockSpec((None, 1, tn), lambda l, j: (l, 0, j)),
        ],
        out_specs=pl.BlockSpec((None, rows, tn), lambda l, j: (l, 0, j)),
        out_shape=jax.ShapeDtypeStruct((depth, rows, n), F32),
        compiler_params=_cparams("arbitrary", "arbitrary"),
        name="adaln",
    )(cond, w_ada, b_ada.reshape(depth, 1, n))


def _mod_row_map(n_ctx_tiles, tiles_per_seq, ctx_row):
    def index_map(i, *_):
        return (jnp.where(i < n_ctx_tiles, ctx_row, (i - n_ctx_tiles) // tiles_per_seq), 0, 0)

    return index_map


def _inproj_kernel(tiles_ctx, tiles_lat, n_ctx_tiles,
                   x_ref, xprev_ref, xnext_ref, mod_ref, w_ref, mu_ref, w0_ref, w2_ref, a0_ref, a2_ref, g2_ref,
                   kk_w_ref, ka_ref, rk_ref,
                   ops_ref, bonus_ref, g_ref, ph_ref, gates_ref):
    i = pl.program_id(0)
    tm = x_ref.shape[0]
    j = jnp.where(i < n_ctx_tiles, i % tiles_ctx, (i - n_ctx_tiles) % tiles_lat)
    per_seq = jnp.where(i < n_ctx_tiles, tiles_ctx, tiles_lat)
    x_ext = jnp.concatenate([xprev_ref[...], x_ref[...], xnext_ref[...]], axis=0)
    h_ext = (x_ext * (1.0 + mod_ref[1:2, :]) + mod_ref[0:1, :]).astype(BF16)
    h = h_ext[HALO:HALO + tm]
    c0, c1 = D_RWKV_PROJ, D_RWKV_PROJ + D_HYENA_PROJ
    ph_ref[...] = _dot(h, w_ref[:, c0:c1])
    gates_ref[...] = _dot(h, w_ref[:, c1:D_IN_PROJ])
    p_ext = _dot(h_ext, w_ref[:, 0:c0])

    x = p_ext[HALO:HALO + tm]
    rowid = jax.lax.broadcasted_iota(jnp.int32, (tm, 1), 0)
    prev_row = jnp.where(j == 0, 0.0, p_ext[HALO - 1:HALO])
    next_row = jnp.where(j == per_seq - 1, 0.0, p_ext[HALO + tm:HALO + tm + 1])
    prev = jnp.where(rowid == 0, prev_row, pltpu.roll(x, 1, axis=0))
    nxt = jnp.where(rowid == tm - 1, next_row, pltpu.roll(x, tm - 1, axis=0))
    p = x + mu_ref[0:1, :] * (prev - x) + mu_ref[1:2, :] * (nxt - x)

    d = D_RWKV
    r, k, v = p[:, 0:d], p[:, d:2 * d], p[:, 2 * d:3 * d]
    low_w = p[:, 3 * d:3 * d + LANES]
    low_a = p[:, 3 * d + LANES:3 * d + 2 * LANES]
    low_g = p[:, 3 * d + 2 * LANES:3 * d + 3 * LANES]

    kk = k * kk_w_ref[...]
    kk = kk * jax.lax.rsqrt(jnp.maximum(_head_sum(kk * kk), 1e-24))
    lw = _dot3(jnp.tanh(low_w), w2_ref[...])
    la = _dot3(low_a, a2_ref[...])
    ksum = jnp.zeros_like(k)
    for dirn in range(2):
        log_w = -DECAY_SCALE * _sigmoid(w0_ref[dirn:dirn + 1, :] + lw[:, dirn * d:(dirn + 1) * d])
        a = _sigmoid(a0_ref[dirn:dirn + 1, :] + la[:, dirn * d:(dirn + 1) * d])
        k_d = k * (1.0 + (a - 1.0) * ka_ref[...])
        ops_ref[3 + 3 * dirn] = jnp.exp(log_w)
        ops_ref[4 + 3 * dirn] = k_d
        ops_ref[5 + 3 * dirn] = kk * a
        ksum = ksum + k_d
    ops_ref[0] = r
    ops_ref[1] = v
    ops_ref[2] = kk
    bonus_ref[...] = _head_sum(r * (0.5 * ksum) * rk_ref[...]) * v
    g_ref[...] = _dot3(_sigmoid(low_g), g2_ref[...])


def _inproj(x, mod, w_in_bf, lp, mod_map, tiles_ctx, tiles_lat, n_ctx_tiles):
    n = x.shape[0]
    tm = ROW_TILE
    halo = tm // HALO
    n_halo = n // HALO
    row = lambda width: pl.BlockSpec((tm, width), lambda i: (i, 0))
    full = lambda a: pl.BlockSpec(a.shape, lambda i: (0,) * a.ndim)
    params = [lp[k] for k in ("mu", "w0", "w2cat", "a0", "a2cat", "g2", "k_k", "k_a", "r_k")]
    out = lambda width: jax.ShapeDtypeStruct((n, width), F32)
    return pl.pallas_call(
        functools.partial(_inproj_kernel, tiles_ctx, tiles_lat, n_ctx_tiles),
        grid=(n // tm,),
        in_specs=[
            row(D_MODEL),
            pl.BlockSpec((HALO, D_MODEL), lambda i: (jnp.maximum(i * halo - 1, 0), 0)),
            pl.BlockSpec((HALO, D_MODEL), lambda i: (jnp.minimum((i + 1) * halo, n_halo - 1), 0)),
            pl.BlockSpec((None, N_MOD, D_MODEL), mod_map),
            pl.BlockSpec((D_MODEL, D_IN_PROJ), lambda i: (0, 0)),
        ] + [full(a) for a in params],
        out_specs=[pl.BlockSpec((N_SCAN_SRC, tm, D_RWKV), lambda i: (0, i, 0)),
                   row(D_RWKV), row(D_RWKV), row(D_HYENA_PROJ), row(2 * D_MODEL)],
        out_shape=[jax.ShapeDtypeStruct((N_SCAN_SRC, n, D_RWKV), F32), out(D_RWKV), out(D_RWKV),
                   out(D_HYENA_PROJ), out(2 * D_MODEL)],
        compiler_params=_cparams("arbitrary"),
        name="inproj",
    )(x, x, x, mod, w_in_bf, *params)


def _tail_kernel(alpha, yf_ref, yb_ref, bonus_ref, g_ref, yh_ref, gate_ref, x_ref, mod_ref,
                 gnw_ref, gnb_ref, wpa_ref, wpb_ref, wo_ref, lng_ref, lnb_ref, o_ref):
    y = yf_ref[...] + yb_ref[...]
    mean = _head_sum(y) * (1.0 / RWKV_HEAD)
    d = y - mean
    var = _head_sum(d * d) * (1.0 / RWKV_HEAD)
    y_n = d * jax.lax.rsqrt(var + GN_EPS) * gnw_ref[...] + gnb_ref[...]
    y_a = ((y_n + bonus_ref[...]) * g_ref[...]).astype(BF16)
    merged = (_sigmoid(gate_ref[:, 0:D_MODEL]) * _dot(y_a, wpa_ref[...])
              + _sigmoid(gate_ref[:, D_MODEL:2 * D_MODEL]) * _dot(yh_ref[...].astype(BF16), wpb_ref[...]))
    m = _dot(merged.astype(BF16), wo_ref[...])
    z = alpha * x_ref[...] + mod_ref[2:3, :] * m
    o_ref[...] = _layer_norm(z, lng_ref[...], lnb_ref[...])


def _tail(y_f, y_b, bonus, g, y_h, gates, x, mod, lp, mod_map, alpha):
    n = x.shape[0]
    tm = ROW_TILE
    row = lambda width: pl.BlockSpec((tm, width), lambda i: (i, 0))
    full = lambda a: pl.BlockSpec(a.shape, lambda i: (0,) * a.ndim)
    params = [lp[k] for k in ("gn_w", "gn_b", "w_pa", "w_pb", "w_o", "ln_g1", "ln_b1")]
    return pl.pallas_call(
        functools.partial(_tail_kernel, alpha),
        grid=(n // tm,),
        in_specs=[row(D_RWKV)] * 5 + [row(2 * D_MODEL), row(D_MODEL),
                                      pl.BlockSpec((None, N_MOD, D_MODEL), mod_map)]
        + [full(a) for a in params],
        out_specs=row(D_MODEL),
        out_shape=jax.ShapeDtypeStruct((n, D_MODEL), F32),
        compiler_params=_cparams("arbitrary"),
        name="mixer_tail",
    )(y_f, y_b, bonus, g, y_h, gates, x, mod, *params)


def _ffn_kernel(routed, alpha, x_ref, mod_ref, wg_ref, wu_ref, wd_ref, rw_ref, rb_ref,
                lng_ref, lnb_ref, o_ref, h_ref, acc_ref, comb_ref):
    e = pl.program_id(1)
    lane = jax.lax.broadcasted_iota(jnp.int32, comb_ref.shape, 1)

    @pl.when(e == 0)
    def _():
        h = x_ref[...] * (1.0 + mod_ref[4:5, :]) + mod_ref[3:4, :]
        h_ref[...] = h.astype(BF16)
        acc_ref[...] = jnp.zeros_like(acc_ref)
        if routed:
            logits = _dot3(h, rw_ref[...]) + rb_ref[...]
            logits = jnp.where(lane < N_EXPERTS, logits, -jnp.inf)
            ex = jnp.exp(logits - jnp.max(logits, axis=-1, keepdims=True))
            probs = ex / jnp.sum(ex, axis=-1, keepdims=True)
            p1 = jnp.max(probs, axis=-1, keepdims=True)
            i1 = jnp.min(jnp.where(probs == p1, lane, PAD), axis=-1, keepdims=True)
            rest = jnp.where(lane == i1, -1.0, probs)
            p2 = jnp.max(rest, axis=-1, keepdims=True)
            i2 = jnp.min(jnp.where(rest == p2, lane, PAD), axis=-1, keepdims=True)
            total = p1 + p2
            comb_ref[...] = jnp.where(lane == i1, p1 / total, 0.0) + jnp.where(lane == i2, p2 / total, 0.0)

    h = h_ref[...]
    act = _silu(_dot(h, wg_ref[...])) * _dot(h, wu_ref[...])
    if routed:
        act = act * jnp.sum(jnp.where(lane == e, comb_ref[...], 0.0), axis=-1, keepdims=True)
    acc_ref[...] += _dot(act.astype(BF16), wd_ref[...])

    @pl.when(e == pl.num_programs(1) - 1)
    def _():
        z = alpha * x_ref[...] + mod_ref[5:6, :] * acc_ref[...]
        o_ref[...] = _layer_norm(z, lng_ref[...], lnb_ref[...])


def _ffn(x, mod, fp, mod_map, alpha):
    n = x.shape[0]
    tm = 512
    scale = tm // ROW_TILE
    routed = fp["routed"]
    tf = D_FF_EXPERT
    if routed:
        groups = N_EXPERTS
        wg_spec = pl.BlockSpec((None, D_MODEL, tf), lambda i, e: (e, 0, 0))
        wu_spec = pl.BlockSpec((None, D_MODEL, tf), lambda i, e: (e, 0, 1))
        wd_spec = pl.BlockSpec((None, tf, D_MODEL), lambda i, e: (e, 0, 0))
    else:
        groups = D_FF // tf
        wg_spec = pl.BlockSpec((D_MODEL, tf), lambda i, e: (0, e))
        wu_spec = pl.BlockSpec((D_MODEL, tf), lambda i, e: (0, e + groups))
        wd_spec = pl.BlockSpec((tf, D_MODEL), lambda i, e: (e, 0))
    full = lambda a: pl.BlockSpec(a.shape, lambda i, e: (0,) * a.ndim)
    row = pl.BlockSpec((tm, D_MODEL), lambda i, e: (i, 0))
    mod_spec = pl.BlockSpec((None, N_MOD, D_MODEL), lambda i, e: mod_map(i * scale))
    return pl.pallas_call(
        functools.partial(_ffn_kernel, routed, alpha),
        grid=(n // tm, groups),
        in_specs=[row, mod_spec, wg_spec, wu_spec, wd_spec, full(fp["router_w"]), full(fp["router_b"]),
                  full(fp["ln_g"]), full(fp["ln_b"])],
        out_specs=row,
        out_shape=jax.ShapeDtypeStruct((n, D_MODEL), F32),
        scratch_shapes=[pltpu.VMEM((tm, D_MODEL), BF16), pltpu.VMEM((tm, D_MODEL), F32),
                        pltpu.VMEM((tm, PAD), F32)],
        compiler_params=_cparams("arbitrary", "arbitrary"),
        name="moe_ffn" if routed else "dense_ffn",
    )(x, mod, fp["w_in"], fp["w_in"], fp["w_out"], fp["router_w"], fp["router_b"], fp["ln_g"], fp["ln_b"])


def _scan_kernel(r_ref, kk_ref, w_ref, k_ref, b_ref, v_ref, s0_ref, y_ref, sfin_ref, s_ref):
    tb = r_ref.shape[1]
    n_vblocks = s_ref.shape[0]
    tile = (SUBLANES, LANES)
    backward = pl.program_id(0) % 2 == 1

    @pl.when(pl.program_id(1) == 0)
    def _():
        s_ref[...] = s0_ref[...]

    def time_of(i):
        return jnp.where(backward, tb - 1 - i, i)

    def row(ref, k, t):
        return jnp.broadcast_to(ref[k, pl.ds(t, 1), :], tile)

    for part in range(n_vblocks // SCAN_VBLOCKS):
        vbs = [part * SCAN_VBLOCKS + j for j in range(SCAN_VBLOCKS)]

        def step(i, sa, vbs=vbs):
            t = time_of(i)
            t_next = time_of(jnp.minimum(i + 1, tb - 1))
            v8 = [jnp.concatenate([v_ref[vb * SUBLANES + j, pl.ds(t, 1), :] for j in range(SUBLANES)], axis=0)
                  for vb in vbs]
            y = [jnp.zeros(tile, F32) for _ in vbs]
            sa_next = [jnp.zeros(tile, F32) for _ in vbs]
            for k in range(RWKV_HEAD):
                w, b, kd, r = row(w_ref, k, t), row(b_ref, k, t), row(k_ref, k, t), row(r_ref, k, t)
                kap = row(kk_ref, k, t_next)
                for j, vb in enumerate(vbs):
                    s = s_ref[vb, k] * w - sa[j] * b + v8[j] * kd
                    s_ref[vb, k] = s
                    y[j] = y[j] + s * r
                    sa_next[j] = sa_next[j] + s * kap
            for j, vb in enumerate(vbs):
                y_ref[t, pl.ds(vb * SUBLANES, SUBLANES), :] = y[j]
            return tuple(sa_next)

        t0 = time_of(0)
        sa0 = [jnp.zeros(tile, F32) for _ in vbs]
        for k in range(RWKV_HEAD):
            kap = row(kk_ref, k, t0)
            for j, vb in enumerate(vbs):
                sa0[j] = sa0[j] + s_ref[vb, k] * kap
        jax.lax.fori_loop(0, tb, step, tuple(sa0))

    @pl.when(pl.program_id(1) == pl.num_programs(1) - 1)
    def _():
        sfin_ref[...] = s_ref[...]


def _scan(xk, vk, s0):
    _, n_sg, hd, t_len, _ = xk.shape
    rows = vk.shape[1]
    tb = SCAN_TBLOCK
    n_t = t_len // tb
    t_of = lambda g, t: jnp.where(g % 2 == 1, n_t - 1 - t, t)
    shared = lambda o: pl.BlockSpec((None, None, hd, tb, LANES), lambda g, t: (o, g // 2, 0, t_of(g, t), 0))
    per_dir = lambda o: pl.BlockSpec((None, None, hd, tb, LANES),
                                     lambda g, t: (o + 3 * (g % 2), g // 2, 0, t_of(g, t), 0))
    st = pl.BlockSpec((None, rows // SUBLANES, hd, SUBLANES, LANES), lambda g, t: (g, 0, 0, 0, 0))
    return pl.pallas_call(
        _scan_kernel,
        grid=(2 * n_sg, n_t),
        in_specs=[shared(0), shared(1), per_dir(2), per_dir(3), per_dir(4),
                  pl.BlockSpec((None, rows, tb, LANES), lambda g, t: (g // 2, 0, t_of(g, t), 0)), st],
        out_specs=[pl.BlockSpec((None, tb, rows, LANES), lambda g, t: (g, t_of(g, t), 0, 0)), st],
        out_shape=[jax.ShapeDtypeStruct((2 * n_sg, t_len, rows, LANES), F32),
                   jax.ShapeDtypeStruct(s0.shape, F32)],
        scratch_shapes=[pltpu.VMEM(s0.shape[1:], F32)],
        compiler_params=_cparams("arbitrary", "arbitrary"),
        name="wkv_scan",
    )(xk, xk, xk, xk, xk, vk, s0)


def _to_chains_kernel(lane_parts, src_ref, o_ref, z_ref):
    n_seq_blk = src_ref.shape[0]
    chains = n_seq_blk * N_RWKV_HEADS
    for s in range(n_seq_blk):
        z_ref[pl.ds(s * D_RWKV, D_RWKV), :] = src_ref[s].T

    def heads_of(c):
        rows = [z_ref[pl.ds(pl.multiple_of(s * D_RWKV + c * N_RWKV_HEADS, N_RWKV_HEADS), N_RWKV_HEADS), :]
                for s in range(n_seq_blk)]
        return jnp.concatenate(rows, axis=0)

    def per_channel(c, carry):
        parts = {off: heads_of(off + c) for off in set(lane_parts)}
        o_ref[c] = jnp.concatenate([parts[off] for off in lane_parts], axis=0).T
        return carry

    jax.lax.fori_loop(0, o_ref.shape[0], per_channel, 0, unroll=4)


def _to_chains(src, stream_map, n_streams, lane_parts, first_blk, n_blk):
    _, _, n_seq_blk, seq_len, d = src.shape
    tb = RELAYOUT_TBLOCK
    channels = RWKV_HEAD // len(set(lane_parts))
    return pl.pallas_call(
        functools.partial(_to_chains_kernel, lane_parts),
        grid=(n_streams, n_blk, seq_len // tb),
        in_specs=[pl.BlockSpec((None, None, n_seq_blk, tb, d),
                               lambda s, g, t: (stream_map(s), first_blk + g, 0, t, 0))],
        out_specs=pl.BlockSpec((None, None, channels, tb, LANES), lambda s, g, t: (s, g, 0, t, 0)),
        out_shape=jax.ShapeDtypeStruct((n_streams, n_blk, channels, seq_len, LANES), F32),
        scratch_shapes=[pltpu.VMEM((n_seq_blk * d, tb), F32)],
        compiler_params=_cparams("arbitrary", "arbitrary", "arbitrary"),
        name="to_chains",
    )(src)


def _from_chains_kernel(n_vsplit, yf_ref, yb_ref, *refs):
    of_ref, ob_ref, z_ref = refs[-3:]
    n_seq_blk, tb, _ = of_ref.shape
    rows = RWKV_HEAD // n_vsplit
    chains = LANES // n_vsplit
    for y_ref, dst in ((yf_ref, of_ref), (yb_ref, ob_ref)):
        def per_row(v, carry, y_ref=y_ref):
            z_ref[pl.ds(pl.multiple_of(v * LANES, LANES), LANES), :] = y_ref[pl.ds(v, tb, stride=rows), :].T
            return carry

        jax.lax.fori_loop(0, rows, per_row, 0, unroll=4)
        for s in range(n_seq_blk):
            pieces = [z_ref[pl.ds(v * LANES + part * chains + s * N_RWKV_HEADS, N_RWKV_HEADS), :]
                      for part in range(n_vsplit) for v in range(rows)]
            dst[s] = jnp.concatenate(pieces, axis=0).T


def _from_chains(y, n_vsplit, n_blocks, first_blk, prev):
    n_lg, seq_len, rows, _ = y.shape
    tb = RELAYOUT_TBLOCK
    n_seq_blk = LANES // n_vsplit // N_RWKV_HEADS
    y2 = y.reshape(n_lg, seq_len * rows, LANES)
    blk = (None, n_seq_blk, tb, D_RWKV)
    shape = jax.ShapeDtypeStruct((n_blocks, n_seq_blk, seq_len, D_RWKV), F32)
    extra = [] if prev is None else [a.reshape(shape.shape) for a in prev]
    y_spec = lambda d: pl.BlockSpec((None, tb * rows, LANES), lambda g, t: (2 * g + d, t, 0))
    return pl.pallas_call(
        functools.partial(_from_chains_kernel, n_vsplit),
        grid=(n_lg // 2, seq_len // tb),
        in_specs=[y_spec(0), y_spec(1)] + [pl.BlockSpec(memory_space=pl.ANY)] * len(extra),
        out_specs=[pl.BlockSpec(blk, lambda g, t: (first_blk + g, 0, t, 0))] * 2,
        out_shape=[shape, shape],
        input_output_aliases={2 + i: i for i in range(len(extra))},
        scratch_shapes=[pltpu.VMEM((rows * LANES, tb), F32)],
        compiler_params=_cparams("arbitrary", "arbitrary"),
        name="from_chains",
    )(y2, y2, *extra)


def _wkv_group(ops, s0, seq_len, first_seq, n_seq, n_vsplit, prev_y):
    n_src, n_tok, d = ops.shape
    n_seq_blk = LANES // n_vsplit // N_RWKV_HEADS
    n_sg = n_seq // n_seq_blk
    rows = RWKV_HEAD // n_vsplit
    src = ops.reshape(n_src, n_tok // seq_len // n_seq_blk, n_seq_blk, seq_len, d)
    first_blk = first_seq // n_seq_blk
    xk = _to_chains(src, lambda s: s + jnp.where(s >= 1, 1, 0), N_SCAN_SRC - 1, (0,) * n_vsplit, first_blk, n_sg)
    vk = _to_chains(src, lambda s: 1, 1, tuple(p * rows for p in range(n_vsplit)), first_blk, n_sg)[0]
    s0c = s0.reshape(n_sg, n_seq_blk, 2, N_RWKV_HEADS, n_vsplit, rows // SUBLANES, SUBLANES, RWKV_HEAD)
    s0c = jnp.transpose(s0c, (0, 2, 5, 7, 6, 4, 1, 3)).reshape(2 * n_sg, rows // SUBLANES, RWKV_HEAD, SUBLANES, LANES)
    y, s_fin = _scan(xk, vk, s0c)
    ys = _from_chains(y, n_vsplit, n_tok // seq_len // n_seq_blk, first_blk, prev_y)
    s_fin = s_fin.reshape(n_sg, 2, rows // SUBLANES, RWKV_HEAD, SUBLANES, n_vsplit, n_seq_blk, N_RWKV_HEADS)
    s_fin = jnp.transpose(s_fin, (0, 6, 1, 7, 5, 2, 4, 3)).reshape(n_seq, 2, N_RWKV_HEADS, RWKV_HEAD, RWKV_HEAD)
    return [a.reshape(n_tok, d) for a in ys], s_fin


@functools.lru_cache(maxsize=None)
def _dft_constants(seq_len):
    n = 2 * seq_len
    idx = np.arange(seq_len)
    ang = (2.0 * np.pi / n) * ((idx[:, None] * idx[None, :]) % n)
    alt = np.where(idx % 2 == 0, 1.0, -1.0)
    f_re = np.cos(ang)
    f_im = -np.sin(ang)
    f_im[0, :] = alt
    fwd = np.concatenate([f_re, f_im], axis=0)
    c = np.full((seq_len,), 2.0)
    c[0] = 1.0
    g_re = np.cos(ang.T) * c[None, :] / n
    g_im = -2.0 * np.sin(ang.T) / n
    g_im[:, 0] = alt / n
    inv = np.concatenate([g_re, g_im], axis=1)

    def split(m):
        hi = m.astype(ml_dtypes.bfloat16)
        lo = (m - hi.astype(np.float64)).astype(ml_dtypes.bfloat16)
        return hi, lo

    return split(fwd) + split(inv)


@functools.lru_cache(maxsize=None)
def _filter_constants(seq_len):
    t = np.linspace(0.0, 1.0, seq_len)[:, None]
    f = np.linspace(1e-4, HYENA_BANDS - 1, HYENA_BANDS)
    ang = (2.0 * np.pi / seq_len) * np.arange(seq_len)[:, None] * f
    feats = np.zeros((seq_len, PAD), np.float32)
    feats[:, :HYENA_EMB] = np.concatenate([t, np.cos(ang), -np.sin(ang)], axis=-1)
    deltas = np.abs(np.linspace(math.log(HYENA_TARGET) / HYENA_SLOW_DECAY,
                                math.log(HYENA_TARGET) / HYENA_FAST_DECAY, D_HYENA))
    decay = np.exp(-t * deltas).astype(np.float32)
    return feats, np.tile(decay, (1, 2 * HYENA_ORDER))


def _filter_kernel(feats_ref, decay_ref, f1_ref, b1_ref, f2_ref, b2_ref, f3_ref, h_ref):
    hid = jnp.sin(_dot3(feats_ref[...], f1_ref[...]) + b1_ref[...])
    hid = jnp.sin(_dot3(hid, f2_ref[...]) + b2_ref[...])
    h_ref[...] = _dot3(hid, f3_ref[...]) * decay_ref[...]


def _spectrum_kernel(fh_ref, fl_ref, hf_ref, hb_ref, kf_ref):
    seq_len = hf_ref.shape[0]
    row = jax.lax.broadcasted_iota(jnp.int32, (seq_len, 1), 0)
    h_f = hf_ref[...]
    h_b = jnp.where(row == 0, 0.0, hb_ref[...])
    fh, fl = fh_ref[...], fl_ref[...]
    a = _dot3_lhs_split(fh, fl, h_f)
    b = _dot3_lhs_split(fh, fl, h_b)
    kf_ref[0:seq_len, :] = a[0:seq_len] + b[0:seq_len]
    kf_ref[seq_len:, :] = jnp.where(row == 0, a[seq_len:] + b[seq_len:], a[seq_len:] - b[seq_len:])


def _hyena_filters(seq_len, lp):
    feats, decay = _filter_constants(seq_len)
    fh, fl, _, _ = _dft_constants(seq_len)
    args = [jnp.asarray(feats), jnp.asarray(decay), lp["hy_f1"], lp["hy_fb1"], lp["hy_f2"], lp["hy_fb2"],
            lp["hy_f3"]]
    h = pl.pallas_call(
        _filter_kernel,
        out_shape=jax.ShapeDtypeStruct((seq_len, 2 * HYENA_ORDER * D_HYENA), F32),
        compiler_params=_cparams(),
        name="hyena_filter_mlp",
    )(*args)
    tc = HY_COL_TILE
    per = D_HYENA // tc
    const = pl.BlockSpec((2 * seq_len, seq_len), lambda n, c: (0, 0), pipeline_mode=pl.Buffered(1))
    return pl.pallas_call(
        _spectrum_kernel,
        grid=(HYENA_ORDER, per),
        in_specs=[const, const,
                  pl.BlockSpec((seq_len, tc), lambda n, c: (0, (2 * n) * per + c)),
                  pl.BlockSpec((seq_len, tc), lambda n, c: (0, (2 * n + 1) * per + c))],
        out_specs=pl.BlockSpec((2 * seq_len, tc), lambda n, c: (0, n * per + c)),
        out_shape=jax.ShapeDtypeStruct((2 * seq_len, HYENA_ORDER * D_HYENA), F32),
        compiler_params=_cparams("arbitrary", "arbitrary"),
        name="hyena_filter_spectrum",
    )(jnp.asarray(fh), jnp.asarray(fl), h, h)


def _hyena_kernel(pz_ref, pg1_ref, pg2_ref, cwz_ref, cwg1_ref, cwg2_ref, cbz_ref, cbg1_ref, cbg2_ref,
                  kf0_ref, kf1_ref, skip_ref, f_ref, g_ref, *rest):
    o_ref = rest[-1]
    seq_len = pz_ref.shape[0]
    row = jax.lax.broadcasted_iota(jnp.int32, (seq_len, 1), 0)

    def short_conv(p_ref, cw_ref, cb_ref):
        x = p_ref[...]
        prev = jnp.where(row == 0, 0.0, pltpu.roll(x, 1, axis=0))
        nxt = jnp.where(row == seq_len - 1, 0.0, pltpu.roll(x, seq_len - 1, axis=0))
        return cw_ref[0:1, :] * prev + cw_ref[1:2, :] * x + cw_ref[2:3, :] * nxt + cb_ref[...]

    z = short_conv(pz_ref, cwz_ref, cbz_ref)
    gates = (short_conv(pg1_ref, cwg1_ref, cbg1_ref), short_conv(pg2_ref, cwg2_ref, cbg2_ref))
    for n, kf_ref in enumerate((kf0_ref, kf1_ref)):
        zf = _dot(f_ref[...], z.astype(BF16))
        z_re, z_im = zf[0:seq_len], zf[seq_len:]
        k_re, k_im = kf_ref[0:seq_len, :], kf_ref[seq_len:, :]
        p_re = jnp.where(row == 0, z_re * k_re, z_re * k_re - z_im * k_im)
        p_im = jnp.where(row == 0, z_im * k_im, z_re * k_im + z_im * k_re)
        conv = (_dot(g_ref[:, 0:seq_len], p_re.astype(BF16)) + _dot(g_ref[:, seq_len:], p_im.astype(BF16)))
        z = gates[n] * (conv + skip_ref[n:n + 1, :] * z)
    o_ref[...] = z


def _hyena(p_h, kf, lp, n_seq, seq_len, seq_offset, prev):
    extra = [] if prev is None else [prev]
    tc = HY_COL_TILE
    per = D_HYENA // tc
    fh, _, gh, _ = [jnp.asarray(a) for a in _dft_constants(seq_len)]
    seg = lambda s: pl.BlockSpec((seq_len, tc), lambda b, c: (seq_offset + b, s * per + c))
    par = lambda rows, s: pl.BlockSpec((rows, tc), lambda b, c: (0, s * per + c))
    fconst = pl.BlockSpec((2 * seq_len, seq_len), lambda b, c: (0, 0), pipeline_mode=pl.Buffered(1))
    gconst = pl.BlockSpec((seq_len, 2 * seq_len), lambda b, c: (0, 0), pipeline_mode=pl.Buffered(1))
    cw, cb = lp["hy_conv_w"], lp["hy_conv_b"]
    return pl.pallas_call(
        _hyena_kernel,
        grid=(n_seq, per),
        in_specs=[seg(0), seg(1), seg(2), par(3, 0), par(3, 1), par(3, 2), par(1, 0), par(1, 1), par(1, 2),
                  pl.BlockSpec((2 * seq_len, tc), lambda b, c: (0, c)),
                  pl.BlockSpec((2 * seq_len, tc), lambda b, c: (0, per + c)),
                  pl.BlockSpec((HYENA_ORDER, tc), lambda b, c: (0, c)),
                  fconst, gconst] + [pl.BlockSpec(memory_space=pl.ANY)] * len(extra),
        out_specs=pl.BlockSpec((seq_len, tc), lambda b, c: (seq_offset + b, c)),
        out_shape=jax.ShapeDtypeStruct((p_h.shape[0], D_HYENA), F32),
        input_output_aliases={14: 0} if extra else {},
        compiler_params=_cparams("arbitrary", "arbitrary"),
        name="hyena_conv",
    )(p_h, p_h, p_h, cw, cw, cw, cb, cb, cb, kf, kf, lp["hy_skip"], fh, gh, *extra)


def _grid_pos_embed(n_tokens):
    rows = n_tokens // GRID_W
    row = jnp.repeat(jnp.arange(rows, dtype=F32), GRID_W)
    col = jnp.tile(jnp.arange(GRID_W, dtype=F32), rows)
    quarter = D_MODEL // 4
    omega = 1.0 / (POS_BASE ** (jnp.arange(quarter, dtype=F32) / quarter))

    def enc(pos):
        ang = pos[:, None] * omega
        return jnp.concatenate([jnp.sin(ang), jnp.cos(ang)], axis=-1)

    return jnp.concatenate([enc(row), enc(col)], axis=-1)


def _block_diag2(m):
    z = jnp.zeros_like(m[0])
    return jnp.concatenate([jnp.concatenate([m[0], z], axis=1), jnp.concatenate([z, m[1]], axis=1)], axis=0)


def _interleave_heads(a, axis=-1):
    a = jnp.moveaxis(a, axis, -1)
    lead = a.shape[:-1]
    a = a.reshape(lead + (-1, N_RWKV_HEADS, RWKV_HEAD))
    a = jnp.swapaxes(a, -1, -2).reshape(lead + (-1,))
    return jnp.moveaxis(a, -1, axis)


def _pad_to(a, rows, cols):
    return jnp.pad(a, ((0, rows - a.shape[0]), (0, cols - a.shape[1])))


def kernel(x_prompt, x_sample, c, state_wkv, c_ctx, w_ada, b_ada, w_in, rwkv_mu, rwkv_w0, rwkv_w2, rwkv_a0, rwkv_a2, rwkv_g2, rwkv_k_k, rwkv_k_a, rwkv_r_k, rwkv_gn_w, rwkv_gn_b, hy_conv_w, hy_conv_b, hy_f1, hy_fb1, hy_f2, hy_fb2, hy_f3, hy_skip, w_pa, w_pb, w_o, ln_g, ln_b, ffn_w_in, ffn_w_out, router_w, router_b, exp_w_in, exp_w_out):
    batch, seq, d = x_prompt.shape
    dec_batch, dec_seq, _ = x_sample.shape
    depth = w_in.shape[0]
    alpha = (2 * depth) ** 0.25
    n_ctx, n_lat = batch * seq, dec_batch * dec_seq
    assert seq % ROW_TILE == 0 and dec_seq % ROW_TILE == 0 and n_ctx % dec_seq == 0
    assert n_ctx % 512 == 0 and n_lat % 512 == 0 and dec_seq % 512 == 0
    assert seq % RELAYOUT_TBLOCK == 0 and dec_seq % RELAYOUT_TBLOCK == 0
    vsplit_ctx, vsplit_lat = (1 if b % 16 == 0 else 2 for b in (batch, dec_batch))
    assert batch % (16 // vsplit_ctx) == 0 and dec_batch % (16 // vsplit_lat) == 0
    assert (n_ctx // dec_seq) % (16 // vsplit_lat) == 0
    tiles_ctx, tiles_lat, n_ctx_tiles = seq // ROW_TILE, dec_seq // ROW_TILE, n_ctx // ROW_TILE
    mod_map = _mod_row_map(n_ctx_tiles, tiles_lat, dec_batch)

    x = jnp.concatenate([x_prompt.reshape(n_ctx, d),
                         (x_sample + _grid_pos_embed(dec_seq)[None]).reshape(n_lat, d)], axis=0)
    cond_rows = -(-(dec_batch + 1) // SUBLANES) * SUBLANES
    cond = jnp.zeros((cond_rows, d), F32).at[:dec_batch].set(c).at[dec_batch].set(c_ctx)
    mods = _adaln(cond, w_ada, b_ada).reshape(depth, cond_rows, N_MOD, d)

    il = _interleave_heads
    s0_ctx = jnp.zeros((batch, 2, N_RWKV_HEADS, RWKV_HEAD, RWKV_HEAD), F32)
    ctx_states = []
    for l in range(depth):
        lp = {
            "mu": jnp.concatenate([il(rwkv_mu[l][:, :3 * D_RWKV]), rwkv_mu[l][:, 3 * D_RWKV:]], axis=1),
            "w0": il(rwkv_w0[l]), "w2cat": _block_diag2(il(rwkv_w2[l])), "a0": il(rwkv_a0[l]),
            "a2cat": _block_diag2(il(rwkv_a2[l])), "g2": il(rwkv_g2[l]), "k_k": il(rwkv_k_k[l])[None],
            "k_a": il(rwkv_k_a[l])[None], "r_k": il(rwkv_r_k[l].reshape(1, D_RWKV)),
            "gn_w": il(rwkv_gn_w[l])[None], "gn_b": il(rwkv_gn_b[l])[None],
            "w_pa": il(w_pa[l], axis=0).astype(BF16), "w_pb": w_pb[l].astype(BF16), "w_o": w_o[l].astype(BF16),
            "ln_g1": ln_g[l, 0][None], "ln_b1": ln_b[l, 0][None],
            "hy_conv_w": hy_conv_w[l], "hy_conv_b": hy_conv_b[l][None], "hy_skip": hy_skip[l],
            "hy_f1": _pad_to(hy_f1[l], PAD, PAD), "hy_fb1": _pad_to(hy_fb1[l][None], 1, PAD),
            "hy_f2": _pad_to(hy_f2[l], PAD, PAD), "hy_fb2": _pad_to(hy_fb2[l][None], 1, PAD),
            "hy_f3": _pad_to(hy_f3[l], PAD, 2 * HYENA_ORDER * D_HYENA),
        }
        if l % 2 == 0:
            fp = {"routed": False, "w_in": ffn_w_in[l // 2].astype(BF16), "w_out": ffn_w_out[l // 2].astype(BF16),
                  "router_w": jnp.zeros((d, PAD), F32), "router_b": jnp.zeros((1, PAD), F32)}
        else:
            fp = {"routed": True, "w_in": exp_w_in[l // 2].astype(BF16), "w_out": exp_w_out[l // 2].astype(BF16),
                  "router_w": _pad_to(router_w[l // 2], d, PAD), "router_b": _pad_to(router_b[l // 2][None], 1, PAD)}
        fp["ln_g"], fp["ln_b"] = ln_g[l, 1][None], ln_b[l, 1][None]
        mod = mods[l]

        w_in_l = jnp.concatenate([il(w_in[l][:, :3 * D_RWKV]), w_in[l][:, 3 * D_RWKV:]], axis=1).astype(BF16)
        scan_ops, bonus, g, p_h, gates = _inproj(x, mod, w_in_l, lp, mod_map, tiles_ctx, tiles_lat, n_ctx_tiles)
        ys, s_ctx = _wkv_group(scan_ops, s0_ctx, seq, 0, batch, vsplit_ctx, None)
        (y_f, y_b), _ = _wkv_group(scan_ops, state_wkv[:, l], dec_seq, n_ctx // dec_seq, dec_batch, vsplit_lat, ys)
        ctx_states.append(s_ctx)

        kf_ctx = _hyena_filters(seq, lp)
        kf_lat = kf_ctx if dec_seq == seq else _hyena_filters(dec_seq, lp)
        y_h = _hyena(p_h, kf_ctx, lp, batch, seq, 0, None)
        y_h = _hyena(p_h, kf_lat, lp, dec_batch, dec_seq, n_ctx // dec_seq, y_h)

        x = _tail(y_f, y_b, bonus, g, y_h, gates, x, mod, lp, mod_map, alpha)
        x = _ffn(x, mod, fp, mod_map, alpha)

    y_prompt = x[:n_ctx].reshape(batch, seq, d)
    y_sample = x[n_ctx:].reshape(dec_batch, dec_seq, d)
    return y_prompt, y_sample, jnp.stack(ctx_states, axis=1)
```

```python
import functools
import math

import jax
import jax.numpy as jnp
import ml_dtypes
import numpy as np
from jax.experimental import pallas as pl
from jax.experimental.pallas import tpu as pltpu

F32 = jnp.float32
BF16 = jnp.bfloat16

D_MODEL = 1024
GRID_W = 64
D_RWKV = 512
RWKV_HEAD = 64
N_RWKV_HEADS = D_RWKV // RWKV_HEAD
LORA_W = 64
LORA_A = 64
LORA_G = 128
D_RWKV_PROJ = 3 * D_RWKV + 2 * LORA_W + 2 * LORA_A + LORA_G
DECAY_SCALE = math.exp(-0.5)
GN_EPS = 64e-5
D_HYENA = 512
HYENA_ORDER = 2
HYENA_EMB = 33
HYENA_BANDS = (HYENA_EMB - 1) // 2
HYENA_HIDDEN = 64
HYENA_FAST_DECAY = 0.3
HYENA_SLOW_DECAY = 1.5
HYENA_TARGET = 1e-2
D_HYENA_PROJ = (HYENA_ORDER + 1) * D_HYENA
D_IN_PROJ = D_RWKV_PROJ + D_HYENA_PROJ + 2 * D_MODEL
D_FF = 2816
N_EXPERTS = 8
TOP_K = 2
D_FF_EXPERT = 1408
N_MOD = 6
LN_EPS = 1e-5
POS_BASE = 10000.0

LANES = 128
SUBLANES = 8
VMEM_LIMIT = 56 * 1024 * 1024

ROW_TILE = 256
HALO = 16
HY_COL_TILE = 512
HY_BLOCK = 512
SCAN_VBLOCKS = 4
SCAN_TBLOCK = 64
N_SCAN_SRC = 9
RELAYOUT_TBLOCK = LANES
PAD = LANES


def _cparams(*sem):
    return pltpu.CompilerParams(dimension_semantics=sem, vmem_limit_bytes=VMEM_LIMIT)


def _dot(a, b):
    return jnp.dot(a, b, preferred_element_type=F32)


def _split2(x):
    hi = x.astype(BF16)
    lo = (x - hi.astype(F32)).astype(BF16)
    return hi, lo


def _dot3(a, b):
    ah, al = _split2(a)
    bh, bl = _split2(b)
    return _dot(ah, bh) + _dot(al, bh) + _dot(ah, bl)


def _dot3_lhs_split(ah, al, b):
    bh, bl = _split2(b)
    return _dot(ah, bh) + _dot(al, bh) + _dot(ah, bl)


def _sigmoid(x):
    return 1.0 / (1.0 + jnp.exp(-x))


def _silu(x):
    return x * _sigmoid(x)


def _head_sum(x):
    s = x[:, 0:LANES]
    for c in range(1, D_RWKV // LANES):
        s = s + x[:, c * LANES:(c + 1) * LANES]
    shift = LANES // 2
    while shift >= N_RWKV_HEADS:
        s = s + pltpu.roll(s, shift, axis=1)
        shift //= 2
    return jnp.concatenate([s] * (D_RWKV // LANES), axis=1)


def _layer_norm(z, g, b):
    mean = jnp.mean(z, axis=-1, keepdims=True)
    d = z - mean
    var = jnp.mean(d * d, axis=-1, keepdims=True)
    return d * jax.lax.rsqrt(var + LN_EPS) * g + b


def _ada_kernel(c_ref, w_ref, b_ref, o_ref):
    o_ref[...] = _dot3(_silu(c_ref[...]), w_ref[...]) + b_ref[...]


def _adaln(cond, w_ada, b_ada):
    depth, d, n = w_ada.shape
    rows = cond.shape[0]
    tn = 1536
    return pl.pallas_call(
        _ada_kernel,
        grid=(depth, n // tn),
        in_specs=[
            pl.BlockSpec((rows, d), lambda l, j: (0, 0)),
            pl.BlockSpec((None, d, tn), lambda l, j: (l, 0, j)),
            pl.BlockSpec((None, 1, tn), lambda l, j: (l, 0, j)),
        ],
        out_specs=pl.BlockSpec((None, rows, tn), lambda l, j: (l, 0, j)),
        out_shape=jax.ShapeDtypeStruct((depth, rows, n), F32),
        compiler_params=_cparams("arbitrary", "arbitrary"),
        name="adaln",
    )(cond, w_ada, b_ada.reshape(depth, 1, n))


def _mod_row_map(n_ctx_tiles, tiles_per_seq, ctx_row):
    def index_map(i, *_):
        return (jnp.where(i < n_ctx_tiles, ctx_row, (i - n_ctx_tiles) // tiles_per_seq), 0, 0)

    return index_map


def _inproj_kernel(tiles_ctx, tiles_lat, n_ctx_tiles,
                   x_ref, xprev_ref, xnext_ref, mod_ref, w_ref, mu_ref, w0_ref, w2_ref, a0_ref, a2_ref, g2_ref,
                   kk_w_ref, ka_ref, rk_ref,
                   ops_ref, bonus_ref, g_ref, ph_ref, gates_ref):
    i = pl.program_id(0)
    tm = x_ref.shape[0]
    j = jnp.where(i < n_ctx_tiles, i % tiles_ctx, (i - n_ctx_tiles) % tiles_lat)
    per_seq = jnp.where(i < n_ctx_tiles, tiles_ctx, tiles_lat)
    x_ext = jnp.concatenate([xprev_ref[...], x_ref[...], xnext_ref[...]], axis=0)
    h_ext = (x_ext * (1.0 + mod_ref[1:2, :]) + mod_ref[0:1, :]).astype(BF16)
    h = h_ext[HALO:HALO + tm]
    c0, c1 = D_RWKV_PROJ, D_RWKV_PROJ + D_HYENA_PROJ
    ph_ref[...] = _dot(h, w_ref[:, c0:c1])
    gates_ref[...] = _dot(h, w_ref[:, c1:D_IN_PROJ])
    p_ext = _dot(h_ext, w_ref[:, 0:c0])

    x = p_ext[HALO:HALO + tm]
    rowid = jax.lax.broadcasted_iota(jnp.int32, (tm, 1), 0)
    prev_row = jnp.where(j == 0, 0.0, p_ext[HALO - 1:HALO])
    next_row = jnp.where(j == per_seq - 1, 0.0, p_ext[HALO + tm:HALO + tm + 1])
    prev = jnp.where(rowid == 0, prev_row, pltpu.roll(x, 1, axis=0))
    nxt = jnp.where(rowid == tm - 1, next_row, pltpu.roll(x, tm - 1, axis=0))
    p = x + mu_ref[0:1, :] * (prev - x) + mu_ref[1:2, :] * (nxt - x)

    d = D_RWKV
    r, k, v = p[:, 0:d], p[:, d:2 * d], p[:, 2 * d:3 * d]
    low_w = p[:, 3 * d:3 * d + LANES]
    low_a = p[:, 3 * d + LANES:3 * d + 2 * LANES]
    low_g = p[:, 3 * d + 2 * LANES:3 * d + 3 * LANES]

    kk = k * kk_w_ref[...]
    kk = kk * jax.lax.rsqrt(jnp.maximum(_head_sum(kk * kk), 1e-24))
    lw = _dot3(jnp.tanh(low_w), w2_ref[...])
    la = _dot(low_a.astype(BF16), a2_ref[...].astype(BF16))
    ksum = jnp.zeros_like(k)
    for dirn in range(2):
        log_w = -DECAY_SCALE * _sigmoid(w0_ref[dirn:dirn + 1, :] + lw[:, dirn * d:(dirn + 1) * d])
        a = _sigmoid(a0_ref[dirn:dirn + 1, :] + la[:, dirn * d:(dirn + 1) * d])
        k_d = k * (1.0 + (a - 1.0) * ka_ref[...])
        ops_ref[3 + 3 * dirn] = jnp.exp(log_w)
        ops_ref[4 + 3 * dirn] = k_d
        ops_ref[5 + 3 * dirn] = kk * a
        ksum = ksum + k_d
    ops_ref[0] = r
    ops_ref[1] = v
    ops_ref[2] = kk
    bonus_ref[...] = _head_sum(r * (0.5 * ksum) * rk_ref[...]) * v
    g_ref[...] = _dot(_sigmoid(low_g).astype(BF16), g2_ref[...].astype(BF16))


def _inproj(x, mod, w_in_bf, lp, mod_map, tiles_ctx, tiles_lat, n_ctx_tiles):
    n = x.shape[0]
    tm = ROW_TILE
    halo = tm // HALO
    n_halo = n // HALO
    row = lambda width: pl.BlockSpec((tm, width), lambda i: (i, 0))
    full = lambda a: pl.BlockSpec(a.shape, lambda i: (0,) * a.ndim)
    params = [lp[k] for k in ("mu", "w0", "w2cat", "a0", "a2cat", "g2", "k_k", "k_a", "r_k")]
    out = lambda width: jax.ShapeDtypeStruct((n, width), F32)
    return pl.pallas_call(
        functools.partial(_inproj_kernel, tiles_ctx, tiles_lat, n_ctx_tiles),
        grid=(n // tm,),
        in_specs=[
            row(D_MODEL),
            pl.BlockSpec((HALO, D_MODEL), lambda i: (jnp.maximum(i * halo - 1, 0), 0)),
            pl.BlockSpec((HALO, D_MODEL), lambda i: (jnp.minimum((i + 1) * halo, n_halo - 1), 0)),
            pl.BlockSpec((None, N_MOD, D_MODEL), mod_map),
            pl.BlockSpec((D_MODEL, D_IN_PROJ), lambda i: (0, 0)),
        ] + [full(a) for a in params],
        out_specs=[pl.BlockSpec((N_SCAN_SRC, tm, D_RWKV), lambda i: (0, i, 0)),
                   row(D_RWKV), row(D_RWKV), row(D_HYENA_PROJ), row(2 * D_MODEL)],
        out_shape=[jax.ShapeDtypeStruct((N_SCAN_SRC, n, D_RWKV), F32), out(D_RWKV), out(D_RWKV),
                   out(D_HYENA_PROJ), out(2 * D_MODEL)],
        compiler_params=_cparams("arbitrary"),
        name="inproj",
    )(x, x, x, mod, w_in_bf, *params)


def _tail_kernel(alpha, yf_ref, yb_ref, bonus_ref, g_ref, yh_ref, gate_ref, x_ref, mod_ref,
                 gnw_ref, gnb_ref, wpa_ref, wpb_ref, wo_ref, lng_ref, lnb_ref, o_ref):
    y = yf_ref[...] + yb_ref[...]
    mean = _head_sum(y) * (1.0 / RWKV_HEAD)
    d = y - mean
    var = _head_sum(d * d) * (1.0 / RWKV_HEAD)
    y_n = d * jax.lax.rsqrt(var + GN_EPS) * gnw_ref[...] + gnb_ref[...]
    y_a = ((y_n + bonus_ref[...]) * g_ref[...]).astype(BF16)
    merged = (_sigmoid(gate_ref[:, 0:D_MODEL]) * _dot(y_a, wpa_ref[...])
              + _sigmoid(gate_ref[:, D_MODEL:2 * D_MODEL]) * _dot(yh_ref[...].astype(BF16), wpb_ref[...]))
    m = _dot(merged.astype(BF16), wo_ref[...])
    z = alpha * x_ref[...] + mod_ref[2:3, :] * m
    o_ref[...] = _layer_norm(z, lng_ref[...], lnb_ref[...])


def _tail(y_f, y_b, bonus, g, y_h, gates, x, mod, lp, mod_map, alpha):
    n = x.shape[0]
    tm = ROW_TILE
    row = lambda width: pl.BlockSpec((tm, width), lambda i: (i, 0))
    full = lambda a: pl.BlockSpec(a.shape, lambda i: (0,) * a.ndim)
    params = [lp[k] for k in ("gn_w", "gn_b", "w_pa", "w_pb", "w_o", "ln_g1", "ln_b1")]
    return pl.pallas_call(
        functools.partial(_tail_kernel, alpha),
        grid=(n // tm,),
        in_specs=[row(D_RWKV)] * 5 + [row(2 * D_MODEL), row(D_MODEL),
                                      pl.BlockSpec((None, N_MOD, D_MODEL), mod_map)]
        + [full(a) for a in params],
        out_specs=row(D_MODEL),
        out_shape=jax.ShapeDtypeStruct((n, D_MODEL), F32),
        compiler_params=_cparams("arbitrary"),
        name="mixer_tail",
    )(y_f, y_b, bonus, g, y_h, gates, x, mod, *params)


def _ffn_kernel(routed, alpha, split_tiles, x_ref, mod_ref, wg_ref, wu_ref, wd_ref, rw_ref, rb_ref,
                lng_ref, lnb_ref, *rest):
    out_refs, (h_ref, acc_ref, comb_ref) = rest[:-3], rest[-3:]
    e = pl.program_id(1)
    lane = jax.lax.broadcasted_iota(jnp.int32, comb_ref.shape, 1)

    @pl.when(e == 0)
    def _():
        h = x_ref[...] * (1.0 + mod_ref[4:5, :]) + mod_ref[3:4, :]
        h_ref[...] = h.astype(BF16)
        acc_ref[...] = jnp.zeros_like(acc_ref)
        if routed:
            logits = _dot3(h, rw_ref[...]) + rb_ref[...]
            logits = jnp.where(lane < N_EXPERTS, logits, -jnp.inf)
            ex = jnp.exp(logits - jnp.max(logits, axis=-1, keepdims=True))
            probs = ex / jnp.sum(ex, axis=-1, keepdims=True)
            p1 = jnp.max(probs, axis=-1, keepdims=True)
            i1 = jnp.min(jnp.where(probs == p1, lane, PAD), axis=-1, keepdims=True)
            rest = jnp.where(lane == i1, -1.0, probs)
            p2 = jnp.max(rest, axis=-1, keepdims=True)
            i2 = jnp.min(jnp.where(rest == p2, lane, PAD), axis=-1, keepdims=True)
            total = p1 + p2
            comb_ref[...] = jnp.where(lane == i1, p1 / total, 0.0) + jnp.where(lane == i2, p2 / total, 0.0)

    h = h_ref[...]
    act = _silu(_dot(h, wg_ref[...])) * _dot(h, wu_ref[...])
    if routed:
        act = act * jnp.sum(jnp.where(lane == e, comb_ref[...], 0.0), axis=-1, keepdims=True)
    acc_ref[...] += _dot(act.astype(BF16), wd_ref[...])

    def finish(o_ref):
        z = alpha * x_ref[...] + mod_ref[5:6, :] * acc_ref[...]
        o_ref[...] = _layer_norm(z, lng_ref[...], lnb_ref[...])

    last = e == pl.num_programs(1) - 1
    if split_tiles is None:
        pl.when(last)(lambda: finish(out_refs[0]))
    else:
        first_group = pl.program_id(0) < split_tiles
        pl.when(last & first_group)(lambda: finish(out_refs[0]))
        pl.when(last & jnp.logical_not(first_group))(lambda: finish(out_refs[1]))


def _ffn(x, mod, fp, mod_map, alpha, split_rows=None):
    n = x.shape[0]
    tm = 512
    scale = tm // ROW_TILE
    routed = fp["routed"]
    tf = D_FF_EXPERT
    if routed:
        groups = N_EXPERTS
        wg_spec = pl.BlockSpec((None, D_MODEL, tf), lambda i, e: (e, 0, 0))
        wu_spec = pl.BlockSpec((None, D_MODEL, tf), lambda i, e: (e, 0, 1))
        wd_spec = pl.BlockSpec((None, tf, D_MODEL), lambda i, e: (e, 0, 0))
    else:
        groups = D_FF // tf
        wg_spec = pl.BlockSpec((D_MODEL, tf), lambda i, e: (0, e))
        wu_spec = pl.BlockSpec((D_MODEL, tf), lambda i, e: (0, e + groups))
        wd_spec = pl.BlockSpec((tf, D_MODEL), lambda i, e: (e, 0))
    full = lambda a: pl.BlockSpec(a.shape, lambda i, e: (0,) * a.ndim)
    row = pl.BlockSpec((tm, D_MODEL), lambda i, e: (i, 0))
    mod_spec = pl.BlockSpec((None, N_MOD, D_MODEL), lambda i, e: mod_map(i * scale))
    if split_rows is None:
        split_tiles, out_specs, out_shape = None, row, jax.ShapeDtypeStruct((n, D_MODEL), F32)
    else:
        split_tiles = split_rows // tm
        out_specs = [pl.BlockSpec((tm, D_MODEL), lambda i, e: (jnp.minimum(i, split_tiles - 1), 0)),
                     pl.BlockSpec((tm, D_MODEL), lambda i, e: (jnp.maximum(i - split_tiles, 0), 0))]
        out_shape = [jax.ShapeDtypeStruct((split_rows, D_MODEL), F32),
                     jax.ShapeDtypeStruct((n - split_rows, D_MODEL), F32)]
    return pl.pallas_call(
        functools.partial(_ffn_kernel, routed, alpha, split_tiles),
        grid=(n // tm, groups),
        in_specs=[row, mod_spec, wg_spec, wu_spec, wd_spec, full(fp["router_w"]), full(fp["router_b"]),
                  full(fp["ln_g"]), full(fp["ln_b"])],
        out_specs=out_specs,
        out_shape=out_shape,
        scratch_shapes=[pltpu.VMEM((tm, D_MODEL), BF16), pltpu.VMEM((tm, D_MODEL), F32),
                        pltpu.VMEM((tm, PAD), F32)],
        compiler_params=_cparams("arbitrary", "arbitrary"),
        name="moe_ffn" if routed else "dense_ffn",
    )(x, mod, fp["w_in"], fp["w_in"], fp["w_out"], fp["router_w"], fp["router_b"], fp["ln_g"], fp["ln_b"])


def _scan_kernel(r_ref, kk_ref, w_ref, k_ref, b_ref, v_ref, s0_ref, y_ref, sfin_ref, s_ref):
    tb = r_ref.shape[1]
    n_vblocks = s_ref.shape[0]
    tile = (SUBLANES, LANES)
    backward = pl.program_id(0) % 2 == 1

    @pl.when(pl.program_id(1) == 0)
    def _():
        s_ref[...] = s0_ref[...]

    def time_of(i):
        return jnp.where(backward, tb - 1 - i, i)

    def row(ref, k, t):
        return jnp.broadcast_to(ref[k, pl.ds(t, 1), :], tile)

    for part in range(n_vblocks // SCAN_VBLOCKS):
        vbs = [part * SCAN_VBLOCKS + j for j in range(SCAN_VBLOCKS)]

        def step(i, sa, vbs=vbs):
            t = time_of(i)
            t_next = time_of(jnp.minimum(i + 1, tb - 1))
            v8 = [jnp.concatenate([v_ref[vb * SUBLANES + j, pl.ds(t, 1), :] for j in range(SUBLANES)], axis=0)
                  for vb in vbs]
            y = [jnp.zeros(tile, F32) for _ in vbs]
            sa_next = [jnp.zeros(tile, F32) for _ in vbs]
            for k in range(RWKV_HEAD):
                w, b, kd, r = row(w_ref, k, t), row(b_ref, k, t), row(k_ref, k, t), row(r_ref, k, t)
                kap = row(kk_ref, k, t_next)
                for j, vb in enumerate(vbs):
                    s = s_ref[vb, k] * w - sa[j] * b + v8[j] * kd
                    s_ref[vb, k] = s
                    y[j] = y[j] + s * r
                    sa_next[j] = sa_next[j] + s * kap
            for j, vb in enumerate(vbs):
                y_ref[t, pl.ds(vb * SUBLANES, SUBLANES), :] = y[j]
            return tuple(sa_next)

        t0 = time_of(0)
        sa0 = [jnp.zeros(tile, F32) for _ in vbs]
        for k in range(RWKV_HEAD):
            kap = row(kk_ref, k, t0)
            for j, vb in enumerate(vbs):
                sa0[j] = sa0[j] + s_ref[vb, k] * kap
        jax.lax.fori_loop(0, tb, step, tuple(sa0))

    @pl.when(pl.program_id(1) == pl.num_programs(1) - 1)
    def _():
        sfin_ref[...] = s_ref[...]


def _scan(xk, vk, s0):
    _, n_sg, hd, t_len, _ = xk.shape
    rows = vk.shape[1]
    tb = SCAN_TBLOCK
    n_t = t_len // tb
    t_of = lambda g, t: jnp.where(g % 2 == 1, n_t - 1 - t, t)
    shared = lambda o: pl.BlockSpec((None, None, hd, tb, LANES), lambda g, t: (o, g // 2, 0, t_of(g, t), 0))
    per_dir = lambda o: pl.BlockSpec((None, None, hd, tb, LANES),
                                     lambda g, t: (o + 3 * (g % 2), g // 2, 0, t_of(g, t), 0))
    st = pl.BlockSpec((None, rows // SUBLANES, hd, SUBLANES, LANES), lambda g, t: (g, 0, 0, 0, 0))
    return pl.pallas_call(
        _scan_kernel,
        grid=(2 * n_sg, n_t),
        in_specs=[shared(0), shared(1), per_dir(2), per_dir(3), per_dir(4),
                  pl.BlockSpec((None, rows, tb, LANES), lambda g, t: (g // 2, 0, t_of(g, t), 0)), st],
        out_specs=[pl.BlockSpec((None, tb, rows, LANES), lambda g, t: (g, t_of(g, t), 0, 0)), st],
        out_shape=[jax.ShapeDtypeStruct((2 * n_sg, t_len, rows, LANES), F32),
                   jax.ShapeDtypeStruct(s0.shape, F32)],
        scratch_shapes=[pltpu.VMEM(s0.shape[1:], F32)],
        compiler_params=_cparams("arbitrary", "arbitrary"),
        name="wkv_scan",
    )(xk, xk, xk, xk, xk, vk, s0)


def _to_chains_kernel(lane_parts, src_ref, o_ref, z_ref):
    n_seq_blk = src_ref.shape[0]
    chains = n_seq_blk * N_RWKV_HEADS
    for s in range(n_seq_blk):
        z_ref[pl.ds(s * D_RWKV, D_RWKV), :] = src_ref[s].T

    def heads_of(c):
        rows = [z_ref[pl.ds(pl.multiple_of(s * D_RWKV + c * N_RWKV_HEADS, N_RWKV_HEADS), N_RWKV_HEADS), :]
                for s in range(n_seq_blk)]
        return jnp.concatenate(rows, axis=0)

    def per_channel(c, carry):
        parts = {off: heads_of(off + c) for off in set(lane_parts)}
        o_ref[c] = jnp.concatenate([parts[off] for off in lane_parts], axis=0).T
        return carry

    jax.lax.fori_loop(0, o_ref.shape[0], per_channel, 0, unroll=4)


def _to_chains(src, stream_map, n_streams, lane_parts, first_blk, n_blk):
    _, _, n_seq_blk, seq_len, d = src.shape
    tb = RELAYOUT_TBLOCK
    channels = RWKV_HEAD // len(set(lane_parts))
    return pl.pallas_call(
        functools.partial(_to_chains_kernel, lane_parts),
        grid=(n_streams, n_blk, seq_len // tb),
        in_specs=[pl.BlockSpec((None, None, n_seq_blk, tb, d),
                               lambda s, g, t: (stream_map(s), first_blk + g, 0, t, 0))],
        out_specs=pl.BlockSpec((None, None, channels, tb, LANES), lambda s, g, t: (s, g, 0, t, 0)),
        out_shape=jax.ShapeDtypeStruct((n_streams, n_blk, channels, seq_len, LANES), F32),
        scratch_shapes=[pltpu.VMEM((n_seq_blk * d, tb), F32)],
        compiler_params=_cparams("arbitrary", "arbitrary", "arbitrary"),
        name="to_chains",
    )(src)


def _from_chains_kernel(n_vsplit, yf_ref, yb_ref, *refs):
    of_ref, ob_ref, z_ref = refs[-3:]
    n_seq_blk, tb, _ = of_ref.shape
    rows = RWKV_HEAD // n_vsplit
    chains = LANES // n_vsplit
    for y_ref, dst in ((yf_ref, of_ref), (yb_ref, ob_ref)):
        def per_row(v, carry, y_ref=y_ref):
            z_ref[pl.ds(pl.multiple_of(v * LANES, LANES), LANES), :] = y_ref[pl.ds(v, tb, stride=rows), :].T
            return carry

        jax.lax.fori_loop(0, rows, per_row, 0, unroll=4)
        for s in range(n_seq_blk):
            pieces = [z_ref[pl.ds(v * LANES + part * chains + s * N_RWKV_HEADS, N_RWKV_HEADS), :]
                      for part in range(n_vsplit) for v in range(rows)]
            dst[s] = jnp.concatenate(pieces, axis=0).T


def _from_chains(y, n_vsplit, n_blocks, first_blk, prev):
    n_lg, seq_len, rows, _ = y.shape
    tb = RELAYOUT_TBLOCK
    n_seq_blk = LANES // n_vsplit // N_RWKV_HEADS
    y2 = y.reshape(n_lg, seq_len * rows, LANES)
    blk = (None, n_seq_blk, tb, D_RWKV)
    shape = jax.ShapeDtypeStruct((n_blocks, n_seq_blk, seq_len, D_RWKV), F32)
    extra = [] if prev is None else [a.reshape(shape.shape) for a in prev]
    y_spec = lambda d: pl.BlockSpec((None, tb * rows, LANES), lambda g, t: (2 * g + d, t, 0))
    return pl.pallas_call(
        functools.partial(_from_chains_kernel, n_vsplit),
        grid=(n_lg // 2, seq_len // tb),
        in_specs=[y_spec(0), y_spec(1)] + [pl.BlockSpec(memory_space=pl.ANY)] * len(extra),
        out_specs=[pl.BlockSpec(blk, lambda g, t: (first_blk + g, 0, t, 0))] * 2,
        out_shape=[shape, shape],
        input_output_aliases={2 + i: i for i in range(len(extra))},
        scratch_shapes=[pltpu.VMEM((rows * LANES, tb), F32)],
        compiler_params=_cparams("arbitrary", "arbitrary"),
        name="from_chains",
    )(y2, y2, *extra)


def _wkv_group(ops, s0, seq_len, first_seq, n_seq, n_vsplit, prev_y):
    n_src, n_tok, d = ops.shape
    n_seq_blk = LANES // n_vsplit // N_RWKV_HEADS
    n_sg = n_seq // n_seq_blk
    rows = RWKV_HEAD // n_vsplit
    src = ops.reshape(n_src, n_tok // seq_len // n_seq_blk, n_seq_blk, seq_len, d)
    first_blk = first_seq // n_seq_blk
    xk = _to_chains(src, lambda s: s + jnp.where(s >= 1, 1, 0), N_SCAN_SRC - 1, (0,) * n_vsplit, first_blk, n_sg)
    vk = _to_chains(src, lambda s: 1, 1, tuple(p * rows for p in range(n_vsplit)), first_blk, n_sg)[0]
    s0c = s0.reshape(n_sg, n_seq_blk, 2, N_RWKV_HEADS, n_vsplit, rows // SUBLANES, SUBLANES, RWKV_HEAD)
    s0c = jnp.transpose(s0c, (0, 2, 5, 7, 6, 4, 1, 3)).reshape(2 * n_sg, rows // SUBLANES, RWKV_HEAD, SUBLANES, LANES)
    y, s_fin = _scan(xk, vk, s0c)
    ys = _from_chains(y, n_vsplit, n_tok // seq_len // n_seq_blk, first_blk, prev_y)
    s_fin = s_fin.reshape(n_sg, 2, rows // SUBLANES, RWKV_HEAD, SUBLANES, n_vsplit, n_seq_blk, N_RWKV_HEADS)
    s_fin = jnp.transpose(s_fin, (0, 6, 1, 7, 5, 2, 4, 3)).reshape(n_seq, 2, N_RWKV_HEADS, RWKV_HEAD, RWKV_HEAD)
    return [a.reshape(n_tok, d) for a in ys], s_fin


@functools.lru_cache(maxsize=None)
def _dft_constants(seq_len):
    n = 2 * seq_len
    idx = np.arange(seq_len)
    ang = (2.0 * np.pi / n) * ((idx[:, None] * idx[None, :]) % n)
    alt = np.where(idx % 2 == 0, 1.0, -1.0)
    f_re = np.cos(ang)
    f_im = -np.sin(ang)
    f_im[0, :] = alt
    fwd = np.concatenate([f_re, f_im], axis=0)
    c = np.full((seq_len,), 2.0)
    c[0] = 1.0
    g_re = np.cos(ang.T) * c[None, :] / n
    g_im = -2.0 * np.sin(ang.T) / n
    g_im[:, 0] = alt / n
    inv = np.concatenate([g_re, g_im], axis=1)

    def split(m):
        hi = m.astype(ml_dtypes.bfloat16)
        lo = (m - hi.astype(np.float64)).astype(ml_dtypes.bfloat16)
        return hi, lo

    return split(fwd) + split(inv)


@functools.lru_cache(maxsize=None)
def _filter_constants(seq_len):
    t = np.linspace(0.0, 1.0, seq_len)[:, None]
    f = np.linspace(1e-4, HYENA_BANDS - 1, HYENA_BANDS)
    ang = (2.0 * np.pi / seq_len) * np.arange(seq_len)[:, None] * f
    feats = np.zeros((seq_len, PAD), np.float32)
    feats[:, :HYENA_EMB] = np.concatenate([t, np.cos(ang), -np.sin(ang)], axis=-1)
    deltas = np.abs(np.linspace(math.log(HYENA_TARGET) / HYENA_SLOW_DECAY,
                                math.log(HYENA_TARGET) / HYENA_FAST_DECAY, D_HYENA))
    decay = np.exp(-t * deltas).astype(np.float32)
    return feats, np.tile(decay, (1, 2 * HYENA_ORDER))


def _filter_kernel(feats_ref, decay_ref, f1_ref, b1_ref, f2_ref, b2_ref, f3_ref, h_ref):
    hid = jnp.sin(_dot3(feats_ref[...], f1_ref[...]) + b1_ref[...])
    hid = jnp.sin(_dot3(hid, f2_ref[...]) + b2_ref[...])
    h_ref[...] = _dot3(hid, f3_ref[...]) * decay_ref[...]


def _spectrum_kernel(fh_ref, fl_ref, hf_ref, hb_ref, kf_ref):
    seq_len = hf_ref.shape[0]
    row = jax.lax.broadcasted_iota(jnp.int32, (seq_len, 1), 0)
    h_f = hf_ref[...]
    h_b = jnp.where(row == 0, 0.0, hb_ref[...])
    fh, fl = fh_ref[...], fl_ref[...]
    a = _dot3_lhs_split(fh, fl, h_f)
    b = _dot3_lhs_split(fh, fl, h_b)
    kf_ref[0:seq_len, :] = a[0:seq_len] + b[0:seq_len]
    kf_ref[seq_len:, :] = jnp.where(row == 0, a[seq_len:] + b[seq_len:], a[seq_len:] - b[seq_len:])


def _hyena_filters(seq_len, lp):
    feats, decay = _filter_constants(seq_len)
    fh, fl, _, _ = _dft_constants(seq_len)
    args = [jnp.asarray(feats), jnp.asarray(decay), lp["hy_f1"], lp["hy_fb1"], lp["hy_f2"], lp["hy_fb2"],
            lp["hy_f3"]]
    h = pl.pallas_call(
        _filter_kernel,
        out_shape=jax.ShapeDtypeStruct((seq_len, 2 * HYENA_ORDER * D_HYENA), F32),
        compiler_params=_cparams(),
        name="hyena_filter_mlp",
    )(*args)
    tc = HY_COL_TILE
    per = D_HYENA // tc
    const = pl.BlockSpec((2 * seq_len, seq_len), lambda n, c: (0, 0), pipeline_mode=pl.Buffered(1))
    return pl.pallas_call(
        _spectrum_kernel,
        grid=(HYENA_ORDER, per),
        in_specs=[const, const,
                  pl.BlockSpec((seq_len, tc), lambda n, c: (0, (2 * n) * per + c)),
                  pl.BlockSpec((seq_len, tc), lambda n, c: (0, (2 * n + 1) * per + c))],
        out_specs=pl.BlockSpec((2 * seq_len, tc), lambda n, c: (0, n * per + c)),
        out_shape=jax.ShapeDtypeStruct((2 * seq_len, HYENA_ORDER * D_HYENA), F32),
        compiler_params=_cparams("arbitrary", "arbitrary"),
        name="hyena_filter_spectrum",
    )(jnp.asarray(fh), jnp.asarray(fl), h, h)


def _hyena_kernel(pz_ref, pg1_ref, pg2_ref, cwz_ref, cwg1_ref, cwg2_ref, cbz_ref, cbg1_ref, cbg2_ref,
                  kf0_ref, kf1_ref, skip_ref, f_ref, g_ref, *rest):
    o_ref = rest[-1]
    seq_len = pz_ref.shape[0]
    row = jax.lax.broadcasted_iota(jnp.int32, (seq_len, 1), 0)

    def short_conv(p_ref, cw_ref, cb_ref, cols):
        x = p_ref[:, cols]
        prev = jnp.where(row == 0, 0.0, pltpu.roll(x, 1, axis=0))
        nxt = jnp.where(row == seq_len - 1, 0.0, pltpu.roll(x, seq_len - 1, axis=0))
        return cw_ref[0:1, cols] * prev + cw_ref[1:2, cols] * x + cw_ref[2:3, cols] * nxt + cb_ref[:, cols]

    for c0 in range(0, pz_ref.shape[1], HY_COL_TILE):
        cols = slice(c0, c0 + HY_COL_TILE)
        z = short_conv(pz_ref, cwz_ref, cbz_ref, cols)
        gates = (short_conv(pg1_ref, cwg1_ref, cbg1_ref, cols), short_conv(pg2_ref, cwg2_ref, cbg2_ref, cols))
        for n, kf_ref in enumerate((kf0_ref, kf1_ref)):
            zf = _dot(f_ref[...], z.astype(BF16))
            z_re, z_im = zf[0:seq_len], zf[seq_len:]
            k_re, k_im = kf_ref[0:seq_len, cols], kf_ref[seq_len:, cols]
            p_re = jnp.where(row == 0, z_re * k_re, z_re * k_re - z_im * k_im)
            p_im = jnp.where(row == 0, z_im * k_im, z_re * k_im + z_im * k_re)
            conv = (_dot(g_ref[:, 0:seq_len], p_re.astype(BF16)) + _dot(g_ref[:, seq_len:], p_im.astype(BF16)))
            z = gates[n] * (conv + skip_ref[n:n + 1, cols] * z)
        o_ref[:, cols] = z


def _hyena(p_h, kf, lp, n_seq, seq_len, seq_offset, prev):
    extra = [] if prev is None else [prev]
    tc = HY_BLOCK
    per = D_HYENA // tc
    fh, _, gh, _ = [jnp.asarray(a) for a in _dft_constants(seq_len)]
    once = dict(pipeline_mode=pl.Buffered(1))
    seg = lambda s: pl.BlockSpec((seq_len, tc), lambda b, c: (seq_offset + b, s * per + c))
    par = lambda rows, s: pl.BlockSpec((rows, tc), lambda b, c: (0, s * per + c))
    fconst = pl.BlockSpec((2 * seq_len, seq_len), lambda b, c: (0, 0), **once)
    gconst = pl.BlockSpec((seq_len, 2 * seq_len), lambda b, c: (0, 0), **once)
    cw, cb = lp["hy_conv_w"], lp["hy_conv_b"]
    return pl.pallas_call(
        _hyena_kernel,
        grid=(n_seq, per),
        in_specs=[seg(0), seg(1), seg(2), par(3, 0), par(3, 1), par(3, 2), par(1, 0), par(1, 1), par(1, 2),
                  pl.BlockSpec((2 * seq_len, tc), lambda b, c: (0, c), **(once if per == 1 else {})),
                  pl.BlockSpec((2 * seq_len, tc), lambda b, c: (0, per + c), **(once if per == 1 else {})),
                  pl.BlockSpec((HYENA_ORDER, tc), lambda b, c: (0, c)),
                  fconst, gconst] + [pl.BlockSpec(memory_space=pl.ANY)] * len(extra),
        out_specs=pl.BlockSpec((seq_len, tc), lambda b, c: (seq_offset + b, c)),
        out_shape=jax.ShapeDtypeStruct((p_h.shape[0], D_HYENA), F32),
        input_output_aliases={14: 0} if extra else {},
        compiler_params=_cparams("arbitrary", "arbitrary"),
        name="hyena_conv",
    )(p_h, p_h, p_h, cw, cw, cw, cb, cb, cb, kf, kf, lp["hy_skip"], fh, gh, *extra)


@functools.lru_cache(maxsize=None)
def _grid_pos_embed(n_tokens):
    rows = n_tokens // GRID_W
    row = np.repeat(np.arange(rows, dtype=np.float64), GRID_W)
    col = np.tile(np.arange(GRID_W, dtype=np.float64), rows)
    quarter = D_MODEL // 4
    omega = 1.0 / (POS_BASE ** (np.arange(quarter, dtype=np.float64) / quarter))

    def enc(pos):
        ang = pos[:, None] * omega
        return np.concatenate([np.sin(ang), np.cos(ang)], axis=-1)

    return np.concatenate([enc(row), enc(col)], axis=-1).astype(np.float32)


def _embed_kernel(n_ctx_tiles, xp_ref, xs_ref, pos_ref, o_ref):
    i = pl.program_id(0)

    @pl.when(i < n_ctx_tiles)
    def _():
        o_ref[...] = xp_ref[...]

    @pl.when(i >= n_ctx_tiles)
    def _():
        o_ref[...] = xs_ref[...] + pos_ref[...]


def _embed(x_prompt, x_sample):
    batch, seq, d = x_prompt.shape
    dec_batch, dec_seq, _ = x_sample.shape
    tm = ROW_TILE
    n_ctx_tiles, n_lat_tiles, per_seq = batch * seq // tm, dec_batch * dec_seq // tm, dec_seq // tm
    lat_tile = lambda i: jnp.maximum(i - n_ctx_tiles, 0)
    return pl.pallas_call(
        functools.partial(_embed_kernel, n_ctx_tiles),
        grid=(n_ctx_tiles + n_lat_tiles,),
        in_specs=[pl.BlockSpec((tm, d), lambda i: (jnp.minimum(i, n_ctx_tiles - 1), 0)),
                  pl.BlockSpec((tm, d), lambda i: (lat_tile(i), 0)),
                  pl.BlockSpec((tm, d), lambda i: (lat_tile(i) % per_seq, 0))],
        out_specs=pl.BlockSpec((tm, d), lambda i: (i, 0)),
        out_shape=jax.ShapeDtypeStruct((batch * seq + dec_batch * dec_seq, d), F32),
        compiler_params=_cparams("arbitrary"),
        name="embed",
    )(x_prompt.reshape(batch * seq, d), x_sample.reshape(dec_batch * dec_seq, d),
      jnp.asarray(_grid_pos_embed(dec_seq)))


def _block_diag2(m):
    z = jnp.zeros_like(m[0])
    return jnp.concatenate([jnp.concatenate([m[0], z], axis=1), jnp.concatenate([z, m[1]], axis=1)], axis=0)


def _interleave_heads(a, axis=-1):
    a = jnp.moveaxis(a, axis, -1)
    lead = a.shape[:-1]
    a = a.reshape(lead + (-1, N_RWKV_HEADS, RWKV_HEAD))
    a = jnp.swapaxes(a, -1, -2).reshape(lead + (-1,))
    return jnp.moveaxis(a, -1, axis)


def _pad_to(a, rows, cols):
    return jnp.pad(a, ((0, rows - a.shape[0]), (0, cols - a.shape[1])))


def kernel(x_prompt, x_sample, c, state_wkv, c_ctx, w_ada, b_ada, w_in, rwkv_mu, rwkv_w0, rwkv_w2, rwkv_a0, rwkv_a2, rwkv_g2, rwkv_k_k, rwkv_k_a, rwkv_r_k, rwkv_gn_w, rwkv_gn_b, hy_conv_w, hy_conv_b, hy_f1, hy_fb1, hy_f2, hy_fb2, hy_f3, hy_skip, w_pa, w_pb, w_o, ln_g, ln_b, ffn_w_in, ffn_w_out, router_w, router_b, exp_w_in, exp_w_out):
    batch, seq, d = x_prompt.shape
    dec_batch, dec_seq, _ = x_sample.shape
    depth = w_in.shape[0]
    alpha = (2 * depth) ** 0.25
    n_ctx, n_lat = batch * seq, dec_batch * dec_seq
    assert seq % ROW_TILE == 0 and dec_seq % ROW_TILE == 0 and n_ctx % dec_seq == 0
    assert n_ctx % 512 == 0 and n_lat % 512 == 0 and dec_seq % 512 == 0
    assert seq % RELAYOUT_TBLOCK == 0 and dec_seq % RELAYOUT_TBLOCK == 0
    vsplit_ctx, vsplit_lat = (1 if b % 16 == 0 else 2 for b in (batch, dec_batch))
    assert batch % (16 // vsplit_ctx) == 0 and dec_batch % (16 // vsplit_lat) == 0
    assert (n_ctx // dec_seq) % (16 // vsplit_lat) == 0
    tiles_ctx, tiles_lat, n_ctx_tiles = seq // ROW_TILE, dec_seq // ROW_TILE, n_ctx // ROW_TILE
    mod_map = _mod_row_map(n_ctx_tiles, tiles_lat, dec_batch)

    x = _embed(x_prompt, x_sample)
    cond_rows = -(-(dec_batch + 1) // SUBLANES) * SUBLANES
    cond = jnp.zeros((cond_rows, d), F32).at[:dec_batch].set(c).at[dec_batch].set(c_ctx)
    mods = _adaln(cond, w_ada, b_ada).reshape(depth, cond_rows, N_MOD, d)

    il = _interleave_heads
    s0_ctx = jnp.zeros((batch, 2, N_RWKV_HEADS, RWKV_HEAD, RWKV_HEAD), F32)
    ctx_states = []
    for l in range(depth):
        lp = {
            "mu": jnp.concatenate([il(rwkv_mu[l][:, :3 * D_RWKV]), rwkv_mu[l][:, 3 * D_RWKV:]], axis=1),
            "w0": il(rwkv_w0[l]), "w2cat": _block_diag2(il(rwkv_w2[l])), "a0": il(rwkv_a0[l]),
            "a2cat": _block_diag2(il(rwkv_a2[l])), "g2": il(rwkv_g2[l]), "k_k": il(rwkv_k_k[l])[None],
            "k_a": il(rwkv_k_a[l])[None], "r_k": il(rwkv_r_k[l].reshape(1, D_RWKV)),
            "gn_w": il(rwkv_gn_w[l])[None], "gn_b": il(rwkv_gn_b[l])[None],
            "w_pa": il(w_pa[l], axis=0).astype(BF16), "w_pb": w_pb[l].astype(BF16), "w_o": w_o[l].astype(BF16),
            "ln_g1": ln_g[l, 0][None], "ln_b1": ln_b[l, 0][None],
            "hy_conv_w": hy_conv_w[l], "hy_conv_b": hy_conv_b[l][None], "hy_skip": hy_skip[l],
            "hy_f1": _pad_to(hy_f1[l], PAD, PAD), "hy_fb1": _pad_to(hy_fb1[l][None], 1, PAD),
            "hy_f2": _pad_to(hy_f2[l], PAD, PAD), "hy_fb2": _pad_to(hy_fb2[l][None], 1, PAD),
            "hy_f3": _pad_to(hy_f3[l], PAD, 2 * HYENA_ORDER * D_HYENA),
        }
        if l % 2 == 0:
            fp = {"routed": False, "w_in": ffn_w_in[l // 2].astype(BF16), "w_out": ffn_w_out[l // 2].astype(BF16),
                  "router_w": jnp.zeros((d, PAD), F32), "router_b": jnp.zeros((1, PAD), F32)}
        else:
            fp = {"routed": True, "w_in": exp_w_in[l // 2].astype(BF16), "w_out": exp_w_out[l // 2].astype(BF16),
                  "router_w": _pad_to(router_w[l // 2], d, PAD), "router_b": _pad_to(router_b[l // 2][None], 1, PAD)}
        fp["ln_g"], fp["ln_b"] = ln_g[l, 1][None], ln_b[l, 1][None]
        mod = mods[l]

        w_in_l = jnp.concatenate([il(w_in[l][:, :3 * D_RWKV]), w_in[l][:, 3 * D_RWKV:]], axis=1).astype(BF16)
        scan_ops, bonus, g, p_h, gates = _inproj(x, mod, w_in_l, lp, mod_map, tiles_ctx, tiles_lat, n_ctx_tiles)
        ys, s_ctx = _wkv_group(scan_ops, s0_ctx, seq, 0, batch, vsplit_ctx, None)
        (y_f, y_b), _ = _wkv_group(scan_ops, state_wkv[:, l], dec_seq, n_ctx // dec_seq, dec_batch, vsplit_lat, ys)
        ctx_states.append(s_ctx)

        kf_ctx = _hyena_filters(seq, lp)
        kf_lat = kf_ctx if dec_seq == seq else _hyena_filters(dec_seq, lp)
        y_h = _hyena(p_h, kf_ctx, lp, batch, seq, 0, None)
        y_h = _hyena(p_h, kf_lat, lp, dec_batch, dec_seq, n_ctx // dec_seq, y_h)

        x = _tail(y_f, y_b, bonus, g, y_h, gates, x, mod, lp, mod_map, alpha)
        x = _ffn(x, mod, fp, mod_map, alpha, split_rows=n_ctx if l == depth - 1 else None)

    y_prompt, y_sample = x
    return (y_prompt.reshape(batch, seq, d), y_sample.reshape(dec_batch, dec_seq, d),
            jnp.stack(ctx_states, axis=1))
```

```python
import functools
import math

import jax
import jax.numpy as jnp
import ml_dtypes
import numpy as np
from jax.experimental import pallas as pl
from jax.experimental.pallas import tpu as pltpu

F32 = jnp.float32
BF16 = jnp.bfloat16

D_MODEL = 1024
GRID_W = 64
D_RWKV = 512
RWKV_HEAD = 64
N_RWKV_HEADS = D_RWKV // RWKV_HEAD
LORA_W = 64
LORA_A = 64
LORA_G = 128
D_RWKV_PROJ = 3 * D_RWKV + 2 * LORA_W + 2 * LORA_A + LORA_G
DECAY_SCALE = math.exp(-0.5)
GN_EPS = 64e-5
D_HYENA = 512
HYENA_ORDER = 2
HYENA_EMB = 33
HYENA_BANDS = (HYENA_EMB - 1) // 2
HYENA_HIDDEN = 64
HYENA_FAST_DECAY = 0.3
HYENA_SLOW_DECAY = 1.5
HYENA_TARGET = 1e-2
D_HYENA_PROJ = (HYENA_ORDER + 1) * D_HYENA
D_IN_PROJ = D_RWKV_PROJ + D_HYENA_PROJ + 2 * D_MODEL
D_FF = 2816
N_EXPERTS = 8
TOP_K = 2
D_FF_EXPERT = 1408
N_MOD = 6
LN_EPS = 1e-5
POS_BASE = 10000.0

LANES = 128
SUBLANES = 8
VMEM_LIMIT = 56 * 1024 * 1024

ROW_TILE = 256
HALO = 16
HY_COL_TILE = 512
HY_BLOCK = 512
SCAN_VBLOCKS = 4
SCAN_TBLOCK = 64
N_SCAN_SRC = 9
RELAYOUT_TBLOCK = LANES
PAD = LANES


def _cparams(*sem):
    return pltpu.CompilerParams(dimension_semantics=sem, vmem_limit_bytes=VMEM_LIMIT)


def _dot(a, b):
    return jnp.dot(a, b, preferred_element_type=F32)


def _split2(x):
    hi = x.astype(BF16)
    lo = (x - hi.astype(F32)).astype(BF16)
    return hi, lo


def _dot3(a, b):
    ah, al = _split2(a)
    bh, bl = _split2(b)
    return _dot(ah, bh) + _dot(al, bh) + _dot(ah, bl)


def _dot3_lhs_split(ah, al, b):
    bh, bl = _split2(b)
    return _dot(ah, bh) + _dot(al, bh) + _dot(ah, bl)


def _sigmoid(x):
    return 1.0 / (1.0 + jnp.exp(-x))


def _silu(x):
    return x * _sigmoid(x)


def _head_sum(x):
    s = x[:, 0:LANES]
    for c in range(1, D_RWKV // LANES):
        s = s + x[:, c * LANES:(c + 1) * LANES]
    shift = LANES // 2
    while shift >= N_RWKV_HEADS:
        s = s + pltpu.roll(s, shift, axis=1)
        shift //= 2
    return jnp.concatenate([s] * (D_RWKV // LANES), axis=1)


def _layer_norm(z, g, b):
    mean = jnp.mean(z, axis=-1, keepdims=True)
    d = z - mean
    var = jnp.mean(d * d, axis=-1, keepdims=True)
    return d * jax.lax.rsqrt(var + LN_EPS) * g + b


def _ada_kernel(c_ref, w_ref, b_ref, o_ref):
    o_ref[...] = _dot3(_silu(c_ref[...]), w_ref[...]) + b_ref[...]


def _adaln(cond, w_ada, b_ada):
    depth, d, n = w_ada.shape
    rows = cond.shape[0]
    tn = 1536
    return pl.pallas_call(
        _ada_kernel,
        grid=(depth, n // tn),
        in_specs=[
            pl.BlockSpec((rows, d), lambda l, j: (0, 0)),
            pl.BlockSpec((None, d, tn), lambda l, j: (l, 0, j)),
            pl.BlockSpec((None, 1, tn), lambda l, j: (l, 0, j)),
        ],
        out_specs=pl.BlockSpec((None, rows, tn), lambda l, j: (l, 0, j)),
        out_shape=jax.ShapeDtypeStruct((depth, rows, n), F32),
        compiler_params=_cparams("arbitrary", "arbitrary"),
        name="adaln",
    )(cond, w_ada, b_ada.reshape(depth, 1, n))


def _mod_row_map(n_ctx_tiles, tiles_per_seq, ctx_row):
    def index_map(i, *_):
        return (jnp.where(i < n_ctx_tiles, ctx_row, (i - n_ctx_tiles) // tiles_per_seq), 0, 0)

    return index_map


def _inproj_kernel(tiles_ctx, tiles_lat, n_ctx_tiles,
                   x_ref, xprev_ref, xnext_ref, mod_ref, w_ref, mu_ref, w0_ref, w2_ref, a0_ref, a2_ref, g2_ref,
                   kk_w_ref, ka_ref, rk_ref,
                   ops_ref, bonus_ref, g_ref, ph_ref, gates_ref):
    i = pl.program_id(0)
    tm = x_ref.shape[0]
    j = jnp.where(i < n_ctx_tiles, i % tiles_ctx, (i - n_ctx_tiles) % tiles_lat)
    per_seq = jnp.where(i < n_ctx_tiles, tiles_ctx, tiles_lat)
    x_ext = jnp.concatenate([xprev_ref[...], x_ref[...], xnext_ref[...]], axis=0)
    h_ext = (x_ext * (1.0 + mod_ref[1:2, :]) + mod_ref[0:1, :]).astype(BF16)
    h = h_ext[HALO:HALO + tm]
    c0, c1 = D_RWKV_PROJ, D_RWKV_PROJ + D_HYENA_PROJ
    ph_ref[...] = _dot(h, w_ref[:, c0:c1])
    gates_ref[...] = _dot(h, w_ref[:, c1:D_IN_PROJ]).astype(BF16)
    p_ext = _dot(h_ext, w_ref[:, 0:c0])

    x = p_ext[HALO:HALO + tm]
    rowid = jax.lax.broadcasted_iota(jnp.int32, (tm, 1), 0)
    prev_row = jnp.where(j == 0, 0.0, p_ext[HALO - 1:HALO])
    next_row = jnp.where(j == per_seq - 1, 0.0, p_ext[HALO + tm:HALO + tm + 1])
    prev = jnp.where(rowid == 0, prev_row, pltpu.roll(x, 1, axis=0))
    nxt = jnp.where(rowid == tm - 1, next_row, pltpu.roll(x, tm - 1, axis=0))
    p = x + mu_ref[0:1, :] * (prev - x) + mu_ref[1:2, :] * (nxt - x)

    d = D_RWKV
    r, k, v = p[:, 0:d], p[:, d:2 * d], p[:, 2 * d:3 * d]
    low_w = p[:, 3 * d:3 * d + LANES]
    low_a = p[:, 3 * d + LANES:3 * d + 2 * LANES]
    low_g = p[:, 3 * d + 2 * LANES:3 * d + 3 * LANES]

    kk = k * kk_w_ref[...]
    kk = kk * jax.lax.rsqrt(jnp.maximum(_head_sum(kk * kk), 1e-24))
    lw = _dot3(jnp.tanh(low_w), w2_ref[...])
    la = _dot(low_a.astype(BF16), a2_ref[...].astype(BF16))
    ksum = jnp.zeros_like(k)
    for dirn in range(2):
        log_w = -DECAY_SCALE * _sigmoid(w0_ref[dirn:dirn + 1, :] + lw[:, dirn * d:(dirn + 1) * d])
        a = _sigmoid(a0_ref[dirn:dirn + 1, :] + la[:, dirn * d:(dirn + 1) * d])
        k_d = k * (1.0 + (a - 1.0) * ka_ref[...])
        ops_ref[3 + 3 * dirn] = jnp.exp(log_w).T
        ops_ref[4 + 3 * dirn] = k_d.T
        ops_ref[5 + 3 * dirn] = (kk * a).T
        ksum = ksum + k_d
    ops_ref[0] = r.T
    ops_ref[1] = v.T
    ops_ref[2] = kk.T
    bonus_ref[...] = _head_sum(r * (0.5 * ksum) * rk_ref[...]) * v
    g_ref[...] = _dot(_sigmoid(low_g).astype(BF16), g2_ref[...].astype(BF16))


def _inproj(x, mod, w_in_bf, lp, mod_map, tiles_ctx, tiles_lat, n_ctx_tiles):
    n = x.shape[0]
    tm = ROW_TILE
    halo = tm // HALO
    n_halo = n // HALO
    row = lambda width: pl.BlockSpec((tm, width), lambda i: (i, 0))
    full = lambda a: pl.BlockSpec(a.shape, lambda i: (0,) * a.ndim)
    params = [lp[k] for k in ("mu", "w0", "w2cat", "a0", "a2cat", "g2", "k_k", "k_a", "r_k")]
    out = lambda width: jax.ShapeDtypeStruct((n, width), F32)
    return pl.pallas_call(
        functools.partial(_inproj_kernel, tiles_ctx, tiles_lat, n_ctx_tiles),
        grid=(n // tm,),
        in_specs=[
            row(D_MODEL),
            pl.BlockSpec((HALO, D_MODEL), lambda i: (jnp.maximum(i * halo - 1, 0), 0)),
            pl.BlockSpec((HALO, D_MODEL), lambda i: (jnp.minimum((i + 1) * halo, n_halo - 1), 0)),
            pl.BlockSpec((None, N_MOD, D_MODEL), mod_map),
            pl.BlockSpec((D_MODEL, D_IN_PROJ), lambda i: (0, 0)),
        ] + [full(a) for a in params],
        out_specs=[pl.BlockSpec((N_SCAN_SRC, D_RWKV, tm), lambda i: (0, 0, i)),
                   row(D_RWKV), row(D_RWKV), row(D_HYENA_PROJ), row(2 * D_MODEL)],
        out_shape=[jax.ShapeDtypeStruct((N_SCAN_SRC, D_RWKV, n), F32), out(D_RWKV), out(D_RWKV),
                   out(D_HYENA_PROJ), jax.ShapeDtypeStruct((n, 2 * D_MODEL), BF16)],
        compiler_params=_cparams("arbitrary"),
        name="inproj",
    )(x, x, x, mod, w_in_bf, *params)


def _tail_kernel(alpha, yf_ref, yb_ref, bonus_ref, g_ref, yh_ref, gate_ref, x_ref, mod_ref,
                 gnw_ref, gnb_ref, wpa_ref, wpb_ref, wo_ref, lng_ref, lnb_ref, o_ref):
    y = yf_ref[...] + yb_ref[...]
    mean = _head_sum(y) * (1.0 / RWKV_HEAD)
    d = y - mean
    var = _head_sum(d * d) * (1.0 / RWKV_HEAD)
    y_n = d * jax.lax.rsqrt(var + GN_EPS) * gnw_ref[...] + gnb_ref[...]
    y_a = ((y_n + bonus_ref[...]) * g_ref[...]).astype(BF16)
    merged = (_sigmoid(gate_ref[:, 0:D_MODEL].astype(F32)) * _dot(y_a, wpa_ref[...])
              + _sigmoid(gate_ref[:, D_MODEL:2 * D_MODEL].astype(F32)) * _dot(yh_ref[...], wpb_ref[...]))
    m = _dot(merged.astype(BF16), wo_ref[...])
    z = alpha * x_ref[...] + mod_ref[2:3, :] * m
    o_ref[...] = _layer_norm(z, lng_ref[...], lnb_ref[...])


def _tail(y_f, y_b, bonus, g, y_h, gates, x, mod, lp, mod_map, alpha):
    n = x.shape[0]
    tm = ROW_TILE
    row = lambda width: pl.BlockSpec((tm, width), lambda i: (i, 0))
    full = lambda a: pl.BlockSpec(a.shape, lambda i: (0,) * a.ndim)
    params = [lp[k] for k in ("gn_w", "gn_b", "w_pa", "w_pb", "w_o", "ln_g1", "ln_b1")]
    return pl.pallas_call(
        functools.partial(_tail_kernel, alpha),
        grid=(n // tm,),
        in_specs=[row(D_RWKV)] * 5 + [row(2 * D_MODEL), row(D_MODEL),
                                      pl.BlockSpec((None, N_MOD, D_MODEL), mod_map)]
        + [full(a) for a in params],
        out_specs=row(D_MODEL),
        out_shape=jax.ShapeDtypeStruct((n, D_MODEL), F32),
        compiler_params=_cparams("arbitrary"),
        name="mixer_tail",
    )(y_f, y_b, bonus, g, y_h, gates, x, mod, *params)


def _ffn_kernel(routed, alpha, split_tiles, x_ref, mod_ref, wg_ref, wu_ref, wd_ref, rw_ref, rb_ref,
                lng_ref, lnb_ref, *rest):
    out_refs, (h_ref, acc_ref, comb_ref) = rest[:-3], rest[-3:]
    e = pl.program_id(1)
    lane = jax.lax.broadcasted_iota(jnp.int32, comb_ref.shape, 1)

    @pl.when(e == 0)
    def _():
        h = x_ref[...] * (1.0 + mod_ref[4:5, :]) + mod_ref[3:4, :]
        h_ref[...] = h.astype(BF16)
        acc_ref[...] = jnp.zeros_like(acc_ref)
        if routed:
            logits = _dot3(h, rw_ref[...]) + rb_ref[...]
            logits = jnp.where(lane < N_EXPERTS, logits, -jnp.inf)
            ex = jnp.exp(logits - jnp.max(logits, axis=-1, keepdims=True))
            probs = ex / jnp.sum(ex, axis=-1, keepdims=True)
            p1 = jnp.max(probs, axis=-1, keepdims=True)
            i1 = jnp.min(jnp.where(probs == p1, lane, PAD), axis=-1, keepdims=True)
            rest = jnp.where(lane == i1, -1.0, probs)
            p2 = jnp.max(rest, axis=-1, keepdims=True)
            i2 = jnp.min(jnp.where(rest == p2, lane, PAD), axis=-1, keepdims=True)
            total = p1 + p2
            comb_ref[...] = jnp.where(lane == i1, p1 / total, 0.0) + jnp.where(lane == i2, p2 / total, 0.0)

    h = h_ref[...]
    act = _silu(_dot(h, wg_ref[...])) * _dot(h, wu_ref[...])
    if routed:
        act = act * jnp.sum(jnp.where(lane == e, comb_ref[...], 0.0), axis=-1, keepdims=True)
    acc_ref[...] += _dot(act.astype(BF16), wd_ref[...])

    def finish(o_ref):
        z = alpha * x_ref[...] + mod_ref[5:6, :] * acc_ref[...]
        o_ref[...] = _layer_norm(z, lng_ref[...], lnb_ref[...])

    last = e == pl.num_programs(1) - 1
    if split_tiles is None:
        pl.when(last)(lambda: finish(out_refs[0]))
    else:
        first_group = pl.program_id(0) < split_tiles
        pl.when(last & first_group)(lambda: finish(out_refs[0]))
        pl.when(last & jnp.logical_not(first_group))(lambda: finish(out_refs[1]))


def _ffn(x, mod, fp, mod_map, alpha, split_rows=None):
    n = x.shape[0]
    tm = 512
    scale = tm // ROW_TILE
    routed = fp["routed"]
    tf = D_FF_EXPERT
    if routed:
        groups = N_EXPERTS
        wg_spec = pl.BlockSpec((None, D_MODEL, tf), lambda i, e: (e, 0, 0))
        wu_spec = pl.BlockSpec((None, D_MODEL, tf), lambda i, e: (e, 0, 1))
        wd_spec = pl.BlockSpec((None, tf, D_MODEL), lambda i, e: (e, 0, 0))
    else:
        groups = D_FF // tf
        wg_spec = pl.BlockSpec((D_MODEL, tf), lambda i, e: (0, e))
        wu_spec = pl.BlockSpec((D_MODEL, tf), lambda i, e: (0, e + groups))
        wd_spec = pl.BlockSpec((tf, D_MODEL), lambda i, e: (e, 0))
    full = lambda a: pl.BlockSpec(a.shape, lambda i, e: (0,) * a.ndim)
    row = pl.BlockSpec((tm, D_MODEL), lambda i, e: (i, 0))
    mod_spec = pl.BlockSpec((None, N_MOD, D_MODEL), lambda i, e: mod_map(i * scale))
    if split_rows is None:
        split_tiles, out_specs, out_shape = None, row, jax.ShapeDtypeStruct((n, D_MODEL), F32)
    else:
        split_tiles = split_rows // tm
        out_specs = [pl.BlockSpec((tm, D_MODEL), lambda i, e: (jnp.minimum(i, split_tiles - 1), 0)),
                     pl.BlockSpec((tm, D_MODEL), lambda i, e: (jnp.maximum(i - split_tiles, 0), 0))]
        out_shape = [jax.ShapeDtypeStruct((split_rows, D_MODEL), F32),
                     jax.ShapeDtypeStruct((n - split_rows, D_MODEL), F32)]
    return pl.pallas_call(
        functools.partial(_ffn_kernel, routed, alpha, split_tiles),
        grid=(n // tm, groups),
        in_specs=[row, mod_spec, wg_spec, wu_spec, wd_spec, full(fp["router_w"]), full(fp["router_b"]),
                  full(fp["ln_g"]), full(fp["ln_b"])],
        out_specs=out_specs,
        out_shape=out_shape,
        scratch_shapes=[pltpu.VMEM((tm, D_MODEL), BF16), pltpu.VMEM((tm, D_MODEL), F32),
                        pltpu.VMEM((tm, PAD), F32)],
        compiler_params=_cparams("arbitrary", "arbitrary"),
        name="moe_ffn" if routed else "dense_ffn",
    )(x, mod, fp["w_in"], fp["w_in"], fp["w_out"], fp["router_w"], fp["router_b"], fp["ln_g"], fp["ln_b"])


def _scan_kernel(r_ref, kk_ref, w_ref, k_ref, b_ref, v_ref, s0_ref, y_ref, sfin_ref, s_ref):
    tb = r_ref.shape[1]
    n_vblocks = s_ref.shape[0]
    tile = (SUBLANES, LANES)
    backward = pl.program_id(0) % 2 == 1

    @pl.when(pl.program_id(1) == 0)
    def _():
        s_ref[...] = s0_ref[...]

    def time_of(i):
        return jnp.where(backward, tb - 1 - i, i)

    def row(ref, k, t):
        return jnp.broadcast_to(ref[k, pl.ds(t, 1), :], tile)

    for part in range(n_vblocks // SCAN_VBLOCKS):
        vbs = [part * SCAN_VBLOCKS + j for j in range(SCAN_VBLOCKS)]

        def step(i, sa, vbs=vbs):
            t = time_of(i)
            t_next = time_of(jnp.minimum(i + 1, tb - 1))
            v8 = [jnp.concatenate([v_ref[vb * SUBLANES + j, pl.ds(t, 1), :] for j in range(SUBLANES)], axis=0)
                  for vb in vbs]
            y = [jnp.zeros(tile, F32) for _ in vbs]
            sa_next = [jnp.zeros(tile, F32) for _ in vbs]
            for k in range(RWKV_HEAD):
                w, b, kd, r = row(w_ref, k, t), row(b_ref, k, t), row(k_ref, k, t), row(r_ref, k, t)
                kap = row(kk_ref, k, t_next)
                for j, vb in enumerate(vbs):
                    s = s_ref[vb, k] * w - sa[j] * b + v8[j] * kd
                    s_ref[vb, k] = s
                    y[j] = y[j] + s * r
                    sa_next[j] = sa_next[j] + s * kap
            for j, vb in enumerate(vbs):
                y_ref[t, pl.ds(vb * SUBLANES, SUBLANES), :] = y[j]
            return tuple(sa_next)

        t0 = time_of(0)
        sa0 = [jnp.zeros(tile, F32) for _ in vbs]
        for k in range(RWKV_HEAD):
            kap = row(kk_ref, k, t0)
            for j, vb in enumerate(vbs):
                sa0[j] = sa0[j] + s_ref[vb, k] * kap
        jax.lax.fori_loop(0, tb, step, tuple(sa0))

    @pl.when(pl.program_id(1) == pl.num_programs(1) - 1)
    def _():
        sfin_ref[...] = s_ref[...]


def _scan(xk, vk, s0):
    _, n_sg, hd, t_len, _ = xk.shape
    rows = vk.shape[1]
    tb = SCAN_TBLOCK
    n_t = t_len // tb
    t_of = lambda g, t: jnp.where(g % 2 == 1, n_t - 1 - t, t)
    shared = lambda o: pl.BlockSpec((None, None, hd, tb, LANES), lambda g, t: (o, g // 2, 0, t_of(g, t), 0))
    per_dir = lambda o: pl.BlockSpec((None, None, hd, tb, LANES),
                                     lambda g, t: (o + 3 * (g % 2), g // 2, 0, t_of(g, t), 0))
    st = pl.BlockSpec((None, rows // SUBLANES, hd, SUBLANES, LANES), lambda g, t: (g, 0, 0, 0, 0))
    return pl.pallas_call(
        _scan_kernel,
        grid=(2 * n_sg, n_t),
        in_specs=[shared(0), shared(1), per_dir(2), per_dir(3), per_dir(4),
                  pl.BlockSpec((None, rows, tb, LANES), lambda g, t: (g // 2, 0, t_of(g, t), 0)), st],
        out_specs=[pl.BlockSpec((None, tb, rows, LANES), lambda g, t: (g, t_of(g, t), 0, 0)), st],
        out_shape=[jax.ShapeDtypeStruct((2 * n_sg, t_len, rows, LANES), F32),
                   jax.ShapeDtypeStruct(s0.shape, F32)],
        scratch_shapes=[pltpu.VMEM(s0.shape[1:], F32)],
        compiler_params=_cparams("arbitrary", "arbitrary"),
        name="wkv_scan",
    )(xk, xk, xk, xk, xk, vk, s0)


def _to_chains_kernel(lane_parts, *refs):
    src_refs, o_ref = refs[:-1], refs[-1]

    def heads_of(c):
        rows = [src[pl.ds(pl.multiple_of(c * N_RWKV_HEADS, N_RWKV_HEADS), N_RWKV_HEADS), :] for src in src_refs]
        return jnp.concatenate(rows, axis=0)

    def per_channel(c, carry):
        parts = {off: heads_of(off + c) for off in set(lane_parts)}
        o_ref[c] = jnp.concatenate([parts[off] for off in lane_parts], axis=0).T
        return carry

    jax.lax.fori_loop(0, o_ref.shape[0], per_channel, 0, unroll=4)


def _to_chains(src, stream_map, n_streams, lane_parts, seq_len, n_seq_blk, first_blk, n_blk):
    tb = RELAYOUT_TBLOCK
    n_t = seq_len // tb
    channels = RWKV_HEAD // len(set(lane_parts))

    def seq_spec(j):
        return pl.BlockSpec((None, D_RWKV, tb),
                            lambda s, g, t: (stream_map(s), 0, ((first_blk + g) * n_seq_blk + j) * n_t + t))

    return pl.pallas_call(
        functools.partial(_to_chains_kernel, lane_parts),
        grid=(n_streams, n_blk, n_t),
        in_specs=[seq_spec(j) for j in range(n_seq_blk)],
        out_specs=pl.BlockSpec((None, None, channels, tb, LANES), lambda s, g, t: (s, g, 0, t, 0)),
        out_shape=jax.ShapeDtypeStruct((n_streams, n_blk, channels, seq_len, LANES), F32),
        compiler_params=_cparams("arbitrary", "arbitrary", "arbitrary"),
        name="to_chains",
    )(*([src] * n_seq_blk))


def _from_chains_kernel(n_vsplit, yf_ref, yb_ref, *refs):
    of_ref, ob_ref, z_ref = refs[-3:]
    n_seq_blk, tb, _ = of_ref.shape
    rows = RWKV_HEAD // n_vsplit
    chains = LANES // n_vsplit
    for y_ref, dst in ((yf_ref, of_ref), (yb_ref, ob_ref)):
        def per_row(v, carry, y_ref=y_ref):
            z_ref[pl.ds(pl.multiple_of(v * LANES, LANES), LANES), :] = y_ref[pl.ds(v, tb, stride=rows), :].T
            return carry

        jax.lax.fori_loop(0, rows, per_row, 0, unroll=4)
        for s in range(n_seq_blk):
            pieces = [z_ref[pl.ds(v * LANES + part * chains + s * N_RWKV_HEADS, N_RWKV_HEADS), :]
                      for part in range(n_vsplit) for v in range(rows)]
            dst[s] = jnp.concatenate(pieces, axis=0).T


def _from_chains(y, n_vsplit, n_blocks, first_blk, prev):
    n_lg, seq_len, rows, _ = y.shape
    tb = RELAYOUT_TBLOCK
    n_seq_blk = LANES // n_vsplit // N_RWKV_HEADS
    y2 = y.reshape(n_lg, seq_len * rows, LANES)
    blk = (None, n_seq_blk, tb, D_RWKV)
    shape = jax.ShapeDtypeStruct((n_blocks, n_seq_blk, seq_len, D_RWKV), F32)
    extra = [] if prev is None else [a.reshape(shape.shape) for a in prev]
    y_spec = lambda d: pl.BlockSpec((None, tb * rows, LANES), lambda g, t: (2 * g + d, t, 0))
    return pl.pallas_call(
        functools.partial(_from_chains_kernel, n_vsplit),
        grid=(n_lg // 2, seq_len // tb),
        in_specs=[y_spec(0), y_spec(1)] + [pl.BlockSpec(memory_space=pl.ANY)] * len(extra),
        out_specs=[pl.BlockSpec(blk, lambda g, t: (first_blk + g, 0, t, 0))] * 2,
        out_shape=[shape, shape],
        input_output_aliases={2 + i: i for i in range(len(extra))},
        scratch_shapes=[pltpu.VMEM((rows * LANES, tb), F32)],
        compiler_params=_cparams("arbitrary", "arbitrary"),
        name="from_chains",
    )(y2, y2, *extra)


def _wkv_group(ops, s0, seq_len, first_seq, n_seq, n_vsplit, prev_y):
    _, d, n_tok = ops.shape
    n_seq_blk = LANES // n_vsplit // N_RWKV_HEADS
    n_sg = n_seq // n_seq_blk
    rows = RWKV_HEAD // n_vsplit
    first_blk = first_seq // n_seq_blk
    xk = _to_chains(ops, lambda s: s + jnp.where(s >= 1, 1, 0), N_SCAN_SRC - 1, (0,) * n_vsplit,
                    seq_len, n_seq_blk, first_blk, n_sg)
    vk = _to_chains(ops, lambda s: 1, 1, tuple(p * rows for p in range(n_vsplit)),
                    seq_len, n_seq_blk, first_blk, n_sg)[0]
    s0c = s0.reshape(n_sg, n_seq_blk, 2, N_RWKV_HEADS, n_vsplit, rows // SUBLANES, SUBLANES, RWKV_HEAD)
    s0c = jnp.transpose(s0c, (0, 2, 5, 7, 6, 4, 1, 3)).reshape(2 * n_sg, rows // SUBLANES, RWKV_HEAD, SUBLANES, LANES)
    y, s_fin = _scan(xk, vk, s0c)
    ys = _from_chains(y, n_vsplit, n_tok // seq_len // n_seq_blk, first_blk, prev_y)
    s_fin = s_fin.reshape(n_sg, 2, rows // SUBLANES, RWKV_HEAD, SUBLANES, n_vsplit, n_seq_blk, N_RWKV_HEADS)
    s_fin = jnp.transpose(s_fin, (0, 6, 1, 7, 5, 2, 4, 3)).reshape(n_seq, 2, N_RWKV_HEADS, RWKV_HEAD, RWKV_HEAD)
    return [a.reshape(n_tok, d) for a in ys], s_fin


@functools.lru_cache(maxsize=None)
def _dft_constants(seq_len):
    n = 2 * seq_len
    idx = np.arange(seq_len)
    ang = (2.0 * np.pi / n) * ((idx[:, None] * idx[None, :]) % n)
    alt = np.where(idx % 2 == 0, 1.0, -1.0)
    f_re = np.cos(ang)
    f_im = -np.sin(ang)
    f_im[0, :] = alt
    fwd = np.concatenate([f_re, f_im], axis=0)
    c = np.full((seq_len,), 2.0)
    c[0] = 1.0
    g_re = np.cos(ang.T) * c[None, :] / n
    g_im = -2.0 * np.sin(ang.T) / n
    g_im[:, 0] = alt / n
    inv = np.concatenate([g_re, g_im], axis=1)

    def split(m):
        hi = m.astype(ml_dtypes.bfloat16)
        lo = (m - hi.astype(np.float64)).astype(ml_dtypes.bfloat16)
        return hi, lo

    return split(fwd) + split(inv)


@functools.lru_cache(maxsize=None)
def _filter_constants(seq_len):
    t = np.linspace(0.0, 1.0, seq_len)[:, None]
    f = np.linspace(1e-4, HYENA_BANDS - 1, HYENA_BANDS)
    ang = (2.0 * np.pi / seq_len) * np.arange(seq_len)[:, None] * f
    feats = np.zeros((seq_len, PAD), np.float32)
    feats[:, :HYENA_EMB] = np.concatenate([t, np.cos(ang), -np.sin(ang)], axis=-1)
    deltas = np.abs(np.linspace(math.log(HYENA_TARGET) / HYENA_SLOW_DECAY,
                                math.log(HYENA_TARGET) / HYENA_FAST_DECAY, D_HYENA))
    decay = np.exp(-t * deltas).astype(np.float32)
    return feats, np.tile(decay, (1, 2 * HYENA_ORDER))


def _filter_kernel(feats_ref, decay_ref, f1_ref, b1_ref, f2_ref, b2_ref, f3_ref, h_ref):
    hid = jnp.sin(_dot3(feats_ref[...], f1_ref[...]) + b1_ref[...])
    hid = jnp.sin(_dot3(hid, f2_ref[...]) + b2_ref[...])
    h_ref[...] = _dot3(hid, f3_ref[...]) * decay_ref[...]


def _spectrum_kernel(fh_ref, fl_ref, hf_ref, hb_ref, kf_ref):
    seq_len = hf_ref.shape[0]
    row = jax.lax.broadcasted_iota(jnp.int32, (seq_len, 1), 0)
    h_f = hf_ref[...]
    h_b = jnp.where(row == 0, 0.0, hb_ref[...])
    fh, fl = fh_ref[...], fl_ref[...]
    a = _dot3_lhs_split(fh, fl, h_f)
    b = _dot3_lhs_split(fh, fl, h_b)
    kf_ref[0:seq_len, :] = a[0:seq_len] + b[0:seq_len]
    kf_ref[seq_len:, :] = jnp.where(row == 0, a[seq_len:] + b[seq_len:], a[seq_len:] - b[seq_len:])


def _hyena_filters(seq_len, lp):
    feats, decay = _filter_constants(seq_len)
    fh, fl, _, _ = _dft_constants(seq_len)
    args = [jnp.asarray(feats), jnp.asarray(decay), lp["hy_f1"], lp["hy_fb1"], lp["hy_f2"], lp["hy_fb2"],
            lp["hy_f3"]]
    h = pl.pallas_call(
        _filter_kernel,
        out_shape=jax.ShapeDtypeStruct((seq_len, 2 * HYENA_ORDER * D_HYENA), F32),
        compiler_params=_cparams(),
        name="hyena_filter_mlp",
    )(*args)
    tc = HY_COL_TILE
    per = D_HYENA // tc
    const = pl.BlockSpec((2 * seq_len, seq_len), lambda n, c: (0, 0), pipeline_mode=pl.Buffered(1))
    return pl.pallas_call(
        _spectrum_kernel,
        grid=(HYENA_ORDER, per),
        in_specs=[const, const,
                  pl.BlockSpec((seq_len, tc), lambda n, c: (0, (2 * n) * per + c)),
                  pl.BlockSpec((seq_len, tc), lambda n, c: (0, (2 * n + 1) * per + c))],
        out_specs=pl.BlockSpec((2 * seq_len, tc), lambda n, c: (0, n * per + c)),
        out_shape=jax.ShapeDtypeStruct((2 * seq_len, HYENA_ORDER * D_HYENA), F32),
        compiler_params=_cparams("arbitrary", "arbitrary"),
        name="hyena_filter_spectrum",
    )(jnp.asarray(fh), jnp.asarray(fl), h, h)


def _hyena_kernel(pz_ref, pg1_ref, pg2_ref, cwz_ref, cwg1_ref, cwg2_ref, cbz_ref, cbg1_ref, cbg2_ref,
                  kf0_ref, kf1_ref, skip_ref, f_ref, g_ref, *rest):
    o_ref = rest[-1]
    seq_len = pz_ref.shape[0]
    row = jax.lax.broadcasted_iota(jnp.int32, (seq_len, 1), 0)

    def short_conv(p_ref, cw_ref, cb_ref, cols):
        x = p_ref[:, cols]
        prev = jnp.where(row == 0, 0.0, pltpu.roll(x, 1, axis=0))
        nxt = jnp.where(row == seq_len - 1, 0.0, pltpu.roll(x, seq_len - 1, axis=0))
        return cw_ref[0:1, cols] * prev + cw_ref[1:2, cols] * x + cw_ref[2:3, cols] * nxt + cb_ref[:, cols]

    for c0 in range(0, pz_ref.shape[1], HY_COL_TILE):
        cols = slice(c0, c0 + HY_COL_TILE)
        z = short_conv(pz_ref, cwz_ref, cbz_ref, cols)
        gates = (short_conv(pg1_ref, cwg1_ref, cbg1_ref, cols), short_conv(pg2_ref, cwg2_ref, cbg2_ref, cols))
        for n, kf_ref in enumerate((kf0_ref, kf1_ref)):
            zf = _dot(f_ref[...], z.astype(BF16))
            z_re, z_im = zf[0:seq_len], zf[seq_len:]
            k_re, k_im = kf_ref[0:seq_len, cols], kf_ref[seq_len:, cols]
            p_re = jnp.where(row == 0, z_re * k_re, z_re * k_re - z_im * k_im)
            p_im = jnp.where(row == 0, z_im * k_im, z_re * k_im + z_im * k_re)
            conv = (_dot(g_ref[:, 0:seq_len], p_re.astype(BF16)) + _dot(g_ref[:, seq_len:], p_im.astype(BF16)))
            z = gates[n] * (conv + skip_ref[n:n + 1, cols] * z)
        o_ref[:, cols] = z.astype(o_ref.dtype)


def _hyena(p_h, kf, lp, n_seq, seq_len, seq_offset, prev):
    extra = [] if prev is None else [prev]
    tc = HY_BLOCK
    per = D_HYENA // tc
    fh, _, gh, _ = [jnp.asarray(a) for a in _dft_constants(seq_len)]
    once = dict(pipeline_mode=pl.Buffered(1))
    seg = lambda s: pl.BlockSpec((seq_len, tc), lambda b, c: (seq_offset + b, s * per + c))
    par = lambda rows, s: pl.BlockSpec((rows, tc), lambda b, c: (0, s * per + c))
    fconst = pl.BlockSpec((2 * seq_len, seq_len), lambda b, c: (0, 0), **once)
    gconst = pl.BlockSpec((seq_len, 2 * seq_len), lambda b, c: (0, 0), **once)
    cw, cb = lp["hy_conv_w"], lp["hy_conv_b"]
    return pl.pallas_call(
        _hyena_kernel,
        grid=(n_seq, per),
        in_specs=[seg(0), seg(1), seg(2), par(3, 0), par(3, 1), par(3, 2), par(1, 0), par(1, 1), par(1, 2),
                  pl.BlockSpec((2 * seq_len, tc), lambda b, c: (0, c), **(once if per == 1 else {})),
                  pl.BlockSpec((2 * seq_len, tc), lambda b, c: (0, per + c), **(once if per == 1 else {})),
                  pl.BlockSpec((HYENA_ORDER, tc), lambda b, c: (0, c)),
                  fconst, gconst] + [pl.BlockSpec(memory_space=pl.ANY)] * len(extra),
        out_specs=pl.BlockSpec((seq_len, tc), lambda b, c: (seq_offset + b, c)),
        out_shape=jax.ShapeDtypeStruct((p_h.shape[0], D_HYENA), BF16),
        input_output_aliases={14: 0} if extra else {},
        compiler_params=_cparams("arbitrary", "arbitrary"),
        name="hyena_conv",
    )(p_h, p_h, p_h, cw, cw, cw, cb, cb, cb, kf, kf, lp["hy_skip"], fh, gh, *extra)


@functools.lru_cache(maxsize=None)
def _grid_pos_embed(n_tokens):
    rows = n_tokens // GRID_W
    row = np.repeat(np.arange(rows, dtype=np.float64), GRID_W)
    col = np.tile(np.arange(GRID_W, dtype=np.float64), rows)
    quarter = D_MODEL // 4
    omega = 1.0 / (POS_BASE ** (np.arange(quarter, dtype=np.float64) / quarter))

    def enc(pos):
        ang = pos[:, None] * omega
        return np.concatenate([np.sin(ang), np.cos(ang)], axis=-1)

    return np.concatenate([enc(row), enc(col)], axis=-1).astype(np.float32)


def _embed_kernel(n_ctx_tiles, xp_ref, xs_ref, pos_ref, o_ref):
    i = pl.program_id(0)

    @pl.when(i < n_ctx_tiles)
    def _():
        o_ref[...] = xp_ref[...]

    @pl.when(i >= n_ctx_tiles)
    def _():
        o_ref[...] = xs_ref[...] + pos_ref[...]


def _embed(x_prompt, x_sample):
    batch, seq, d = x_prompt.shape
    dec_batch, dec_seq, _ = x_sample.shape
    tm = ROW_TILE
    n_ctx_tiles, n_lat_tiles, per_seq = batch * seq // tm, dec_batch * dec_seq // tm, dec_seq // tm
    lat_tile = lambda i: jnp.maximum(i - n_ctx_tiles, 0)
    return pl.pallas_call(
        functools.partial(_embed_kernel, n_ctx_tiles),
        grid=(n_ctx_tiles + n_lat_tiles,),
        in_specs=[pl.BlockSpec((tm, d), lambda i: (jnp.minimum(i, n_ctx_tiles - 1), 0)),
                  pl.BlockSpec((tm, d), lambda i: (lat_tile(i), 0)),
                  pl.BlockSpec((tm, d), lambda i: (lat_tile(i) % per_seq, 0))],
        out_specs=pl.BlockSpec((tm, d), lambda i: (i, 0)),
        out_shape=jax.ShapeDtypeStruct((batch * seq + dec_batch * dec_seq, d), F32),
        compiler_params=_cparams("arbitrary"),
        name="embed",
    )(x_prompt.reshape(batch * seq, d), x_sample.reshape(dec_batch * dec_seq, d),
      jnp.asarray(_grid_pos_embed(dec_seq)))


def _block_diag2(m):
    z = jnp.zeros_like(m[0])
    return jnp.concatenate([jnp.concatenate([m[0], z], axis=1), jnp.concatenate([z, m[1]], axis=1)], axis=0)


def _interleave_heads(a, axis=-1):
    a = jnp.moveaxis(a, axis, -1)
    lead = a.shape[:-1]
    a = a.reshape(lead + (-1, N_RWKV_HEADS, RWKV_HEAD))
    a = jnp.swapaxes(a, -1, -2).reshape(lead + (-1,))
    return jnp.moveaxis(a, -1, axis)


def _pad_to(a, rows, cols):
    return jnp.pad(a, ((0, rows - a.shape[0]), (0, cols - a.shape[1])))


def kernel(x_prompt, x_sample, c, state_wkv, c_ctx, w_ada, b_ada, w_in, rwkv_mu, rwkv_w0, rwkv_w2, rwkv_a0, rwkv_a2, rwkv_g2, rwkv_k_k, rwkv_k_a, rwkv_r_k, rwkv_gn_w, rwkv_gn_b, hy_conv_w, hy_conv_b, hy_f1, hy_fb1, hy_f2, hy_fb2, hy_f3, hy_skip, w_pa, w_pb, w_o, ln_g, ln_b, ffn_w_in, ffn_w_out, router_w, router_b, exp_w_in, exp_w_out):
    batch, seq, d = x_prompt.shape
    dec_batch, dec_seq, _ = x_sample.shape
    depth = w_in.shape[0]
    alpha = (2 * depth) ** 0.25
    n_ctx, n_lat = batch * seq, dec_batch * dec_seq
    assert seq % ROW_TILE == 0 and dec_seq % ROW_TILE == 0 and n_ctx % dec_seq == 0
    assert n_ctx % 512 == 0 and n_lat % 512 == 0 and dec_seq % 512 == 0
    assert seq % RELAYOUT_TBLOCK == 0 and dec_seq % RELAYOUT_TBLOCK == 0
    vsplit_ctx, vsplit_lat = (1 if b % 16 == 0 else 2 for b in (batch, dec_batch))
    assert batch % (16 // vsplit_ctx) == 0 and dec_batch % (16 // vsplit_lat) == 0
    assert (n_ctx // dec_seq) % (16 // vsplit_lat) == 0
    tiles_ctx, tiles_lat, n_ctx_tiles = seq // ROW_TILE, dec_seq // ROW_TILE, n_ctx // ROW_TILE
    mod_map = _mod_row_map(n_ctx_tiles, tiles_lat, dec_batch)

    x = _embed(x_prompt, x_sample)
    cond_rows = -(-(dec_batch + 1) // SUBLANES) * SUBLANES
    cond = jnp.zeros((cond_rows, d), F32).at[:dec_batch].set(c).at[dec_batch].set(c_ctx)
    mods = _adaln(cond, w_ada, b_ada).reshape(depth, cond_rows, N_MOD, d)

    il = _interleave_heads
    s0_ctx = jnp.zeros((batch, 2, N_RWKV_HEADS, RWKV_HEAD, RWKV_HEAD), F32)
    ctx_states = []
    for l in range(depth):
        lp = {
            "mu": jnp.concatenate([il(rwkv_mu[l][:, :3 * D_RWKV]), rwkv_mu[l][:, 3 * D_RWKV:]], axis=1),
            "w0": il(rwkv_w0[l]), "w2cat": _block_diag2(il(rwkv_w2[l])), "a0": il(rwkv_a0[l]),
            "a2cat": _block_diag2(il(rwkv_a2[l])), "g2": il(rwkv_g2[l]), "k_k": il(rwkv_k_k[l])[None],
            "k_a": il(rwkv_k_a[l])[None], "r_k": il(rwkv_r_k[l].reshape(1, D_RWKV)),
            "gn_w": il(rwkv_gn_w[l])[None], "gn_b": il(rwkv_gn_b[l])[None],
            "w_pa": il(w_pa[l], axis=0).astype(BF16), "w_pb": w_pb[l].astype(BF16), "w_o": w_o[l].astype(BF16),
            "ln_g1": ln_g[l, 0][None], "ln_b1": ln_b[l, 0][None],
            "hy_conv_w": hy_conv_w[l], "hy_conv_b": hy_conv_b[l][None], "hy_skip": hy_skip[l],
            "hy_f1": _pad_to(hy_f1[l], PAD, PAD), "hy_fb1": _pad_to(hy_fb1[l][None], 1, PAD),
            "hy_f2": _pad_to(hy_f2[l], PAD, PAD), "hy_fb2": _pad_to(hy_fb2[l][None], 1, PAD),
            "hy_f3": _pad_to(hy_f3[l], PAD, 2 * HYENA_ORDER * D_HYENA),
        }
        if l % 2 == 0:
            fp = {"routed": False, "w_in": ffn_w_in[l // 2].astype(BF16), "w_out": ffn_w_out[l // 2].astype(BF16),
                  "router_w": jnp.zeros((d, PAD), F32), "router_b": jnp.zeros((1, PAD), F32)}
        else:
            fp = {"routed": True, "w_in": exp_w_in[l // 2].astype(BF16), "w_out": exp_w_out[l // 2].astype(BF16),
                  "router_w": _pad_to(router_w[l // 2], d, PAD), "router_b": _pad_to(router_b[l // 2][None], 1, PAD)}
        fp["ln_g"], fp["ln_b"] = ln_g[l, 1][None], ln_b[l, 1][None]
        mod = mods[l]

        w_in_l = jnp.concatenate([il(w_in[l][:, :3 * D_RWKV]), w_in[l][:, 3 * D_RWKV:]], axis=1).astype(BF16)
        scan_ops, bonus, g, p_h, gates = _inproj(x, mod, w_in_l, lp, mod_map, tiles_ctx, tiles_lat, n_ctx_tiles)
        ys, s_ctx = _wkv_group(scan_ops, s0_ctx, seq, 0, batch, vsplit_ctx, None)
        (y_f, y_b), _ = _wkv_group(scan_ops, state_wkv[:, l], dec_seq, n_ctx // dec_seq, dec_batch, vsplit_lat, ys)
        ctx_states.append(s_ctx)

        kf_ctx = _hyena_filters(seq, lp)
        kf_lat = kf_ctx if dec_seq == seq else _hyena_filters(dec_seq, lp)
        y_h = _hyena(p_h, kf_ctx, lp, batch, seq, 0, None)
        y_h = _hyena(p_h, kf_lat, lp, dec_batch, dec_seq, n_ctx // dec_seq, y_h)

        x = _tail(y_f, y_b, bonus, g, y_h, gates, x, mod, lp, mod_map, alpha)
        x = _ffn(x, mod, fp, mod_map, alpha, split_rows=n_ctx if l == depth - 1 else None)

    y_prompt, y_sample = x
    return (y_prompt.reshape(batch, seq, d), y_sample.reshape(dec_batch, dec_seq, d),
            jnp.stack(ctx_states, axis=1))
```

```python
import functools
import math

import jax
import jax.numpy as jnp
import ml_dtypes
import numpy as np
from jax.experimental import pallas as pl
from jax.experimental.pallas import tpu as pltpu

F32 = jnp.float32
BF16 = jnp.bfloat16

D_MODEL = 1024
GRID_W = 64
D_RWKV = 512
RWKV_HEAD = 64
N_RWKV_HEADS = D_RWKV // RWKV_HEAD
LORA_W = 64
LORA_A = 64
LORA_G = 128
D_RWKV_PROJ = 3 * D_RWKV + 2 * LORA_W + 2 * LORA_A + LORA_G
DECAY_SCALE = math.exp(-0.5)
GN_EPS = 64e-5
D_HYENA = 512
HYENA_ORDER = 2
HYENA_EMB = 33
HYENA_BANDS = (HYENA_EMB - 1) // 2
HYENA_HIDDEN = 64
HYENA_FAST_DECAY = 0.3
HYENA_SLOW_DECAY = 1.5
HYENA_TARGET = 1e-2
D_HYENA_PROJ = (HYENA_ORDER + 1) * D_HYENA
D_IN_PROJ = D_RWKV_PROJ + D_HYENA_PROJ + 2 * D_MODEL
D_FF = 2816
N_EXPERTS = 8
TOP_K = 2
D_FF_EXPERT = 1408
N_MOD = 6
LN_EPS = 1e-5
POS_BASE = 10000.0

LANES = 128
SUBLANES = 8
VMEM_LIMIT = 56 * 1024 * 1024

ROW_TILE = 256
HALO = 16
FFN_ROW_TILE = 512
ADA_COL_TILE = 1536
MXU_COLS = 256
HY_COL_TILE = 2 * MXU_COLS
SCAN_VBLOCKS = 8
SCAN_TBLOCK = 64
N_SCAN_SRC = 9
RELAYOUT_TBLOCK = LANES
PAD = LANES


def _cparams(*sem):
    return pltpu.CompilerParams(dimension_semantics=sem, vmem_limit_bytes=VMEM_LIMIT)


def _dot(a, b):
    return jnp.dot(a, b, preferred_element_type=F32)


def _split2(x):
    hi = x.astype(BF16)
    lo = (x - hi.astype(F32)).astype(BF16)
    return hi, lo


def _dot3(a, b):
    ah, al = _split2(a)
    bh, bl = _split2(b)
    return _dot(ah, bh) + _dot(al, bh) + _dot(ah, bl)


def _dot3_lhs_split(ah, al, b):
    bh, bl = _split2(b)
    return _dot(ah, bh) + _dot(al, bh) + _dot(ah, bl)


def _sigmoid(x):
    return 1.0 / (1.0 + jnp.exp(-x))


def _silu(x):
    return x * _sigmoid(x)


def _head_sum(x):
    s = x[:, 0:LANES]
    for c in range(1, D_RWKV // LANES):
        s = s + x[:, c * LANES:(c + 1) * LANES]
    shift = LANES // 2
    while shift >= N_RWKV_HEADS:
        s = s + pltpu.roll(s, shift, axis=1)
        shift //= 2
    return jnp.concatenate([s] * (D_RWKV // LANES), axis=1)


def _layer_norm(z, g, b):
    mean = jnp.mean(z, axis=-1, keepdims=True)
    d = z - mean
    var = jnp.mean(d * d, axis=-1, keepdims=True)
    return d * jax.lax.rsqrt(var + LN_EPS) * g + b


def _ada_kernel(c_ref, w_ref, b_ref, o_ref):
    o_ref[...] = _dot3(_silu(c_ref[...]), w_ref[...]) + b_ref[...]


def _adaln(cond, w_ada, b_ada):
    depth, d, n = w_ada.shape
    rows = cond.shape[0]
    tn = ADA_COL_TILE
    return pl.pallas_call(
        _ada_kernel,
        grid=(depth, n // tn),
        in_specs=[
            pl.BlockSpec((rows, d), lambda l, j: (0, 0)),
            pl.BlockSpec((None, d, tn), lambda l, j: (l, 0, j)),
            pl.BlockSpec((None, 1, tn), lambda l, j: (l, 0, j)),
        ],
        out_specs=pl.BlockSpec((None, rows, tn), lambda l, j: (l, 0, j)),
        out_shape=jax.ShapeDtypeStruct((depth, rows, n), F32),
        compiler_params=_cparams("arbitrary", "arbitrary"),
        name="adaln",
    )(cond, w_ada, b_ada.reshape(depth, 1, n))


def _mod_row_map(n_ctx_tiles, tiles_per_seq, ctx_row):
    def index_map(i, *_):
        return (jnp.where(i < n_ctx_tiles, ctx_row, (i - n_ctx_tiles) // tiles_per_seq), 0, 0)

    return index_map


def _inproj_kernel(tiles_ctx, tiles_lat, n_ctx_tiles,
                   x_ref, xprev_ref, xnext_ref, mod_ref, w_ref, mu_ref, w0_ref, w2_ref, a0_ref, a2_ref, g2_ref,
                   kk_w_ref, ka_ref, rk_ref,
                   ops_ref, bonus_ref, g_ref, ph_ref, gates_ref):
    i = pl.program_id(0)
    tm = x_ref.shape[0]
    j = jnp.where(i < n_ctx_tiles, i % tiles_ctx, (i - n_ctx_tiles) % tiles_lat)
    per_seq = jnp.where(i < n_ctx_tiles, tiles_ctx, tiles_lat)
    x_ext = jnp.concatenate([xprev_ref[...], x_ref[...], xnext_ref[...]], axis=0)
    h_ext = (x_ext * (1.0 + mod_ref[1:2, :]) + mod_ref[0:1, :]).astype(BF16)
    h = h_ext[HALO:HALO + tm]
    c0, c1 = D_RWKV_PROJ, D_RWKV_PROJ + D_HYENA_PROJ
    ph_ref[...] = _dot(h, w_ref[:, c0:c1])
    gates_ref[...] = _dot(h, w_ref[:, c1:D_IN_PROJ]).astype(BF16)
    p_ext = _dot(h_ext, w_ref[:, 0:c0])

    x = p_ext[HALO:HALO + tm]
    rowid = jax.lax.broadcasted_iota(jnp.int32, (tm, 1), 0)
    prev_row = jnp.where(j == 0, 0.0, p_ext[HALO - 1:HALO])
    next_row = jnp.where(j == per_seq - 1, 0.0, p_ext[HALO + tm:HALO + tm + 1])
    prev = jnp.where(rowid == 0, prev_row, pltpu.roll(x, 1, axis=0))
    nxt = jnp.where(rowid == tm - 1, next_row, pltpu.roll(x, tm - 1, axis=0))
    p = x + mu_ref[0:1, :] * (prev - x) + mu_ref[1:2, :] * (nxt - x)

    d = D_RWKV
    r, k, v = p[:, 0:d], p[:, d:2 * d], p[:, 2 * d:3 * d]
    low_w = p[:, 3 * d:3 * d + LANES]
    low_a = p[:, 3 * d + LANES:3 * d + 2 * LANES]
    low_g = p[:, 3 * d + 2 * LANES:3 * d + 3 * LANES]

    kk = k * kk_w_ref[...]
    kk = kk * jax.lax.rsqrt(jnp.maximum(_head_sum(kk * kk), 1e-24))
    lw = _dot3(jnp.tanh(low_w), w2_ref[...])
    la = _dot(low_a.astype(BF16), a2_ref[...].astype(BF16))
    ksum = jnp.zeros_like(k)
    for dirn in range(2):
        log_w = -DECAY_SCALE * _sigmoid(w0_ref[dirn:dirn + 1, :] + lw[:, dirn * d:(dirn + 1) * d])
        a = _sigmoid(a0_ref[dirn:dirn + 1, :] + la[:, dirn * d:(dirn + 1) * d])
        k_d = k * (1.0 + (a - 1.0) * ka_ref[...])
        ops_ref[3 + 3 * dirn] = jnp.exp(log_w).T
        ops_ref[4 + 3 * dirn] = k_d.T
        ops_ref[5 + 3 * dirn] = (kk * a).T
        ksum = ksum + k_d
    ops_ref[0] = r.T
    ops_ref[1] = v.T
    ops_ref[2] = kk.T
    bonus_ref[...] = _head_sum(r * (0.5 * ksum) * rk_ref[...]) * v
    g_ref[...] = _dot(_sigmoid(low_g).astype(BF16), g2_ref[...].astype(BF16))


def _inproj(x, mod, w_in_bf, lp, mod_map, tiles_ctx, tiles_lat, n_ctx_tiles):
    n = x.shape[0]
    tm = ROW_TILE
    halo = tm // HALO
    n_halo = n // HALO
    row = lambda width: pl.BlockSpec((tm, width), lambda i: (i, 0))
    full = lambda a: pl.BlockSpec(a.shape, lambda i: (0,) * a.ndim)
    params = [lp[k] for k in ("mu", "w0", "w2cat", "a0", "a2cat", "g2", "k_k", "k_a", "r_k")]
    out = lambda width: jax.ShapeDtypeStruct((n, width), F32)
    return pl.pallas_call(
        functools.partial(_inproj_kernel, tiles_ctx, tiles_lat, n_ctx_tiles),
        grid=(n // tm,),
        in_specs=[
            row(D_MODEL),
            pl.BlockSpec((HALO, D_MODEL), lambda i: (jnp.maximum(i * halo - 1, 0), 0)),
            pl.BlockSpec((HALO, D_MODEL), lambda i: (jnp.minimum((i + 1) * halo, n_halo - 1), 0)),
            pl.BlockSpec((None, N_MOD, D_MODEL), mod_map),
            pl.BlockSpec((D_MODEL, D_IN_PROJ), lambda i: (0, 0)),
        ] + [full(a) for a in params],
        out_specs=[pl.BlockSpec((N_SCAN_SRC, D_RWKV, tm), lambda i: (0, 0, i)),
                   row(D_RWKV), row(D_RWKV), row(D_HYENA_PROJ), row(2 * D_MODEL)],
        out_shape=[jax.ShapeDtypeStruct((N_SCAN_SRC, D_RWKV, n), F32), out(D_RWKV), out(D_RWKV),
                   out(D_HYENA_PROJ), jax.ShapeDtypeStruct((n, 2 * D_MODEL), BF16)],
        compiler_params=_cparams("arbitrary"),
        name="inproj",
    )(x, x, x, mod, w_in_bf, *params)


def _tail_kernel(alpha, yf_ref, yb_ref, bonus_ref, g_ref, yh_ref, gate_ref, x_ref, mod_ref,
                 gnw_ref, gnb_ref, wpa_ref, wpb_ref, wo_ref, lng_ref, lnb_ref, o_ref):
    y = yf_ref[...] + yb_ref[...]
    mean = _head_sum(y) * (1.0 / RWKV_HEAD)
    d = y - mean
    var = _head_sum(d * d) * (1.0 / RWKV_HEAD)
    y_n = d * jax.lax.rsqrt(var + GN_EPS) * gnw_ref[...] + gnb_ref[...]
    y_a = ((y_n + bonus_ref[...]) * g_ref[...]).astype(BF16)
    merged = (_sigmoid(gate_ref[:, 0:D_MODEL].astype(F32)) * _dot(y_a, wpa_ref[...])
              + _sigmoid(gate_ref[:, D_MODEL:2 * D_MODEL].astype(F32)) * _dot(yh_ref[...], wpb_ref[...]))
    m = _dot(merged.astype(BF16), wo_ref[...])
    z = alpha * x_ref[...] + mod_ref[2:3, :] * m
    o_ref[...] = _layer_norm(z, lng_ref[...], lnb_ref[...])


def _tail(ys, bonus, g, y_h, gates, x, mod, lp, mod_map, alpha):
    n = x.shape[0]
    tm = ROW_TILE
    row = lambda width: pl.BlockSpec((tm, width), lambda i: (i, 0))
    scan_dir = lambda d: pl.BlockSpec((None, tm, D_RWKV), lambda i: (d, i, 0))
    full = lambda a: pl.BlockSpec(a.shape, lambda i: (0,) * a.ndim)
    params = [lp[k] for k in ("gn_w", "gn_b", "w_pa", "w_pb", "w_o", "ln_g1", "ln_b1")]
    return pl.pallas_call(
        functools.partial(_tail_kernel, alpha),
        grid=(n // tm,),
        in_specs=[scan_dir(0), scan_dir(1)] + [row(D_RWKV)] * 3
        + [row(2 * D_MODEL), row(D_MODEL), pl.BlockSpec((None, N_MOD, D_MODEL), mod_map)]
        + [full(a) for a in params],
        out_specs=row(D_MODEL),
        out_shape=jax.ShapeDtypeStruct((n, D_MODEL), F32),
        compiler_params=_cparams("arbitrary"),
        name="mixer_tail",
    )(ys, ys, bonus, g, y_h, gates, x, mod, *params)


def _ffn_kernel(routed, alpha, split_tiles, x_ref, mod_ref, wgu_ref, wd_ref, rw_ref, rb_ref,
                lng_ref, lnb_ref, *rest):
    out_refs, (h_ref, acc_ref, comb_ref) = rest[:-3], rest[-3:]
    e = pl.program_id(1)
    lane = jax.lax.broadcasted_iota(jnp.int32, comb_ref.shape, 1)

    @pl.when(e == 0)
    def _():
        h = x_ref[...] * (1.0 + mod_ref[4:5, :]) + mod_ref[3:4, :]
        h_ref[...] = h.astype(BF16)
        acc_ref[...] = jnp.zeros_like(acc_ref)
        if routed:
            logits = _dot3(h, rw_ref[...]) + rb_ref[...]
            logits = jnp.where(lane < N_EXPERTS, logits, -jnp.inf)
            ex = jnp.exp(logits - jnp.max(logits, axis=-1, keepdims=True))
            probs = ex / jnp.sum(ex, axis=-1, keepdims=True)
            p1 = jnp.max(probs, axis=-1, keepdims=True)
            i1 = jnp.min(jnp.where(probs == p1, lane, PAD), axis=-1, keepdims=True)
            rest = jnp.where(lane == i1, -1.0, probs)
            p2 = jnp.max(rest, axis=-1, keepdims=True)
            i2 = jnp.min(jnp.where(rest == p2, lane, PAD), axis=-1, keepdims=True)
            total = p1 + p2
            comb_ref[...] = jnp.where(lane == i1, p1 / total, 0.0) + jnp.where(lane == i2, p2 / total, 0.0)

    gate_up = _dot(h_ref[...], wgu_ref[...])
    width = gate_up.shape[1] // 2
    act = _silu(gate_up[:, 0:width]) * gate_up[:, width:]
    if routed:
        act = act * jnp.sum(jnp.where(lane == e, comb_ref[...], 0.0), axis=-1, keepdims=True)
    acc_ref[...] += _dot(act.astype(BF16), wd_ref[...])

    def finish(o_ref):
        z = alpha * x_ref[...] + mod_ref[5:6, :] * acc_ref[...]
        o_ref[...] = _layer_norm(z, lng_ref[...], lnb_ref[...])

    last = e == pl.num_programs(1) - 1
    if split_tiles is None:
        pl.when(last)(lambda: finish(out_refs[0]))
    else:
        first_group = pl.program_id(0) < split_tiles
        pl.when(last & first_group)(lambda: finish(out_refs[0]))
        pl.when(last & jnp.logical_not(first_group))(lambda: finish(out_refs[1]))


def _ffn(x, mod, fp, mod_map, alpha, split_rows=None):
    n = x.shape[0]
    tm = FFN_ROW_TILE
    scale = tm // ROW_TILE
    routed = fp["routed"]
    if routed:
        groups = N_EXPERTS
        wgu_spec = pl.BlockSpec((None, D_MODEL, 2 * D_FF_EXPERT), lambda i, e: (e, 0, 0))
        wd_spec = pl.BlockSpec((None, D_FF_EXPERT, D_MODEL), lambda i, e: (e, 0, 0))
    else:
        groups = 1
        once = dict(pipeline_mode=pl.Buffered(1))
        wgu_spec = pl.BlockSpec((D_MODEL, 2 * D_FF), lambda i, e: (0, 0), **once)
        wd_spec = pl.BlockSpec((D_FF, D_MODEL), lambda i, e: (0, 0), **once)
    full = lambda a: pl.BlockSpec(a.shape, lambda i, e: (0,) * a.ndim)
    row = pl.BlockSpec((tm, D_MODEL), lambda i, e: (i, 0))
    mod_spec = pl.BlockSpec((None, N_MOD, D_MODEL), lambda i, e: mod_map(i * scale))
    if split_rows is None:
        split_tiles, out_specs, out_shape = None, row, jax.ShapeDtypeStruct((n, D_MODEL), F32)
    else:
        split_tiles = split_rows // tm
        out_specs = [pl.BlockSpec((tm, D_MODEL), lambda i, e: (jnp.minimum(i, split_tiles - 1), 0)),
                     pl.BlockSpec((tm, D_MODEL), lambda i, e: (jnp.maximum(i - split_tiles, 0), 0))]
        out_shape = [jax.ShapeDtypeStruct((split_rows, D_MODEL), F32),
                     jax.ShapeDtypeStruct((n - split_rows, D_MODEL), F32)]
    return pl.pallas_call(
        functools.partial(_ffn_kernel, routed, alpha, split_tiles),
        grid=(n // tm, groups),
        in_specs=[row, mod_spec, wgu_spec, wd_spec, full(fp["router_w"]), full(fp["router_b"]),
                  full(fp["ln_g"]), full(fp["ln_b"])],
        out_specs=out_specs,
        out_shape=out_shape,
        scratch_shapes=[pltpu.VMEM((tm, D_MODEL), BF16), pltpu.VMEM((tm, D_MODEL), F32),
                        pltpu.VMEM((tm, PAD), F32)],
        compiler_params=_cparams("arbitrary", "arbitrary"),
        name="moe_ffn" if routed else "dense_ffn",
    )(x, mod, fp["w_in"], fp["w_out"], fp["router_w"], fp["router_b"], fp["ln_g"], fp["ln_b"])


def _scan_kernel(r_ref, kk_ref, w_ref, k_ref, b_ref, v_ref, s0_ref, y_ref, sfin_ref, s_ref):
    tb = r_ref.shape[1]
    n_vblocks = s_ref.shape[0]
    tile = (SUBLANES, LANES)
    backward = pl.program_id(0) % 2 == 1

    @pl.when(pl.program_id(1) == 0)
    def _():
        s_ref[...] = s0_ref[...]

    def time_of(i):
        return jnp.where(backward, tb - 1 - i, i)

    def row(ref, k, t):
        return jnp.broadcast_to(ref[k, pl.ds(t, 1), :], tile)

    per_sweep = min(SCAN_VBLOCKS, n_vblocks)
    for part in range(n_vblocks // per_sweep):
        vbs = [part * per_sweep + j for j in range(per_sweep)]

        def step(i, sa, vbs=vbs):
            t = time_of(i)
            t_next = time_of(jnp.minimum(i + 1, tb - 1))
            v8 = [jnp.concatenate([v_ref[vb * SUBLANES + j, pl.ds(t, 1), :] for j in range(SUBLANES)], axis=0)
                  for vb in vbs]
            y = [jnp.zeros(tile, F32) for _ in vbs]
            sa_next = [jnp.zeros(tile, F32) for _ in vbs]
            for k in range(RWKV_HEAD):
                w, b, kd, r = row(w_ref, k, t), row(b_ref, k, t), row(k_ref, k, t), row(r_ref, k, t)
                kap = row(kk_ref, k, t_next)
                for j, vb in enumerate(vbs):
                    s = s_ref[vb, k] * w - sa[j] * b + v8[j] * kd
                    s_ref[vb, k] = s
                    y[j] = y[j] + s * r
                    sa_next[j] = sa_next[j] + s * kap
            for j, vb in enumerate(vbs):
                y_ref[t, pl.ds(vb * SUBLANES, SUBLANES), :] = y[j]
            return tuple(sa_next)

        t0 = time_of(0)
        sa0 = [jnp.zeros(tile, F32) for _ in vbs]
        for k in range(RWKV_HEAD):
            kap = row(kk_ref, k, t0)
            for j, vb in enumerate(vbs):
                sa0[j] = sa0[j] + s_ref[vb, k] * kap
        jax.lax.fori_loop(0, tb, step, tuple(sa0))

    @pl.when(pl.program_id(1) == pl.num_programs(1) - 1)
    def _():
        sfin_ref[...] = s_ref[...]


def _scan(xk, vk, s0):
    _, n_sg, hd, t_len, _ = xk.shape
    rows = vk.shape[1]
    tb = SCAN_TBLOCK
    n_t = t_len // tb
    t_of = lambda g, t: jnp.where(g % 2 == 1, n_t - 1 - t, t)
    shared = lambda o: pl.BlockSpec((None, None, hd, tb, LANES), lambda g, t: (o, g // 2, 0, t_of(g, t), 0))
    per_dir = lambda o: pl.BlockSpec((None, None, hd, tb, LANES),
                                     lambda g, t: (o + 3 * (g % 2), g // 2, 0, t_of(g, t), 0))
    st = pl.BlockSpec((None, rows // SUBLANES, hd, SUBLANES, LANES), lambda g, t: (g, 0, 0, 0, 0))
    return pl.pallas_call(
        _scan_kernel,
        grid=(2 * n_sg, n_t),
        in_specs=[shared(0), shared(1), per_dir(2), per_dir(3), per_dir(4),
                  pl.BlockSpec((None, rows, tb, LANES), lambda g, t: (g // 2, 0, t_of(g, t), 0)), st],
        out_specs=[pl.BlockSpec((None, tb, rows, LANES), lambda g, t: (g, t_of(g, t), 0, 0)), st],
        out_shape=[jax.ShapeDtypeStruct((2 * n_sg, t_len, rows, LANES), F32),
                   jax.ShapeDtypeStruct(s0.shape, F32)],
        scratch_shapes=[pltpu.VMEM(s0.shape[1:], F32)],
        compiler_params=_cparams("arbitrary", "arbitrary"),
        name="wkv_scan",
    )(xk, xk, xk, xk, xk, vk, s0)


def _to_chains_kernel(lane_parts, *refs):
    src_refs, o_ref = refs[:-1], refs[-1]

    def heads_of(c):
        rows = [src[pl.ds(pl.multiple_of(c * N_RWKV_HEADS, N_RWKV_HEADS), N_RWKV_HEADS), :] for src in src_refs]
        return jnp.concatenate(rows, axis=0)

    def per_channel(c, carry):
        parts = {off: heads_of(off + c) for off in set(lane_parts)}
        o_ref[c] = jnp.concatenate([parts[off] for off in lane_parts], axis=0).T
        return carry

    jax.lax.fori_loop(0, o_ref.shape[0], per_channel, 0, unroll=4)


def _to_chains(src, stream_map, n_streams, lane_parts, seq_len, n_seq_blk, first_blk, n_blk):
    tb = RELAYOUT_TBLOCK
    n_t = seq_len // tb
    channels = RWKV_HEAD // len(set(lane_parts))

    def seq_spec(j):
        return pl.BlockSpec((None, D_RWKV, tb),
                            lambda s, g, t: (stream_map(s), 0, ((first_blk + g) * n_seq_blk + j) * n_t + t))

    return pl.pallas_call(
        functools.partial(_to_chains_kernel, lane_parts),
        grid=(n_streams, n_blk, n_t),
        in_specs=[seq_spec(j) for j in range(n_seq_blk)],
        out_specs=pl.BlockSpec((None, None, channels, tb, LANES), lambda s, g, t: (s, g, 0, t, 0)),
        out_shape=jax.ShapeDtypeStruct((n_streams, n_blk, channels, seq_len, LANES), F32),
        compiler_params=_cparams("arbitrary", "arbitrary", "arbitrary"),
        name="to_chains",
    )(*([src] * n_seq_blk))


def _from_chains_kernel(n_vsplit, yf_ref, yb_ref, prev_ref, o_ref, z_ref):
    del prev_ref
    _, n_seq_blk, tb, _ = o_ref.shape
    rows = RWKV_HEAD // n_vsplit
    chains = LANES // n_vsplit
    for y_ref, dst in ((yf_ref, o_ref.at[0]), (yb_ref, o_ref.at[1])):
        def per_row(v, carry, y_ref=y_ref):
            z_ref[pl.ds(pl.multiple_of(v * LANES, LANES), LANES), :] = y_ref[pl.ds(v, tb, stride=rows), :].T
            return carry

        jax.lax.fori_loop(0, rows, per_row, 0, unroll=4)
        for s in range(n_seq_blk):
            pieces = [z_ref[pl.ds(v * LANES + part * chains + s * N_RWKV_HEADS, N_RWKV_HEADS), :]
                      for part in range(n_vsplit) for v in range(rows)]
            dst[s] = jnp.concatenate(pieces, axis=0).T


def _from_chains(y, n_vsplit, n_blocks, first_blk, prev):
    n_lg, seq_len, rows, _ = y.shape
    tb = RELAYOUT_TBLOCK
    n_seq_blk = LANES // n_vsplit // N_RWKV_HEADS
    y2 = y.reshape(n_lg, seq_len * rows, LANES)
    shape = (2, n_blocks, n_seq_blk, seq_len, D_RWKV)
    y_spec = lambda d: pl.BlockSpec((None, tb * rows, LANES), lambda g, t: (2 * g + d, t, 0))
    out = pl.pallas_call(
        functools.partial(_from_chains_kernel, n_vsplit),
        grid=(n_lg // 2, seq_len // tb),
        in_specs=[y_spec(0), y_spec(1), pl.BlockSpec(memory_space=pl.ANY)],
        out_specs=pl.BlockSpec((2, None, n_seq_blk, tb, D_RWKV), lambda g, t: (0, first_blk + g, 0, t, 0)),
        out_shape=jax.ShapeDtypeStruct(shape, F32),
        input_output_aliases={2: 0},
        scratch_shapes=[pltpu.VMEM((rows * LANES, tb), F32)],
        compiler_params=_cparams("arbitrary", "arbitrary"),
        name="from_chains",
    )(y2, y2, prev.reshape(shape))
    return out.reshape(prev.shape)


def _wkv_group(ops, s0, seq_len, first_seq, n_seq, n_vsplit, prev_y):
    _, d, n_tok = ops.shape
    n_seq_blk = LANES // n_vsplit // N_RWKV_HEADS
    n_sg = n_seq // n_seq_blk
    rows = RWKV_HEAD // n_vsplit
    first_blk = first_seq // n_seq_blk
    xk = _to_chains(ops, lambda s: s + jnp.where(s >= 1, 1, 0), N_SCAN_SRC - 1, (0,) * n_vsplit,
                    seq_len, n_seq_blk, first_blk, n_sg)
    vk = _to_chains(ops, lambda s: 1, 1, tuple(p * rows for p in range(n_vsplit)),
                    seq_len, n_seq_blk, first_blk, n_sg)[0]
    s0c = s0.reshape(n_sg, n_seq_blk, 2, N_RWKV_HEADS, n_vsplit, rows // SUBLANES, SUBLANES, RWKV_HEAD)
    s0c = jnp.transpose(s0c, (0, 2, 5, 7, 6, 4, 1, 3)).reshape(2 * n_sg, rows // SUBLANES, RWKV_HEAD, SUBLANES, LANES)
    y, s_fin = _scan(xk, vk, s0c)
    ys = _from_chains(y, n_vsplit, n_tok // seq_len // n_seq_blk, first_blk, prev_y)
    s_fin = s_fin.reshape(n_sg, 2, rows // SUBLANES, RWKV_HEAD, SUBLANES, n_vsplit, n_seq_blk, N_RWKV_HEADS)
    s_fin = jnp.transpose(s_fin, (0, 6, 1, 7, 5, 2, 4, 3)).reshape(n_seq, 2, N_RWKV_HEADS, RWKV_HEAD, RWKV_HEAD)
    return ys, s_fin


@functools.lru_cache(maxsize=None)
def _dft_constants(seq_len):
    n = 2 * seq_len
    idx = np.arange(seq_len)
    ang = (2.0 * np.pi / n) * ((idx[:, None] * idx[None, :]) % n)
    alt = np.where(idx % 2 == 0, 1.0, -1.0)
    f_re = np.cos(ang)
    f_im = -np.sin(ang)
    f_im[0, :] = alt
    fwd = np.concatenate([f_re, f_im], axis=0)
    c = np.full((seq_len,), 2.0)
    c[0] = 1.0
    g_re = np.cos(ang.T) * c[None, :] / n
    g_im = -2.0 * np.sin(ang.T) / n
    g_im[:, 0] = alt / n
    inv = np.concatenate([g_re, g_im], axis=1)

    def split(m):
        hi = m.astype(ml_dtypes.bfloat16)
        lo = (m - hi.astype(np.float64)).astype(ml_dtypes.bfloat16)
        return hi, lo

    return split(fwd) + split(inv)


@functools.lru_cache(maxsize=None)
def _filter_constants(seq_len):
    t = np.linspace(0.0, 1.0, seq_len)[:, None]
    f = np.linspace(1e-4, HYENA_BANDS - 1, HYENA_BANDS)
    ang = (2.0 * np.pi / seq_len) * np.arange(seq_len)[:, None] * f
    feats = np.zeros((seq_len, PAD), np.float32)
    feats[:, :HYENA_EMB] = np.concatenate([t, np.cos(ang), -np.sin(ang)], axis=-1)
    deltas = np.abs(np.linspace(math.log(HYENA_TARGET) / HYENA_SLOW_DECAY,
                                math.log(HYENA_TARGET) / HYENA_FAST_DECAY, D_HYENA))
    decay = np.exp(-t * deltas).astype(np.float32)
    return feats, np.tile(decay, (1, 2 * HYENA_ORDER))


def _filter_kernel(feats_ref, decay_ref, f1_ref, b1_ref, f2_ref, b2_ref, f3_ref, h_ref):
    hid = jnp.sin(_dot3(feats_ref[...], f1_ref[...]) + b1_ref[...])
    hid = jnp.sin(_dot3(hid, f2_ref[...]) + b2_ref[...])
    h_ref[...] = _dot3(hid, f3_ref[...]) * decay_ref[...]


def _spectrum_kernel(fh_ref, fl_ref, hf_ref, hb_ref, kf_ref):
    seq_len = hf_ref.shape[0]
    row = jax.lax.broadcasted_iota(jnp.int32, (seq_len, 1), 0)
    h_f = hf_ref[...]
    h_b = jnp.where(row == 0, 0.0, hb_ref[...])
    fh, fl = fh_ref[...], fl_ref[...]
    a = _dot3_lhs_split(fh, fl, h_f)
    b = _dot3_lhs_split(fh, fl, h_b)
    kf_ref[0:seq_len, :] = a[0:seq_len] + b[0:seq_len]
    kf_ref[seq_len:, :] = jnp.where(row == 0, a[seq_len:] + b[seq_len:], a[seq_len:] - b[seq_len:])


def _hyena_filters(seq_len, lp):
    feats, decay = _filter_constants(seq_len)
    fh, fl, _, _ = _dft_constants(seq_len)
    args = [jnp.asarray(feats), jnp.asarray(decay), lp["hy_f1"], lp["hy_fb1"], lp["hy_f2"], lp["hy_fb2"],
            lp["hy_f3"]]
    h = pl.pallas_call(
        _filter_kernel,
        out_shape=jax.ShapeDtypeStruct((seq_len, 2 * HYENA_ORDER * D_HYENA), F32),
        compiler_params=_cparams(),
        name="hyena_filter_mlp",
    )(*args)
    tc = HY_COL_TILE
    per = D_HYENA // tc
    const = pl.BlockSpec((2 * seq_len, seq_len), lambda n, c: (0, 0), pipeline_mode=pl.Buffered(1))
    return pl.pallas_call(
        _spectrum_kernel,
        grid=(HYENA_ORDER, per),
        in_specs=[const, const,
                  pl.BlockSpec((seq_len, tc), lambda n, c: (0, (2 * n) * per + c)),
                  pl.BlockSpec((seq_len, tc), lambda n, c: (0, (2 * n + 1) * per + c))],
        out_specs=pl.BlockSpec((2 * seq_len, tc), lambda n, c: (0, n * per + c)),
        out_shape=jax.ShapeDtypeStruct((2 * seq_len, HYENA_ORDER * D_HYENA), F32),
        compiler_params=_cparams("arbitrary", "arbitrary"),
        name="hyena_filter_spectrum",
    )(jnp.asarray(fh), jnp.asarray(fl), h, h)


def _hyena_kernel(pz_ref, pg1_ref, pg2_ref, cwz_ref, cwg1_ref, cwg2_ref, cbz_ref, cbg1_ref, cbg2_ref,
                  kf0_ref, kf1_ref, skip_ref, f_ref, g_ref, *rest):
    o_ref = rest[-1]
    seq_len = pz_ref.shape[0]
    row = jax.lax.broadcasted_iota(jnp.int32, (seq_len, 1), 0)

    def short_conv(p_ref, cw_ref, cb_ref, cols):
        x = p_ref[:, cols]
        prev = jnp.where(row == 0, 0.0, pltpu.roll(x, 1, axis=0))
        nxt = jnp.where(row == seq_len - 1, 0.0, pltpu.roll(x, seq_len - 1, axis=0))
        return cw_ref[0:1, cols] * prev + cw_ref[1:2, cols] * x + cw_ref[2:3, cols] * nxt + cb_ref[:, cols]

    for c0 in range(0, pz_ref.shape[1], HY_COL_TILE):
        cols = slice(c0, c0 + HY_COL_TILE)
        z = short_conv(pz_ref, cwz_ref, cbz_ref, cols)
        gates = (short_conv(pg1_ref, cwg1_ref, cbg1_ref, cols), short_conv(pg2_ref, cwg2_ref, cbg2_ref, cols))
        for n, kf_ref in enumerate((kf0_ref, kf1_ref)):
            zf = _dot(f_ref[...], z.astype(BF16))
            z_re, z_im = zf[0:seq_len], zf[seq_len:]
            k_re, k_im = kf_ref[0:seq_len, cols], kf_ref[seq_len:, cols]
            p_re = jnp.where(row == 0, z_re * k_re, z_re * k_re - z_im * k_im)
            p_im = jnp.where(row == 0, z_im * k_im, z_re * k_im + z_im * k_re)
            conv = (_dot(g_ref[:, 0:seq_len], p_re.astype(BF16)) + _dot(g_ref[:, seq_len:], p_im.astype(BF16)))
            z = gates[n] * (conv + skip_ref[n:n + 1, cols] * z)
        o_ref[:, cols] = z.astype(o_ref.dtype)


def _hyena(p_h, kf, lp, n_seq, seq_len, seq_offset, prev):
    tc = HY_COL_TILE
    per = D_HYENA // tc
    fh, _, gh, _ = [jnp.asarray(a) for a in _dft_constants(seq_len)]
    once = dict(pipeline_mode=pl.Buffered(1))
    seg = lambda s: pl.BlockSpec((seq_len, tc), lambda b, c: (seq_offset + b, s * per + c))
    par = lambda rows, s: pl.BlockSpec((rows, tc), lambda b, c: (0, s * per + c))
    fconst = pl.BlockSpec((2 * seq_len, seq_len), lambda b, c: (0, 0), **once)
    gconst = pl.BlockSpec((seq_len, 2 * seq_len), lambda b, c: (0, 0), **once)
    cw, cb = lp["hy_conv_w"], lp["hy_conv_b"]
    return pl.pallas_call(
        _hyena_kernel,
        grid=(n_seq, per),
        in_specs=[seg(0), seg(1), seg(2), par(3, 0), par(3, 1), par(3, 2), par(1, 0), par(1, 1), par(1, 2),
                  pl.BlockSpec((2 * seq_len, tc), lambda b, c: (0, c), **(once if per == 1 else {})),
                  pl.BlockSpec((2 * seq_len, tc), lambda b, c: (0, per + c), **(once if per == 1 else {})),
                  pl.BlockSpec((HYENA_ORDER, tc), lambda b, c: (0, c)),
                  fconst, gconst, pl.BlockSpec(memory_space=pl.ANY)],
        out_specs=pl.BlockSpec((seq_len, tc), lambda b, c: (seq_offset + b, c)),
        out_shape=jax.ShapeDtypeStruct((p_h.shape[0], D_HYENA), BF16),
        input_output_aliases={14: 0},
        compiler_params=_cparams("arbitrary", "arbitrary"),
        name="hyena_conv",
    )(p_h, p_h, p_h, cw, cw, cw, cb, cb, cb, kf, kf, lp["hy_skip"], fh, gh, prev)


@functools.lru_cache(maxsize=None)
def _grid_pos_embed(n_tokens):
    rows = n_tokens // GRID_W
    row = np.repeat(np.arange(rows, dtype=np.float64), GRID_W)
    col = np.tile(np.arange(GRID_W, dtype=np.float64), rows)
    quarter = D_MODEL // 4
    omega = 1.0 / (POS_BASE ** (np.arange(quarter, dtype=np.float64) / quarter))

    def enc(pos):
        ang = pos[:, None] * omega
        return np.concatenate([np.sin(ang), np.cos(ang)], axis=-1)

    return np.concatenate([enc(row), enc(col)], axis=-1).astype(np.float32)


def _embed_kernel(n_ctx_tiles, xp_ref, xs_ref, pos_ref, o_ref):
    i = pl.program_id(0)

    @pl.when(i < n_ctx_tiles)
    def _():
        o_ref[...] = xp_ref[...]

    @pl.when(i >= n_ctx_tiles)
    def _():
        o_ref[...] = xs_ref[...] + pos_ref[...]


def _embed(x_prompt, x_sample):
    batch, seq, d = x_prompt.shape
    dec_batch, dec_seq, _ = x_sample.shape
    tm = ROW_TILE
    n_ctx_tiles, n_lat_tiles, per_seq = batch * seq // tm, dec_batch * dec_seq // tm, dec_seq // tm
    lat_tile = lambda i: jnp.maximum(i - n_ctx_tiles, 0)
    return pl.pallas_call(
        functools.partial(_embed_kernel, n_ctx_tiles),
        grid=(n_ctx_tiles + n_lat_tiles,),
        in_specs=[pl.BlockSpec((tm, d), lambda i: (jnp.minimum(i, n_ctx_tiles - 1), 0)),
                  pl.BlockSpec((tm, d), lambda i: (lat_tile(i), 0)),
                  pl.BlockSpec((tm, d), lambda i: (lat_tile(i) % per_seq, 0))],
        out_specs=pl.BlockSpec((tm, d), lambda i: (i, 0)),
        out_shape=jax.ShapeDtypeStruct((batch * seq + dec_batch * dec_seq, d), F32),
        compiler_params=_cparams("arbitrary"),
        name="embed",
    )(x_prompt.reshape(batch * seq, d), x_sample.reshape(dec_batch * dec_seq, d),
      jnp.asarray(_grid_pos_embed(dec_seq)))


def _block_diag2(m):
    z = jnp.zeros_like(m[0])
    return jnp.concatenate([jnp.concatenate([m[0], z], axis=1), jnp.concatenate([z, m[1]], axis=1)], axis=0)


def _interleave_heads(a, axis=-1):
    a = jnp.moveaxis(a, axis, -1)
    lead = a.shape[:-1]
    a = a.reshape(lead + (-1, N_RWKV_HEADS, RWKV_HEAD))
    a = jnp.swapaxes(a, -1, -2).reshape(lead + (-1,))
    return jnp.moveaxis(a, -1, axis)


def _pad_to(a, rows, cols):
    return jnp.pad(a, ((0, rows - a.shape[0]), (0, cols - a.shape[1])))


def kernel(x_prompt, x_sample, c, state_wkv, c_ctx, w_ada, b_ada, w_in, rwkv_mu, rwkv_w0, rwkv_w2, rwkv_a0, rwkv_a2, rwkv_g2, rwkv_k_k, rwkv_k_a, rwkv_r_k, rwkv_gn_w, rwkv_gn_b, hy_conv_w, hy_conv_b, hy_f1, hy_fb1, hy_f2, hy_fb2, hy_f3, hy_skip, w_pa, w_pb, w_o, ln_g, ln_b, ffn_w_in, ffn_w_out, router_w, router_b, exp_w_in, exp_w_out):
    batch, seq, d = x_prompt.shape
    dec_batch, dec_seq, _ = x_sample.shape
    depth = w_in.shape[0]
    alpha = (2 * depth) ** 0.25
    n_ctx, n_lat = batch * seq, dec_batch * dec_seq
    assert seq % ROW_TILE == 0 and dec_seq % ROW_TILE == 0 and n_ctx % dec_seq == 0
    assert n_ctx % FFN_ROW_TILE == 0 and n_lat % FFN_ROW_TILE == 0 and dec_seq % FFN_ROW_TILE == 0
    assert seq % RELAYOUT_TBLOCK == 0 and dec_seq % RELAYOUT_TBLOCK == 0
    assert 2 * LORA_W == LANES and 2 * LORA_A == LANES and LORA_G == LANES
    vsplit_ctx, vsplit_lat = (1 if b % 16 == 0 else 2 for b in (batch, dec_batch))
    assert batch % (16 // vsplit_ctx) == 0 and dec_batch % (16 // vsplit_lat) == 0
    assert (n_ctx // dec_seq) % (16 // vsplit_lat) == 0
    tiles_ctx, tiles_lat, n_ctx_tiles = seq // ROW_TILE, dec_seq // ROW_TILE, n_ctx // ROW_TILE
    mod_map = _mod_row_map(n_ctx_tiles, tiles_lat, dec_batch)

    x = _embed(x_prompt, x_sample)
    cond_rows = -(-(dec_batch + 1) // SUBLANES) * SUBLANES
    cond = jnp.zeros((cond_rows, d), F32).at[:dec_batch].set(c).at[dec_batch].set(c_ctx)
    mods = _adaln(cond, w_ada, b_ada).reshape(depth, cond_rows, N_MOD, d)

    il = _interleave_heads
    s0_ctx = jnp.zeros((batch, 2, N_RWKV_HEADS, RWKV_HEAD, RWKV_HEAD), F32)
    ctx_states = []
    for l in range(depth):
        lp = {
            "mu": jnp.concatenate([il(rwkv_mu[l][:, :3 * D_RWKV]), rwkv_mu[l][:, 3 * D_RWKV:]], axis=1),
            "w0": il(rwkv_w0[l]), "w2cat": _block_diag2(il(rwkv_w2[l])), "a0": il(rwkv_a0[l]),
            "a2cat": _block_diag2(il(rwkv_a2[l])), "g2": il(rwkv_g2[l]), "k_k": il(rwkv_k_k[l])[None],
            "k_a": il(rwkv_k_a[l])[None], "r_k": il(rwkv_r_k[l].reshape(1, D_RWKV)),
            "gn_w": il(rwkv_gn_w[l])[None], "gn_b": il(rwkv_gn_b[l])[None],
            "w_pa": il(w_pa[l], axis=0).astype(BF16), "w_pb": w_pb[l].astype(BF16), "w_o": w_o[l].astype(BF16),
            "ln_g1": ln_g[l, 0][None], "ln_b1": ln_b[l, 0][None],
            "hy_conv_w": hy_conv_w[l], "hy_conv_b": hy_conv_b[l][None], "hy_skip": hy_skip[l],
            "hy_f1": _pad_to(hy_f1[l], PAD, PAD), "hy_fb1": _pad_to(hy_fb1[l][None], 1, PAD),
            "hy_f2": _pad_to(hy_f2[l], PAD, PAD), "hy_fb2": _pad_to(hy_fb2[l][None], 1, PAD),
            "hy_f3": _pad_to(hy_f3[l], PAD, 2 * HYENA_ORDER * D_HYENA),
        }
        if l % 2 == 0:
            fp = {"routed": False, "w_in": ffn_w_in[l // 2].astype(BF16), "w_out": ffn_w_out[l // 2].astype(BF16),
                  "router_w": jnp.zeros((d, PAD), F32), "router_b": jnp.zeros((1, PAD), F32)}
        else:
            fp = {"routed": True, "w_in": exp_w_in[l // 2].astype(BF16), "w_out": exp_w_out[l // 2].astype(BF16),
                  "router_w": _pad_to(router_w[l // 2], d, PAD), "router_b": _pad_to(router_b[l // 2][None], 1, PAD)}
        fp["ln_g"], fp["ln_b"] = ln_g[l, 1][None], ln_b[l, 1][None]
        mod = mods[l]

        w_in_l = jnp.concatenate([il(w_in[l][:, :3 * D_RWKV]), w_in[l][:, 3 * D_RWKV:]], axis=1).astype(BF16)
        scan_ops, bonus, g, p_h, gates = _inproj(x, mod, w_in_l, lp, mod_map, tiles_ctx, tiles_lat, n_ctx_tiles)
        ys = jnp.zeros((2, n_ctx + n_lat, D_RWKV), F32)
        ys, s_ctx = _wkv_group(scan_ops, s0_ctx, seq, 0, batch, vsplit_ctx, ys)
        ys, _ = _wkv_group(scan_ops, state_wkv[:, l], dec_seq, n_ctx // dec_seq, dec_batch, vsplit_lat, ys)
        ctx_states.append(s_ctx)

        kf_ctx = _hyena_filters(seq, lp)
        kf_lat = kf_ctx if dec_seq == seq else _hyena_filters(dec_seq, lp)
        y_h = _hyena(p_h, kf_ctx, lp, batch, seq, 0, jnp.zeros((n_ctx + n_lat, D_HYENA), BF16))
        y_h = _hyena(p_h, kf_lat, lp, dec_batch, dec_seq, n_ctx // dec_seq, y_h)

        x = _tail(ys, bonus, g, y_h, gates, x, mod, lp, mod_map, alpha)
        x = _ffn(x, mod, fp, mod_map, alpha, split_rows=n_ctx if l == depth - 1 else None)

    y_prompt, y_sample = x
    return (y_prompt.reshape(batch, seq, d), y_sample.reshape(dec_batch, dec_seq, d),
            jnp.stack(ctx_states, axis=1))
```

```python
import functools
import math

import jax
import jax.numpy as jnp
import ml_dtypes
import numpy as np
from jax.experimental import pallas as pl
from jax.experimental.pallas import tpu as pltpu

F32 = jnp.float32
BF16 = jnp.bfloat16

D_MODEL = 1024
GRID_W = 64
D_RWKV = 512
RWKV_HEAD = 64
N_RWKV_HEADS = D_RWKV // RWKV_HEAD
LORA_W = 64
LORA_A = 64
LORA_G = 128
D_RWKV_PROJ = 3 * D_RWKV + 2 * LORA_W + 2 * LORA_A + LORA_G
DECAY_SCALE = math.exp(-0.5)
GN_EPS = 64e-5
D_HYENA = 512
HYENA_ORDER = 2
HYENA_EMB = 33
HYENA_BANDS = (HYENA_EMB - 1) // 2
HYENA_HIDDEN = 64
HYENA_FAST_DECAY = 0.3
HYENA_SLOW_DECAY = 1.5
HYENA_TARGET = 1e-2
D_HYENA_PROJ = (HYENA_ORDER + 1) * D_HYENA
D_IN_PROJ = D_RWKV_PROJ + D_HYENA_PROJ + 2 * D_MODEL
D_FF = 2816
N_EXPERTS = 8
TOP_K = 2
D_FF_EXPERT = 1408
N_MOD = 6
LN_EPS = 1e-5
POS_BASE = 10000.0

LANES = 128
SUBLANES = 8
VMEM_LIMIT = 56 * 1024 * 1024

ROW_TILE = 256
HALO = 16
FFN_ROW_TILE = 512
ADA_COL_TILE = 1536
MXU_COLS = 256
HY_COL_TILE = 2 * MXU_COLS
SCAN_VBLOCKS = 8
SCAN_TBLOCK = 64
N_SCAN_SRC = 9
RELAYOUT_TBLOCK = LANES
PAD = LANES


def _cparams(*sem):
    return pltpu.CompilerParams(dimension_semantics=sem, vmem_limit_bytes=VMEM_LIMIT)


def _dot(a, b):
    return jnp.dot(a, b, preferred_element_type=F32)


def _split2(x):
    hi = x.astype(BF16)
    lo = (x - hi.astype(F32)).astype(BF16)
    return hi, lo


def _dot3(a, b):
    ah, al = _split2(a)
    bh, bl = _split2(b)
    return _dot(ah, bh) + _dot(al, bh) + _dot(ah, bl)


def _dot3_short_k(a, b):
    ah, al = _split2(a)
    bh, bl = _split2(b)
    return _dot(jnp.concatenate([ah, al], axis=1), jnp.concatenate([bh, bh], axis=0)) + _dot(ah, bl)


def _dot3_lhs_split(ah, al, b):
    bh, bl = _split2(b)
    return _dot(ah, bh) + _dot(al, bh) + _dot(ah, bl)


def _sigmoid(x):
    return 1.0 / (1.0 + jnp.exp(-x))


def _silu(x):
    return x * _sigmoid(x)


def _head_sum(x):
    s = x[:, 0:LANES]
    for c in range(1, D_RWKV // LANES):
        s = s + x[:, c * LANES:(c + 1) * LANES]
    shift = LANES // 2
    while shift >= N_RWKV_HEADS:
        s = s + pltpu.roll(s, shift, axis=1)
        shift //= 2
    return jnp.concatenate([s] * (D_RWKV // LANES), axis=1)


def _layer_norm(z, g, b):
    mean = jnp.mean(z, axis=-1, keepdims=True)
    d = z - mean
    var = jnp.mean(d * d, axis=-1, keepdims=True)
    return d * jax.lax.rsqrt(var + LN_EPS) * g + b


def _ada_kernel(c_ref, w_ref, b_ref, o_ref):
    o_ref[...] = _dot3(_silu(c_ref[...]), w_ref[...]) + b_ref[...]


def _adaln(cond, w_ada, b_ada):
    depth, d, n = w_ada.shape
    rows = cond.shape[0]
    tn = ADA_COL_TILE
    return pl.pallas_call(
        _ada_kernel,
        grid=(depth, n // tn),
        in_specs=[
            pl.BlockSpec((rows, d), lambda l, j: (0, 0)),
            pl.BlockSpec((None, d, tn), lambda l, j: (l, 0, j)),
            pl.BlockSpec((None, 1, tn), lambda l, j: (l, 0, j)),
        ],
        out_specs=pl.BlockSpec((None, rows, tn), lambda l, j: (l, 0, j)),
        out_shape=jax.ShapeDtypeStruct((depth, rows, n), F32),
        compiler_params=_cparams("arbitrary", "arbitrary"),
        name="adaln",
    )(cond, w_ada, b_ada.reshape(depth, 1, n))


def _mod_row_map(n_ctx_tiles, tiles_per_seq, ctx_row):
    def index_map(i, *_):
        return (jnp.where(i < n_ctx_tiles, ctx_row, (i - n_ctx_tiles) // tiles_per_seq), 0, 0)

    return index_map


def _inproj_kernel(tiles_ctx, tiles_lat, n_ctx_tiles,
                   x_ref, xprev_ref, xnext_ref, mod_ref, w_ref, mu_ref, w0_ref, w2_ref, a0_ref, a2_ref, g2_ref,
                   kk_w_ref, ka_ref, rk_ref,
                   ops_ref, bonus_ref, g_ref, ph_ref, gates_ref):
    i = pl.program_id(0)
    tm = x_ref.shape[0]
    j = jnp.where(i < n_ctx_tiles, i % tiles_ctx, (i - n_ctx_tiles) % tiles_lat)
    per_seq = jnp.where(i < n_ctx_tiles, tiles_ctx, tiles_lat)
    x_ext = jnp.concatenate([xprev_ref[...], x_ref[...], xnext_ref[...]], axis=0)
    h_ext = (x_ext * (1.0 + mod_ref[1:2, :]) + mod_ref[0:1, :]).astype(BF16)
    h = h_ext[HALO:HALO + tm]
    c0, c1 = D_RWKV_PROJ, D_RWKV_PROJ + D_HYENA_PROJ
    p_ext = _dot(h_ext, w_ref[:, 0:c0])
    ph_ref[...] = _dot(h, w_ref[:, c0:c1])
    gates_ref[...] = _dot(h, w_ref[:, c1:D_IN_PROJ]).astype(BF16)

    x = p_ext[HALO:HALO + tm]
    rowid = jax.lax.broadcasted_iota(jnp.int32, (tm, 1), 0)
    prev_row = jnp.where(j == 0, 0.0, p_ext[HALO - 1:HALO])
    next_row = jnp.where(j == per_seq - 1, 0.0, p_ext[HALO + tm:HALO + tm + 1])
    prev = jnp.where(rowid == 0, prev_row, pltpu.roll(x, 1, axis=0))
    nxt = jnp.where(rowid == tm - 1, next_row, pltpu.roll(x, tm - 1, axis=0))
    p = x + mu_ref[0:1, :] * (prev - x) + mu_ref[1:2, :] * (nxt - x)

    d = D_RWKV
    r, k, v = p[:, 0:d], p[:, d:2 * d], p[:, 2 * d:3 * d]
    low_w = p[:, 3 * d:3 * d + LANES]
    low_a = p[:, 3 * d + LANES:3 * d + 2 * LANES]
    low_g = p[:, 3 * d + 2 * LANES:3 * d + 3 * LANES]

    ops_ref[0] = r.T
    ops_ref[1] = v.T
    kk = k * kk_w_ref[...]
    kk = kk * jax.lax.rsqrt(jnp.maximum(_head_sum(kk * kk), 1e-24))
    ops_ref[2] = kk.T
    lw = _dot3_short_k(jnp.tanh(low_w), w2_ref[...])
    la = _dot(low_a.astype(BF16), a2_ref[...].astype(BF16))
    g_ref[...] = _dot(_sigmoid(low_g).astype(BF16), g2_ref[...].astype(BF16))
    ksum = jnp.zeros_like(k)
    for dirn in range(2):
        log_w = -DECAY_SCALE * _sigmoid(w0_ref[dirn:dirn + 1, :] + lw[:, dirn * d:(dirn + 1) * d])
        a = _sigmoid(a0_ref[dirn:dirn + 1, :] + la[:, dirn * d:(dirn + 1) * d])
        k_d = k * (1.0 + (a - 1.0) * ka_ref[...])
        ops_ref[3 + 3 * dirn] = jnp.exp(log_w).T
        ops_ref[4 + 3 * dirn] = k_d.T
        ops_ref[5 + 3 * dirn] = (kk * a).T
        ksum = ksum + k_d
    bonus_ref[...] = _head_sum(r * (0.5 * ksum) * rk_ref[...]) * v


def _inproj(x, mod, w_in_bf, lp, mod_map, tiles_ctx, tiles_lat, n_ctx_tiles):
    n = x.shape[0]
    tm = ROW_TILE
    halo = tm // HALO
    n_halo = n // HALO
    row = lambda width: pl.BlockSpec((tm, width), lambda i: (i, 0))
    full = lambda a: pl.BlockSpec(a.shape, lambda i: (0,) * a.ndim)
    params = [lp[k] for k in ("mu", "w0", "w2cat", "a0", "a2cat", "g2", "k_k", "k_a", "r_k")]
    out = lambda width: jax.ShapeDtypeStruct((n, width), F32)
    return pl.pallas_call(
        functools.partial(_inproj_kernel, tiles_ctx, tiles_lat, n_ctx_tiles),
        grid=(n // tm,),
        in_specs=[
            row(D_MODEL),
            pl.BlockSpec((HALO, D_MODEL), lambda i: (jnp.maximum(i * halo - 1, 0), 0)),
            pl.BlockSpec((HALO, D_MODEL), lambda i: (jnp.minimum((i + 1) * halo, n_halo - 1), 0)),
            pl.BlockSpec((None, N_MOD, D_MODEL), mod_map),
            pl.BlockSpec((D_MODEL, D_IN_PROJ), lambda i: (0, 0)),
        ] + [full(a) for a in params],
        out_specs=[pl.BlockSpec((N_SCAN_SRC, D_RWKV, tm), lambda i: (0, 0, i)),
                   row(D_RWKV), row(D_RWKV), row(D_HYENA_PROJ), row(2 * D_MODEL)],
        out_shape=[jax.ShapeDtypeStruct((N_SCAN_SRC, D_RWKV, n), F32), out(D_RWKV), out(D_RWKV),
                   out(D_HYENA_PROJ), jax.ShapeDtypeStruct((n, 2 * D_MODEL), BF16)],
        compiler_params=_cparams("arbitrary"),
        name="inproj",
    )(x, x, x, mod, w_in_bf, *params)


def _tail_kernel(alpha, yf_ref, yb_ref, bonus_ref, g_ref, yh_ref, gate_ref, x_ref, mod_ref,
                 gnw_ref, gnb_ref, wpa_ref, wpb_ref, wo_ref, lng_ref, lnb_ref, o_ref):
    y = yf_ref[...] + yb_ref[...]
    mean = _head_sum(y) * (1.0 / RWKV_HEAD)
    d = y - mean
    var = _head_sum(d * d) * (1.0 / RWKV_HEAD)
    y_n = d * jax.lax.rsqrt(var + GN_EPS) * gnw_ref[...] + gnb_ref[...]
    y_a = ((y_n + bonus_ref[...]) * g_ref[...]).astype(BF16)
    merged = (_sigmoid(gate_ref[:, 0:D_MODEL].astype(F32)) * _dot(y_a, wpa_ref[...])
              + _sigmoid(gate_ref[:, D_MODEL:2 * D_MODEL].astype(F32)) * _dot(yh_ref[...], wpb_ref[...]))
    m = _dot(merged.astype(BF16), wo_ref[...])
    z = alpha * x_ref[...] + mod_ref[2:3, :] * m
    o_ref[...] = _layer_norm(z, lng_ref[...], lnb_ref[...])


def _tail(ys, bonus, g, y_h, gates, x, mod, lp, mod_map, alpha):
    n = x.shape[0]
    tm = ROW_TILE
    row = lambda width: pl.BlockSpec((tm, width), lambda i: (i, 0))
    scan_dir = lambda d: pl.BlockSpec((None, tm, D_RWKV), lambda i: (d, i, 0))
    full = lambda a: pl.BlockSpec(a.shape, lambda i: (0,) * a.ndim)
    params = [lp[k] for k in ("gn_w", "gn_b", "w_pa", "w_pb", "w_o", "ln_g1", "ln_b1")]
    return pl.pallas_call(
        functools.partial(_tail_kernel, alpha),
        grid=(n // tm,),
        in_specs=[scan_dir(0), scan_dir(1)] + [row(D_RWKV)] * 3
        + [row(2 * D_MODEL), row(D_MODEL), pl.BlockSpec((None, N_MOD, D_MODEL), mod_map)]
        + [full(a) for a in params],
        out_specs=row(D_MODEL),
        out_shape=jax.ShapeDtypeStruct((n, D_MODEL), F32),
        compiler_params=_cparams("arbitrary"),
        name="mixer_tail",
    )(ys, ys, bonus, g, y_h, gates, x, mod, *params)


def _ffn_kernel(routed, alpha, split_tiles, x_ref, mod_ref, wgu_ref, wd_ref, rw_ref, rb_ref,
                lng_ref, lnb_ref, *rest):
    out_refs, (h_ref, acc_ref, comb_ref) = rest[:-3], rest[-3:]
    e = pl.program_id(1)
    lane = jax.lax.broadcasted_iota(jnp.int32, comb_ref.shape, 1)

    @pl.when(e == 0)
    def _():
        h = x_ref[...] * (1.0 + mod_ref[4:5, :]) + mod_ref[3:4, :]
        h_ref[...] = h.astype(BF16)
        acc_ref[...] = jnp.zeros_like(acc_ref)
        if routed:
            logits = _dot3(h, rw_ref[...]) + rb_ref[...]
            logits = jnp.where(lane < N_EXPERTS, logits, -jnp.inf)
            ex = jnp.exp(logits - jnp.max(logits, axis=-1, keepdims=True))
            probs = ex / jnp.sum(ex, axis=-1, keepdims=True)
            p1 = jnp.max(probs, axis=-1, keepdims=True)
            i1 = jnp.min(jnp.where(probs == p1, lane, PAD), axis=-1, keepdims=True)
            rest = jnp.where(lane == i1, -1.0, probs)
            p2 = jnp.max(rest, axis=-1, keepdims=True)
            i2 = jnp.min(jnp.where(rest == p2, lane, PAD), axis=-1, keepdims=True)
            total = p1 + p2
            comb_ref[...] = jnp.where(lane == i1, p1 / total, 0.0) + jnp.where(lane == i2, p2 / total, 0.0)

    gate_up = _dot(h_ref[...], wgu_ref[...])
    width = gate_up.shape[1] // 2
    act = _silu(gate_up[:, 0:width]) * gate_up[:, width:]
    if routed:
        act = act * jnp.sum(jnp.where(lane == e, comb_ref[...], 0.0), axis=-1, keepdims=True)
    acc_ref[...] += _dot(act.astype(BF16), wd_ref[...])

    def finish(o_ref):
        z = alpha * x_ref[...] + mod_ref[5:6, :] * acc_ref[...]
        o_ref[...] = _layer_norm(z, lng_ref[...], lnb_ref[...])

    last = e == pl.num_programs(1) - 1
    if split_tiles is None:
        pl.when(last)(lambda: finish(out_refs[0]))
    else:
        first_group = pl.program_id(0) < split_tiles
        pl.when(last & first_group)(lambda: finish(out_refs[0]))
        pl.when(last & jnp.logical_not(first_group))(lambda: finish(out_refs[1]))


def _ffn(x, mod, fp, mod_map, alpha, split_rows=None):
    n = x.shape[0]
    tm = FFN_ROW_TILE
    scale = tm // ROW_TILE
    routed = fp["routed"]
    if routed:
        groups = N_EXPERTS
        wgu_spec = pl.BlockSpec((None, D_MODEL, 2 * D_FF_EXPERT), lambda i, e: (e, 0, 0))
        wd_spec = pl.BlockSpec((None, D_FF_EXPERT, D_MODEL), lambda i, e: (e, 0, 0))
    else:
        groups = 1
        once = dict(pipeline_mode=pl.Buffered(1))
        wgu_spec = pl.BlockSpec((D_MODEL, 2 * D_FF), lambda i, e: (0, 0), **once)
        wd_spec = pl.BlockSpec((D_FF, D_MODEL), lambda i, e: (0, 0), **once)
    full = lambda a: pl.BlockSpec(a.shape, lambda i, e: (0,) * a.ndim)
    row = pl.BlockSpec((tm, D_MODEL), lambda i, e: (i, 0))
    mod_spec = pl.BlockSpec((None, N_MOD, D_MODEL), lambda i, e: mod_map(i * scale))
    if split_rows is None:
        split_tiles, out_specs, out_shape = None, row, jax.ShapeDtypeStruct((n, D_MODEL), F32)
    else:
        split_tiles = split_rows // tm
        out_specs = [pl.BlockSpec((tm, D_MODEL), lambda i, e: (jnp.minimum(i, split_tiles - 1), 0)),
                     pl.BlockSpec((tm, D_MODEL), lambda i, e: (jnp.maximum(i - split_tiles, 0), 0))]
        out_shape = [jax.ShapeDtypeStruct((split_rows, D_MODEL), F32),
                     jax.ShapeDtypeStruct((n - split_rows, D_MODEL), F32)]
    return pl.pallas_call(
        functools.partial(_ffn_kernel, routed, alpha, split_tiles),
        grid=(n // tm, groups),
        in_specs=[row, mod_spec, wgu_spec, wd_spec, full(fp["router_w"]), full(fp["router_b"]),
                  full(fp["ln_g"]), full(fp["ln_b"])],
        out_specs=out_specs,
        out_shape=out_shape,
        scratch_shapes=[pltpu.VMEM((tm, D_MODEL), BF16), pltpu.VMEM((tm, D_MODEL), F32),
                        pltpu.VMEM((tm, PAD), F32)],
        compiler_params=_cparams("arbitrary", "arbitrary"),
        name="moe_ffn" if routed else "dense_ffn",
    )(x, mod, fp["w_in"], fp["w_out"], fp["router_w"], fp["router_b"], fp["ln_g"], fp["ln_b"])


def _scan_kernel(r_ref, kk_ref, w_ref, k_ref, b_ref, v_ref, s0_ref, y_ref, sfin_ref, s_ref):
    tb = r_ref.shape[1]
    n_vblocks = s_ref.shape[0]
    tile = (SUBLANES, LANES)
    backward = pl.program_id(0) % 2 == 1

    @pl.when(pl.program_id(1) == 0)
    def _():
        s_ref[...] = s0_ref[...]

    def time_of(i):
        return jnp.where(backward, tb - 1 - i, i)

    def row(ref, k, t):
        return jnp.broadcast_to(ref[k, pl.ds(t, 1), :], tile)

    per_sweep = min(SCAN_VBLOCKS, n_vblocks)
    for part in range(n_vblocks // per_sweep):
        vbs = [part * per_sweep + j for j in range(per_sweep)]

        def step(i, sa, vbs=vbs):
            t = time_of(i)
            t_next = time_of(jnp.minimum(i + 1, tb - 1))
            v8 = [jnp.concatenate([v_ref[vb * SUBLANES + j, pl.ds(t, 1), :] for j in range(SUBLANES)], axis=0)
                  for vb in vbs]
            y = [jnp.zeros(tile, F32) for _ in vbs]
            sa_next = [jnp.zeros(tile, F32) for _ in vbs]
            for k in range(RWKV_HEAD):
                w, b, kd, r = row(w_ref, k, t), row(b_ref, k, t), row(k_ref, k, t), row(r_ref, k, t)
                kap = row(kk_ref, k, t_next)
                for j, vb in enumerate(vbs):
                    s = s_ref[vb, k] * w - sa[j] * b + v8[j] * kd
                    s_ref[vb, k] = s
                    y[j] = y[j] + s * r
                    sa_next[j] = sa_next[j] + s * kap
            for j, vb in enumerate(vbs):
                y_ref[t, pl.ds(vb * SUBLANES, SUBLANES), :] = y[j]
            return tuple(sa_next)

        t0 = time_of(0)
        sa0 = [jnp.zeros(tile, F32) for _ in vbs]
        for k in range(RWKV_HEAD):
            kap = row(kk_ref, k, t0)
            for j, vb in enumerate(vbs):
                sa0[j] = sa0[j] + s_ref[vb, k] * kap
        jax.lax.fori_loop(0, tb, step, tuple(sa0))

    @pl.when(pl.program_id(1) == pl.num_programs(1) - 1)
    def _():
        sfin_ref[...] = s_ref[...]


def _scan(xk, vk, s0):
    _, n_sg, hd, t_len, _ = xk.shape
    rows = vk.shape[1]
    tb = SCAN_TBLOCK
    n_t = t_len // tb
    t_of = lambda g, t: jnp.where(g % 2 == 1, n_t - 1 - t, t)
    shared = lambda o: pl.BlockSpec((None, None, hd, tb, LANES), lambda g, t: (o, g // 2, 0, t_of(g, t), 0))
    per_dir = lambda o: pl.BlockSpec((None, None, hd, tb, LANES),
                                     lambda g, t: (o + 3 * (g % 2), g // 2, 0, t_of(g, t), 0))
    st = pl.BlockSpec((None, rows // SUBLANES, hd, SUBLANES, LANES), lambda g, t: (g, 0, 0, 0, 0))
    return pl.pallas_call(
        _scan_kernel,
        grid=(2 * n_sg, n_t),
        in_specs=[shared(0), shared(1), per_dir(2), per_dir(3), per_dir(4),
                  pl.BlockSpec((None, rows, tb, LANES), lambda g, t: (g // 2, 0, t_of(g, t), 0)), st],
        out_specs=[pl.BlockSpec((None, tb, rows, LANES), lambda g, t: (g, t_of(g, t), 0, 0)), st],
        out_shape=[jax.ShapeDtypeStruct((2 * n_sg, t_len, rows, LANES), F32),
                   jax.ShapeDtypeStruct(s0.shape, F32)],
        scratch_shapes=[pltpu.VMEM(s0.shape[1:], F32)],
        compiler_params=_cparams("arbitrary", "arbitrary"),
        name="wkv_scan",
    )(xk, xk, xk, xk, xk, vk, s0)


def _to_chains_kernel(lane_parts, *refs):
    src_refs, o_ref = refs[:-1], refs[-1]

    def heads_of(c):
        rows = [src[pl.ds(pl.multiple_of(c * N_RWKV_HEADS, N_RWKV_HEADS), N_RWKV_HEADS), :] for src in src_refs]
        return jnp.concatenate(rows, axis=0)

    def per_channel(c, carry):
        parts = {off: heads_of(off + c) for off in set(lane_parts)}
        o_ref[c] = jnp.concatenate([parts[off] for off in lane_parts], axis=0).T
        return carry

    jax.lax.fori_loop(0, o_ref.shape[0], per_channel, 0, unroll=4)


def _to_chains(src, stream_map, n_streams, lane_parts, seq_len, n_seq_blk, first_blk, n_blk):
    tb = RELAYOUT_TBLOCK
    n_t = seq_len // tb
    channels = RWKV_HEAD // len(set(lane_parts))

    def seq_spec(j):
        return pl.BlockSpec((None, D_RWKV, tb),
                            lambda s, g, t: (stream_map(s), 0, ((first_blk + g) * n_seq_blk + j) * n_t + t))

    return pl.pallas_call(
        functools.partial(_to_chains_kernel, lane_parts),
        grid=(n_streams, n_blk, n_t),
        in_specs=[seq_spec(j) for j in range(n_seq_blk)],
        out_specs=pl.BlockSpec((None, None, channels, tb, LANES), lambda s, g, t: (s, g, 0, t, 0)),
        out_shape=jax.ShapeDtypeStruct((n_streams, n_blk, channels, seq_len, LANES), F32),
        compiler_params=_cparams("arbitrary", "arbitrary", "arbitrary"),
        name="to_chains",
    )(*([src] * n_seq_blk))


def _from_chains_kernel(n_vsplit, yf_ref, yb_ref, prev_ref, o_ref, z_ref):
    del prev_ref
    _, n_seq_blk, tb, _ = o_ref.shape
    rows = RWKV_HEAD // n_vsplit
    chains = LANES // n_vsplit
    for y_ref, dst in ((yf_ref, o_ref.at[0]), (yb_ref, o_ref.at[1])):
        def per_row(v, carry, y_ref=y_ref):
            z_ref[pl.ds(pl.multiple_of(v * LANES, LANES), LANES), :] = y_ref[pl.ds(v, tb, stride=rows), :].T
            return carry

        jax.lax.fori_loop(0, rows, per_row, 0, unroll=4)
        for s in range(n_seq_blk):
            pieces = [z_ref[pl.ds(v * LANES + part * chains + s * N_RWKV_HEADS, N_RWKV_HEADS), :]
                      for part in range(n_vsplit) for v in range(rows)]
            dst[s] = jnp.concatenate(pieces, axis=0).T


def _from_chains(y, n_vsplit, n_blocks, first_blk, prev):
    n_lg, seq_len, rows, _ = y.shape
    tb = RELAYOUT_TBLOCK
    n_seq_blk = LANES // n_vsplit // N_RWKV_HEADS
    y2 = y.reshape(n_lg, seq_len * rows, LANES)
    shape = (2, n_blocks, n_seq_blk, seq_len, D_RWKV)
    y_spec = lambda d: pl.BlockSpec((None, tb * rows, LANES), lambda g, t: (2 * g + d, t, 0))
    out = pl.pallas_call(
        functools.partial(_from_chains_kernel, n_vsplit),
        grid=(n_lg // 2, seq_len // tb),
        in_specs=[y_spec(0), y_spec(1), pl.BlockSpec(memory_space=pl.ANY)],
        out_specs=pl.BlockSpec((2, None, n_seq_blk, tb, D_RWKV), lambda g, t: (0, first_blk + g, 0, t, 0)),
        out_shape=jax.ShapeDtypeStruct(shape, F32),
        input_output_aliases={2: 0},
        scratch_shapes=[pltpu.VMEM((rows * LANES, tb), F32)],
        compiler_params=_cparams("arbitrary", "arbitrary"),
        name="from_chains",
    )(y2, y2, prev.reshape(shape))
    return out.reshape(prev.shape)


def _wkv_group(ops, s0, seq_len, first_seq, n_seq, n_vsplit, prev_y):
    _, d, n_tok = ops.shape
    n_seq_blk = LANES // n_vsplit // N_RWKV_HEADS
    n_sg = n_seq // n_seq_blk
    rows = RWKV_HEAD // n_vsplit
    first_blk = first_seq // n_seq_blk
    xk = _to_chains(ops, lambda s: s + jnp.where(s >= 1, 1, 0), N_SCAN_SRC - 1, (0,) * n_vsplit,
                    seq_len, n_seq_blk, first_blk, n_sg)
    vk = _to_chains(ops, lambda s: 1, 1, tuple(p * rows for p in range(n_vsplit)),
                    seq_len, n_seq_blk, first_blk, n_sg)[0]
    s0c = s0.reshape(n_sg, n_seq_blk, 2, N_RWKV_HEADS, n_vsplit, rows // SUBLANES, SUBLANES, RWKV_HEAD)
    s0c = jnp.transpose(s0c, (0, 2, 5, 7, 6, 4, 1, 3)).reshape(2 * n_sg, rows // SUBLANES, RWKV_HEAD, SUBLANES, LANES)
    y, s_fin = _scan(xk, vk, s0c)
    ys = _from_chains(y, n_vsplit, n_tok // seq_len // n_seq_blk, first_blk, prev_y)
    s_fin = s_fin.reshape(n_sg, 2, rows // SUBLANES, RWKV_HEAD, SUBLANES, n_vsplit, n_seq_blk, N_RWKV_HEADS)
    s_fin = jnp.transpose(s_fin, (0, 6, 1, 7, 5, 2, 4, 3)).reshape(n_seq, 2, N_RWKV_HEADS, RWKV_HEAD, RWKV_HEAD)
    return ys, s_fin


@functools.lru_cache(maxsize=None)
def _dft_constants(seq_len):
    n = 2 * seq_len
    idx = np.arange(seq_len)
    ang = (2.0 * np.pi / n) * ((idx[:, None] * idx[None, :]) % n)
    alt = np.where(idx % 2 == 0, 1.0, -1.0)
    f_re = np.cos(ang)
    f_im = -np.sin(ang)
    f_im[0, :] = alt
    fwd = np.concatenate([f_re, f_im], axis=0)
    c = np.full((seq_len,), 2.0)
    c[0] = 1.0
    g_re = np.cos(ang.T) * c[None, :] / n
    g_im = -2.0 * np.sin(ang.T) / n
    g_im[:, 0] = alt / n
    inv = np.concatenate([g_re, g_im], axis=1)

    def split(m):
        hi = m.astype(ml_dtypes.bfloat16)
        lo = (m - hi.astype(np.float64)).astype(ml_dtypes.bfloat16)
        return hi, lo

    return split(fwd) + split(inv)


@functools.lru_cache(maxsize=None)
def _filter_constants(seq_len):
    t = np.linspace(0.0, 1.0, seq_len)[:, None]
    f = np.linspace(1e-4, HYENA_BANDS - 1, HYENA_BANDS)
    ang = (2.0 * np.pi / seq_len) * np.arange(seq_len)[:, None] * f
    feats = np.zeros((seq_len, PAD), np.float32)
    feats[:, :HYENA_EMB] = np.concatenate([t, np.cos(ang), -np.sin(ang)], axis=-1)
    deltas = np.abs(np.linspace(math.log(HYENA_TARGET) / HYENA_SLOW_DECAY,
                                math.log(HYENA_TARGET) / HYENA_FAST_DECAY, D_HYENA))
    decay = np.exp(-t * deltas).astype(np.float32)
    return feats, np.tile(decay, (1, 2 * HYENA_ORDER))


def _filter_kernel(feats_ref, decay_ref, f1_ref, b1_ref, f2_ref, b2_ref, f3_ref, h_ref):
    hid = jnp.sin(_dot3(feats_ref[...], f1_ref[...]) + b1_ref[...])
    hid = jnp.sin(_dot3(hid, f2_ref[...]) + b2_ref[...])
    h_ref[...] = _dot3(hid, f3_ref[...]) * decay_ref[...]


def _spectrum_kernel(fh_ref, fl_ref, hf_ref, hb_ref, kf_ref):
    seq_len = hf_ref.shape[0]
    row = jax.lax.broadcasted_iota(jnp.int32, (seq_len, 1), 0)
    h_f = hf_ref[...]
    h_b = jnp.where(row == 0, 0.0, hb_ref[...])
    kf_ref[0:seq_len, :] = _dot3_lhs_split(fh_ref[0:seq_len, :], fl_ref[0:seq_len, :], h_f + h_b)
    im = _dot3_lhs_split(fh_ref[seq_len:, :], fl_ref[seq_len:, :], h_f - h_b)
    alternating = jnp.where(row % 2 == 0, 1.0, -1.0)
    nyquist_fix = 2.0 * jnp.sum(alternating * h_b, axis=0, keepdims=True)
    kf_ref[seq_len:, :] = jnp.where(row == 0, im + nyquist_fix, im)


def _hyena_filters(seq_len, lp):
    feats, decay = _filter_constants(seq_len)
    fh, fl, _, _ = _dft_constants(seq_len)
    args = [jnp.asarray(feats), jnp.asarray(decay), lp["hy_f1"], lp["hy_fb1"], lp["hy_f2"], lp["hy_fb2"],
            lp["hy_f3"]]
    h = pl.pallas_call(
        _filter_kernel,
        out_shape=jax.ShapeDtypeStruct((seq_len, 2 * HYENA_ORDER * D_HYENA), F32),
        compiler_params=_cparams(),
        name="hyena_filter_mlp",
    )(*args)
    tc = HY_COL_TILE
    per = D_HYENA // tc
    const = pl.BlockSpec((2 * seq_len, seq_len), lambda n, c: (0, 0), pipeline_mode=pl.Buffered(1))
    return pl.pallas_call(
        _spectrum_kernel,
        grid=(HYENA_ORDER, per),
        in_specs=[const, const,
                  pl.BlockSpec((seq_len, tc), lambda n, c: (0, (2 * n) * per + c)),
                  pl.BlockSpec((seq_len, tc), lambda n, c: (0, (2 * n + 1) * per + c))],
        out_specs=pl.BlockSpec((2 * seq_len, tc), lambda n, c: (0, n * per + c)),
        out_shape=jax.ShapeDtypeStruct((2 * seq_len, HYENA_ORDER * D_HYENA), F32),
        compiler_params=_cparams("arbitrary", "arbitrary"),
        name="hyena_filter_spectrum",
    )(jnp.asarray(fh), jnp.asarray(fl), h, h)


def _hyena_kernel(pz_ref, pg1_ref, pg2_ref, cwz_ref, cwg1_ref, cwg2_ref, cbz_ref, cbg1_ref, cbg2_ref,
                  kf0_ref, kf1_ref, skip_ref, f_ref, g_ref, *rest):
    o_ref = rest[-1]
    seq_len = pz_ref.shape[0]
    row = jax.lax.broadcasted_iota(jnp.int32, (seq_len, 1), 0)

    def short_conv(p_ref, cw_ref, cb_ref, cols):
        x = p_ref[:, cols]
        prev = jnp.where(row == 0, 0.0, pltpu.roll(x, 1, axis=0))
        nxt = jnp.where(row == seq_len - 1, 0.0, pltpu.roll(x, seq_len - 1, axis=0))
        return cw_ref[0:1, cols] * prev + cw_ref[1:2, cols] * x + cw_ref[2:3, cols] * nxt + cb_ref[:, cols]

    for c0 in range(0, pz_ref.shape[1], HY_COL_TILE):
        cols = slice(c0, c0 + HY_COL_TILE)
        z = short_conv(pz_ref, cwz_ref, cbz_ref, cols)
        gates = (short_conv(pg1_ref, cwg1_ref, cbg1_ref, cols), short_conv(pg2_ref, cwg2_ref, cbg2_ref, cols))
        for n, kf_ref in enumerate((kf0_ref, kf1_ref)):
            zf = _dot(f_ref[...], z.astype(BF16))
            z_re, z_im = zf[0:seq_len], zf[seq_len:]
            k_re, k_im = kf_ref[0:seq_len, cols], kf_ref[seq_len:, cols]
            p_re = jnp.where(row == 0, z_re * k_re, z_re * k_re - z_im * k_im)
            p_im = jnp.where(row == 0, z_im * k_im, z_re * k_im + z_im * k_re)
            conv = (_dot(g_ref[:, 0:seq_len], p_re.astype(BF16)) + _dot(g_ref[:, seq_len:], p_im.astype(BF16)))
            z = gates[n] * (conv + skip_ref[n:n + 1, cols] * z)
        o_ref[:, cols] = z.astype(o_ref.dtype)


def _hyena(p_h, kf, lp, n_seq, seq_len, seq_offset, prev):
    tc = HY_COL_TILE
    per = D_HYENA // tc
    fh, _, gh, _ = [jnp.asarray(a) for a in _dft_constants(seq_len)]
    once = dict(pipeline_mode=pl.Buffered(1))
    seg = lambda s: pl.BlockSpec((seq_len, tc), lambda b, c: (seq_offset + b, s * per + c))
    par = lambda rows, s: pl.BlockSpec((rows, tc), lambda b, c: (0, s * per + c))
    fconst = pl.BlockSpec((2 * seq_len, seq_len), lambda b, c: (0, 0), **once)
    gconst = pl.BlockSpec((seq_len, 2 * seq_len), lambda b, c: (0, 0), **once)
    cw, cb = lp["hy_conv_w"], lp["hy_conv_b"]
    return pl.pallas_call(
        _hyena_kernel,
        grid=(n_seq, per),
        in_specs=[seg(0), seg(1), seg(2), par(3, 0), par(3, 1), par(3, 2), par(1, 0), par(1, 1), par(1, 2),
                  pl.BlockSpec((2 * seq_len, tc), lambda b, c: (0, c), **(once if per == 1 else {})),
                  pl.BlockSpec((2 * seq_len, tc), lambda b, c: (0, per + c), **(once if per == 1 else {})),
                  pl.BlockSpec((HYENA_ORDER, tc), lambda b, c: (0, c)),
                  fconst, gconst, pl.BlockSpec(memory_space=pl.ANY)],
        out_specs=pl.BlockSpec((seq_len, tc), lambda b, c: (seq_offset + b, c)),
        out_shape=jax.ShapeDtypeStruct((p_h.shape[0], D_HYENA), BF16),
        input_output_aliases={14: 0},
        compiler_params=_cparams("arbitrary", "arbitrary"),
        name="hyena_conv",
    )(p_h, p_h, p_h, cw, cw, cw, cb, cb, cb, kf, kf, lp["hy_skip"], fh, gh, prev)


@functools.lru_cache(maxsize=None)
def _grid_pos_embed(n_tokens):
    rows = n_tokens // GRID_W
    row = np.repeat(np.arange(rows, dtype=np.float64), GRID_W)
    col = np.tile(np.arange(GRID_W, dtype=np.float64), rows)
    quarter = D_MODEL // 4
    omega = 1.0 / (POS_BASE ** (np.arange(quarter, dtype=np.float64) / quarter))

    def enc(pos):
        ang = pos[:, None] * omega
        return np.concatenate([np.sin(ang), np.cos(ang)], axis=-1)

    return np.concatenate([enc(row), enc(col)], axis=-1).astype(np.float32)


def _embed_kernel(n_ctx_tiles, xp_ref, xs_ref, pos_ref, o_ref):
    i = pl.program_id(0)

    @pl.when(i < n_ctx_tiles)
    def _():
        o_ref[...] = xp_ref[...]

    @pl.when(i >= n_ctx_tiles)
    def _():
        o_ref[...] = xs_ref[...] + pos_ref[...]


def _embed(x_prompt, x_sample):
    batch, seq, d = x_prompt.shape
    dec_batch, dec_seq, _ = x_sample.shape
    tm = ROW_TILE
    n_ctx_tiles, n_lat_tiles, per_seq = batch * seq // tm, dec_batch * dec_seq // tm, dec_seq // tm
    lat_tile = lambda i: jnp.maximum(i - n_ctx_tiles, 0)
    return pl.pallas_call(
        functools.partial(_embed_kernel, n_ctx_tiles),
        grid=(n_ctx_tiles + n_lat_tiles,),
        in_specs=[pl.BlockSpec((tm, d), lambda i: (jnp.minimum(i, n_ctx_tiles - 1), 0)),
                  pl.BlockSpec((tm, d), lambda i: (lat_tile(i), 0)),
                  pl.BlockSpec((tm, d), lambda i: (lat_tile(i) % per_seq, 0))],
        out_specs=pl.BlockSpec((tm, d), lambda i: (i, 0)),
        out_shape=jax.ShapeDtypeStruct((batch * seq + dec_batch * dec_seq, d), F32),
        compiler_params=_cparams("arbitrary"),
        name="embed",
    )(x_prompt.reshape(batch * seq, d), x_sample.reshape(dec_batch * dec_seq, d),
      jnp.asarray(_grid_pos_embed(dec_seq)))


def _block_diag2(m):
    z = jnp.zeros_like(m[0])
    return jnp.concatenate([jnp.concatenate([m[0], z], axis=1), jnp.concatenate([z, m[1]], axis=1)], axis=0)


def _interleave_heads(a, axis=-1):
    a = jnp.moveaxis(a, axis, -1)
    lead = a.shape[:-1]
    a = a.reshape(lead + (-1, N_RWKV_HEADS, RWKV_HEAD))
    a = jnp.swapaxes(a, -1, -2).reshape(lead + (-1,))
    return jnp.moveaxis(a, -1, axis)


def _pad_to(a, rows, cols):
    return jnp.pad(a, ((0, rows - a.shape[0]), (0, cols - a.shape[1])))


def kernel(x_prompt, x_sample, c, state_wkv, c_ctx, w_ada, b_ada, w_in, rwkv_mu, rwkv_w0, rwkv_w2, rwkv_a0, rwkv_a2, rwkv_g2, rwkv_k_k, rwkv_k_a, rwkv_r_k, rwkv_gn_w, rwkv_gn_b, hy_conv_w, hy_conv_b, hy_f1, hy_fb1, hy_f2, hy_fb2, hy_f3, hy_skip, w_pa, w_pb, w_o, ln_g, ln_b, ffn_w_in, ffn_w_out, router_w, router_b, exp_w_in, exp_w_out):
    batch, seq, d = x_prompt.shape
    dec_batch, dec_seq, _ = x_sample.shape
    depth = w_in.shape[0]
    alpha = (2 * depth) ** 0.25
    n_ctx, n_lat = batch * seq, dec_batch * dec_seq
    assert seq % ROW_TILE == 0 and dec_seq % ROW_TILE == 0 and n_ctx % dec_seq == 0
    assert n_ctx % FFN_ROW_TILE == 0 and n_lat % FFN_ROW_TILE == 0 and dec_seq % FFN_ROW_TILE == 0
    assert seq % RELAYOUT_TBLOCK == 0 and dec_seq % RELAYOUT_TBLOCK == 0
    assert 2 * LORA_W == LANES and 2 * LORA_A == LANES and LORA_G == LANES
    vsplit_ctx, vsplit_lat = (1 if b % 16 == 0 else 2 for b in (batch, dec_batch))
    assert batch % (16 // vsplit_ctx) == 0 and dec_batch % (16 // vsplit_lat) == 0
    assert (n_ctx // dec_seq) % (16 // vsplit_lat) == 0
    tiles_ctx, tiles_lat, n_ctx_tiles = seq // ROW_TILE, dec_seq // ROW_TILE, n_ctx // ROW_TILE
    mod_map = _mod_row_map(n_ctx_tiles, tiles_lat, dec_batch)

    x = _embed(x_prompt, x_sample)
    cond_rows = -(-(dec_batch + 1) // SUBLANES) * SUBLANES
    cond = jnp.zeros((cond_rows, d), F32).at[:dec_batch].set(c).at[dec_batch].set(c_ctx)
    mods = _adaln(cond, w_ada, b_ada).reshape(depth, cond_rows, N_MOD, d)

    il = _interleave_heads
    s0_ctx = jnp.zeros((batch, 2, N_RWKV_HEADS, RWKV_HEAD, RWKV_HEAD), F32)
    ctx_states = []
    ys = jnp.zeros((2, n_ctx + n_lat, D_RWKV), F32)
    y_h = jnp.zeros((n_ctx + n_lat, D_HYENA), BF16)
    for l in range(depth):
        lp = {
            "mu": jnp.concatenate([il(rwkv_mu[l][:, :3 * D_RWKV]), rwkv_mu[l][:, 3 * D_RWKV:]], axis=1),
            "w0": il(rwkv_w0[l]), "w2cat": _block_diag2(il(rwkv_w2[l])), "a0": il(rwkv_a0[l]),
            "a2cat": _block_diag2(il(rwkv_a2[l])), "g2": il(rwkv_g2[l]), "k_k": il(rwkv_k_k[l])[None],
            "k_a": il(rwkv_k_a[l])[None], "r_k": il(rwkv_r_k[l].reshape(1, D_RWKV)),
            "gn_w": il(rwkv_gn_w[l])[None], "gn_b": il(rwkv_gn_b[l])[None],
            "w_pa": il(w_pa[l], axis=0).astype(BF16), "w_pb": w_pb[l].astype(BF16), "w_o": w_o[l].astype(BF16),
            "ln_g1": ln_g[l, 0][None], "ln_b1": ln_b[l, 0][None],
            "hy_conv_w": hy_conv_w[l], "hy_conv_b": hy_conv_b[l][None], "hy_skip": hy_skip[l],
            "hy_f1": _pad_to(hy_f1[l], PAD, PAD), "hy_fb1": _pad_to(hy_fb1[l][None], 1, PAD),
            "hy_f2": _pad_to(hy_f2[l], PAD, PAD), "hy_fb2": _pad_to(hy_fb2[l][None], 1, PAD),
            "hy_f3": _pad_to(hy_f3[l], PAD, 2 * HYENA_ORDER * D_HYENA),
        }
        if l % 2 == 0:
            fp = {"routed": False, "w_in": ffn_w_in[l // 2].astype(BF16), "w_out": ffn_w_out[l // 2].astype(BF16),
                  "router_w": jnp.zeros((d, PAD), F32), "router_b": jnp.zeros((1, PAD), F32)}
        else:
            fp = {"routed": True, "w_in": exp_w_in[l // 2].astype(BF16), "w_out": exp_w_out[l // 2].astype(BF16),
                  "router_w": _pad_to(router_w[l // 2], d, PAD), "router_b": _pad_to(router_b[l // 2][None], 1, PAD)}
        fp["ln_g"], fp["ln_b"] = ln_g[l, 1][None], ln_b[l, 1][None]
        mod = mods[l]

        w_in_l = jnp.concatenate([il(w_in[l][:, :3 * D_RWKV]), w_in[l][:, 3 * D_RWKV:]], axis=1).astype(BF16)
        scan_ops, bonus, g, p_h, gates = _inproj(x, mod, w_in_l, lp, mod_map, tiles_ctx, tiles_lat, n_ctx_tiles)
        ys, s_ctx = _wkv_group(scan_ops, s0_ctx, seq, 0, batch, vsplit_ctx, ys)
        ys, _ = _wkv_group(scan_ops, state_wkv[:, l], dec_seq, n_ctx // dec_seq, dec_batch, vsplit_lat, ys)
        ctx_states.append(s_ctx)

        kf_ctx = _hyena_filters(seq, lp)
        kf_lat = kf_ctx if dec_seq == seq else _hyena_filters(dec_seq, lp)
        y_h = _hyena(p_h, kf_ctx, lp, batch, seq, 0, y_h)
        y_h = _hyena(p_h, kf_lat, lp, dec_batch, dec_seq, n_ctx // dec_seq, y_h)

        x = _tail(ys, bonus, g, y_h, gates, x, mod, lp, mod_map, alpha)
        x = _ffn(x, mod, fp, mod_map, alpha, split_rows=n_ctx if l == depth - 1 else None)

    y_prompt, y_sample = x
    return (y_prompt.reshape(batch, seq, d), y_sample.reshape(dec_batch, dec_seq, d),
            jnp.stack(ctx_states, axis=1))
```

```python
import functools
import math

import jax
import jax.numpy as jnp
import ml_dtypes
import numpy as np
from jax.experimental import pallas as pl
from jax.experimental.pallas import tpu as pltpu

F32 = jnp.float32
BF16 = jnp.bfloat16

D_MODEL = 1024
GRID_W = 64
D_RWKV = 512
RWKV_HEAD = 64
N_RWKV_HEADS = D_RWKV // RWKV_HEAD
LORA_W = 64
LORA_A = 64
LORA_G = 128
D_RWKV_PROJ = 3 * D_RWKV + 2 * LORA_W + 2 * LORA_A + LORA_G
DECAY_SCALE = math.exp(-0.5)
GN_EPS = 64e-5
D_HYENA = 512
HYENA_ORDER = 2
HYENA_EMB = 33
HYENA_BANDS = (HYENA_EMB - 1) // 2
HYENA_HIDDEN = 64
HYENA_FAST_DECAY = 0.3
HYENA_SLOW_DECAY = 1.5
HYENA_TARGET = 1e-2
D_HYENA_PROJ = (HYENA_ORDER + 1) * D_HYENA
D_IN_PROJ = D_RWKV_PROJ + D_HYENA_PROJ + 2 * D_MODEL
D_FF = 2816
N_EXPERTS = 8
TOP_K = 2
D_FF_EXPERT = 1408
N_MOD = 6
LN_EPS = 1e-5
POS_BASE = 10000.0

LANES = 128
SUBLANES = 8
VMEM_LIMIT = 56 * 1024 * 1024

ROW_TILE = 256
HALO = 16
FFN_ROW_TILE = 512
ADA_COL_TILE = 1536
MXU_COLS = 256
HY_COL_TILE = 2 * MXU_COLS
SCAN_VBLOCKS = 8
SCAN_TBLOCK = 64
N_SCAN_SRC = 9
RELAYOUT_TBLOCK = LANES
PAD = LANES


def _cparams(*sem):
    return pltpu.CompilerParams(dimension_semantics=sem, vmem_limit_bytes=VMEM_LIMIT)


def _dot(a, b):
    return jnp.dot(a, b, preferred_element_type=F32)


def _split2(x):
    hi = x.astype(BF16)
    lo = (x - hi.astype(F32)).astype(BF16)
    return hi, lo


def _dot3(a, b):
    ah, al = _split2(a)
    bh, bl = _split2(b)
    return _dot(ah, bh) + _dot(al, bh) + _dot(ah, bl)


def _dot3_short_k(a, b):
    ah, al = _split2(a)
    bh, bl = _split2(b)
    return _dot(jnp.concatenate([ah, al], axis=1), jnp.concatenate([bh, bh], axis=0)) + _dot(ah, bl)


def _dot3_lhs_split(ah, al, b):
    bh, bl = _split2(b)
    return _dot(ah, bh) + _dot(al, bh) + _dot(ah, bl)


def _sigmoid(x):
    return 1.0 / (1.0 + jnp.exp(-x))


def _silu(x):
    return x * _sigmoid(x)


def _head_sum(x):
    s = x[:, 0:LANES]
    for c in range(1, D_RWKV // LANES):
        s = s + x[:, c * LANES:(c + 1) * LANES]
    shift = LANES // 2
    while shift >= N_RWKV_HEADS:
        s = s + pltpu.roll(s, shift, axis=1)
        shift //= 2
    return jnp.concatenate([s] * (D_RWKV // LANES), axis=1)


def _layer_norm(z, g, b):
    mean = jnp.mean(z, axis=-1, keepdims=True)
    d = z - mean
    var = jnp.mean(d * d, axis=-1, keepdims=True)
    return d * jax.lax.rsqrt(var + LN_EPS) * g + b


def _ada_kernel(c_ref, w_ref, b_ref, o_ref):
    o_ref[...] = _dot3(_silu(c_ref[...]), w_ref[...]) + b_ref[...]


def _adaln(cond, w_ada, b_ada):
    depth, d, n = w_ada.shape
    rows = cond.shape[0]
    tn = ADA_COL_TILE
    return pl.pallas_call(
        _ada_kernel,
        grid=(depth, n // tn),
        in_specs=[
            pl.BlockSpec((rows, d), lambda l, j: (0, 0)),
            pl.BlockSpec((None, d, tn), lambda l, j: (l, 0, j)),
            pl.BlockSpec((None, 1, tn), lambda l, j: (l, 0, j)),
        ],
        out_specs=pl.BlockSpec((None, rows, tn), lambda l, j: (l, 0, j)),
        out_shape=jax.ShapeDtypeStruct((depth, rows, n), F32),
        compiler_params=_cparams("arbitrary", "arbitrary"),
        name="adaln",
    )(cond, w_ada, b_ada.reshape(depth, 1, n))


def _mod_row_map(n_ctx_tiles, tiles_per_seq, ctx_row):
    def index_map(i, *_):
        return (jnp.where(i < n_ctx_tiles, ctx_row, (i - n_ctx_tiles) // tiles_per_seq), 0, 0)

    return index_map


def _inproj_kernel(tiles_ctx, tiles_lat, n_ctx_tiles,
                   x_ref, xprev_ref, xnext_ref, mod_ref, w_ref, mu_ref, w0_ref, w2_ref, a0_ref, a2_ref, g2_ref,
                   kk_w_ref, ka_ref, rk_ref,
                   ops_ref, bonus_ref, g_ref, ph_ref, gates_ref):
    i = pl.program_id(0)
    tm = x_ref.shape[0]
    j = jnp.where(i < n_ctx_tiles, i % tiles_ctx, (i - n_ctx_tiles) % tiles_lat)
    per_seq = jnp.where(i < n_ctx_tiles, tiles_ctx, tiles_lat)
    x_ext = jnp.concatenate([xprev_ref[...], x_ref[...], xnext_ref[...]], axis=0)
    h_ext = (x_ext * (1.0 + mod_ref[1:2, :]) + mod_ref[0:1, :]).astype(BF16)
    h = h_ext[HALO:HALO + tm]
    c0, c1 = D_RWKV_PROJ, D_RWKV_PROJ + D_HYENA_PROJ
    p_ext = _dot(h_ext, w_ref[:, 0:c0])
    ph_ref[...] = _dot(h, w_ref[:, c0:c1])
    gates_ref[...] = _dot(h, w_ref[:, c1:D_IN_PROJ]).astype(BF16)

    x = p_ext[HALO:HALO + tm]
    rowid = jax.lax.broadcasted_iota(jnp.int32, (tm, 1), 0)
    prev_row = jnp.where(j == 0, 0.0, p_ext[HALO - 1:HALO])
    next_row = jnp.where(j == per_seq - 1, 0.0, p_ext[HALO + tm:HALO + tm + 1])
    prev = jnp.where(rowid == 0, prev_row, pltpu.roll(x, 1, axis=0))
    nxt = jnp.where(rowid == tm - 1, next_row, pltpu.roll(x, tm - 1, axis=0))
    p = x + mu_ref[0:1, :] * (prev - x) + mu_ref[1:2, :] * (nxt - x)

    d = D_RWKV
    r, k, v = p[:, 0:d], p[:, d:2 * d], p[:, 2 * d:3 * d]
    low_w = p[:, 3 * d:3 * d + LANES]
    low_a = p[:, 3 * d + LANES:3 * d + 2 * LANES]
    low_g = p[:, 3 * d + 2 * LANES:3 * d + 3 * LANES]

    ops_ref[0] = r.T
    ops_ref[1] = v.T
    kk = k * kk_w_ref[...]
    kk = kk * jax.lax.rsqrt(jnp.maximum(_head_sum(kk * kk), 1e-24))
    ops_ref[2] = kk.T
    lw = _dot3_short_k(jnp.tanh(low_w), w2_ref[...])
    la = _dot(low_a.astype(BF16), a2_ref[...].astype(BF16))
    g_ref[...] = _dot(_sigmoid(low_g).astype(BF16), g2_ref[...].astype(BF16))
    ksum = jnp.zeros_like(k)
    for dirn in range(2):
        log_w = -DECAY_SCALE * _sigmoid(w0_ref[dirn:dirn + 1, :] + lw[:, dirn * d:(dirn + 1) * d])
        a = _sigmoid(a0_ref[dirn:dirn + 1, :] + la[:, dirn * d:(dirn + 1) * d])
        k_d = k * (1.0 + (a - 1.0) * ka_ref[...])
        ops_ref[3 + 3 * dirn] = jnp.exp(log_w).T
        ops_ref[4 + 3 * dirn] = k_d.T
        ops_ref[5 + 3 * dirn] = (kk * a).T
        ksum = ksum + k_d
    bonus_ref[...] = _head_sum(r * (0.5 * ksum) * rk_ref[...]) * v


def _inproj(x, mod, w_in_bf, lp, mod_map, tiles_ctx, tiles_lat, n_ctx_tiles):
    n = x.shape[0]
    tm = ROW_TILE
    halo = tm // HALO
    n_halo = n // HALO
    row = lambda width: pl.BlockSpec((tm, width), lambda i: (i, 0))
    full = lambda a: pl.BlockSpec(a.shape, lambda i: (0,) * a.ndim)
    params = [lp[k] for k in ("mu", "w0", "w2cat", "a0", "a2cat", "g2", "k_k", "k_a", "r_k")]
    out = lambda width: jax.ShapeDtypeStruct((n, width), F32)
    return pl.pallas_call(
        functools.partial(_inproj_kernel, tiles_ctx, tiles_lat, n_ctx_tiles),
        grid=(n // tm,),
        in_specs=[
            row(D_MODEL),
            pl.BlockSpec((HALO, D_MODEL), lambda i: (jnp.maximum(i * halo - 1, 0), 0)),
            pl.BlockSpec((HALO, D_MODEL), lambda i: (jnp.minimum((i + 1) * halo, n_halo - 1), 0)),
            pl.BlockSpec((None, N_MOD, D_MODEL), mod_map),
            pl.BlockSpec((D_MODEL, D_IN_PROJ), lambda i: (0, 0)),
        ] + [full(a) for a in params],
        out_specs=[pl.BlockSpec((N_SCAN_SRC, D_RWKV, tm), lambda i: (0, 0, i)),
                   row(D_RWKV), row(D_RWKV), row(D_HYENA_PROJ), row(2 * D_MODEL)],
        out_shape=[jax.ShapeDtypeStruct((N_SCAN_SRC, D_RWKV, n), F32), out(D_RWKV), out(D_RWKV),
                   out(D_HYENA_PROJ), jax.ShapeDtypeStruct((n, 2 * D_MODEL), BF16)],
        compiler_params=_cparams("arbitrary"),
        name="inproj",
    )(x, x, x, mod, w_in_bf, *params)


def _tail_kernel(alpha, yf_ref, yb_ref, bonus_ref, g_ref, yh_ref, gate_ref, x_ref, mod_ref,
                 gnw_ref, gnb_ref, wpa_ref, wpb_ref, wo_ref, lng_ref, lnb_ref, o_ref):
    y = yf_ref[...] + yb_ref[...]
    mean = _head_sum(y) * (1.0 / RWKV_HEAD)
    d = y - mean
    var = _head_sum(d * d) * (1.0 / RWKV_HEAD)
    y_n = d * jax.lax.rsqrt(var + GN_EPS) * gnw_ref[...] + gnb_ref[...]
    y_a = ((y_n + bonus_ref[...]) * g_ref[...]).astype(BF16)
    merged = (_sigmoid(gate_ref[:, 0:D_MODEL].astype(F32)) * _dot(y_a, wpa_ref[...])
              + _sigmoid(gate_ref[:, D_MODEL:2 * D_MODEL].astype(F32)) * _dot(yh_ref[...], wpb_ref[...]))
    m = _dot(merged.astype(BF16), wo_ref[...])
    z = alpha * x_ref[...] + mod_ref[2:3, :] * m
    o_ref[...] = _layer_norm(z, lng_ref[...], lnb_ref[...])


def _tail(ys, bonus, g, y_h, gates, x, mod, lp, mod_map, alpha):
    n = x.shape[0]
    tm = ROW_TILE
    row = lambda width: pl.BlockSpec((tm, width), lambda i: (i, 0))
    scan_dir = lambda d: pl.BlockSpec((None, tm, D_RWKV), lambda i: (d, i, 0))
    full = lambda a: pl.BlockSpec(a.shape, lambda i: (0,) * a.ndim)
    params = [lp[k] for k in ("gn_w", "gn_b", "w_pa", "w_pb", "w_o", "ln_g1", "ln_b1")]
    return pl.pallas_call(
        functools.partial(_tail_kernel, alpha),
        grid=(n // tm,),
        in_specs=[scan_dir(0), scan_dir(1)] + [row(D_RWKV)] * 3
        + [row(2 * D_MODEL), row(D_MODEL), pl.BlockSpec((None, N_MOD, D_MODEL), mod_map)]
        + [full(a) for a in params],
        out_specs=row(D_MODEL),
        out_shape=jax.ShapeDtypeStruct((n, D_MODEL), F32),
        compiler_params=_cparams("arbitrary"),
        name="mixer_tail",
    )(ys, ys, bonus, g, y_h, gates, x, mod, *params)


def _ffn_kernel(routed, alpha, split_tiles, x_ref, mod_ref, wgu_ref, wd_ref, rw_ref, rb_ref,
                lng_ref, lnb_ref, *rest):
    out_refs, (h_ref, acc_ref, comb_ref) = rest[:-3], rest[-3:]
    e = pl.program_id(1)
    lane = jax.lax.broadcasted_iota(jnp.int32, comb_ref.shape, 1)

    @pl.when(e == 0)
    def _():
        h = x_ref[...] * (1.0 + mod_ref[4:5, :]) + mod_ref[3:4, :]
        h_ref[...] = h.astype(BF16)
        acc_ref[...] = jnp.zeros_like(acc_ref)
        if routed:
            logits = _dot3(h, rw_ref[...]) + rb_ref[...]
            logits = jnp.where(lane < N_EXPERTS, logits, -jnp.inf)
            ex = jnp.exp(logits - jnp.max(logits, axis=-1, keepdims=True))
            probs = ex / jnp.sum(ex, axis=-1, keepdims=True)
            p1 = jnp.max(probs, axis=-1, keepdims=True)
            i1 = jnp.min(jnp.where(probs == p1, lane, PAD), axis=-1, keepdims=True)
            rest = jnp.where(lane == i1, -1.0, probs)
            p2 = jnp.max(rest, axis=-1, keepdims=True)
            i2 = jnp.min(jnp.where(rest == p2, lane, PAD), axis=-1, keepdims=True)
            total = p1 + p2
            comb_ref[...] = jnp.where(lane == i1, p1 / total, 0.0) + jnp.where(lane == i2, p2 / total, 0.0)

    gate_up = _dot(h_ref[...], wgu_ref[...])
    width = gate_up.shape[1] // 2
    act = _silu(gate_up[:, 0:width]) * gate_up[:, width:]
    if routed:
        act = act * jnp.sum(jnp.where(lane == e, comb_ref[...], 0.0), axis=-1, keepdims=True)
    acc_ref[...] += _dot(act.astype(BF16), wd_ref[...])

    def finish(o_ref):
        z = alpha * x_ref[...] + mod_ref[5:6, :] * acc_ref[...]
        o_ref[...] = _layer_norm(z, lng_ref[...], lnb_ref[...])

    last = e == pl.num_programs(1) - 1
    if split_tiles is None:
        pl.when(last)(lambda: finish(out_refs[0]))
    else:
        first_group = pl.program_id(0) < split_tiles
        pl.when(last & first_group)(lambda: finish(out_refs[0]))
        pl.when(last & jnp.logical_not(first_group))(lambda: finish(out_refs[1]))


def _ffn(x, mod, fp, mod_map, alpha, split_rows=None):
    n = x.shape[0]
    tm = FFN_ROW_TILE
    scale = tm // ROW_TILE
    routed, layer = fp["routed"], fp["layer"]
    if routed:
        groups = N_EXPERTS
        wgu_spec = pl.BlockSpec((None, None, D_MODEL, 2 * D_FF_EXPERT), lambda i, e: (layer, e, 0, 0))
        wd_spec = pl.BlockSpec((None, None, D_FF_EXPERT, D_MODEL), lambda i, e: (layer, e, 0, 0))
    else:
        groups = 1
        once = dict(pipeline_mode=pl.Buffered(1))
        wgu_spec = pl.BlockSpec((None, D_MODEL, 2 * D_FF), lambda i, e: (layer, 0, 0), **once)
        wd_spec = pl.BlockSpec((None, D_FF, D_MODEL), lambda i, e: (layer, 0, 0), **once)
    full = lambda a: pl.BlockSpec(a.shape, lambda i, e: (0,) * a.ndim)
    row = pl.BlockSpec((tm, D_MODEL), lambda i, e: (i, 0))
    mod_spec = pl.BlockSpec((None, N_MOD, D_MODEL), lambda i, e: mod_map(i * scale))
    if split_rows is None:
        split_tiles, out_specs, out_shape = None, row, jax.ShapeDtypeStruct((n, D_MODEL), F32)
    else:
        split_tiles = split_rows // tm
        out_specs = [pl.BlockSpec((tm, D_MODEL), lambda i, e: (jnp.minimum(i, split_tiles - 1), 0)),
                     pl.BlockSpec((tm, D_MODEL), lambda i, e: (jnp.maximum(i - split_tiles, 0), 0))]
        out_shape = [jax.ShapeDtypeStruct((split_rows, D_MODEL), F32),
                     jax.ShapeDtypeStruct((n - split_rows, D_MODEL), F32)]
    return pl.pallas_call(
        functools.partial(_ffn_kernel, routed, alpha, split_tiles),
        grid=(n // tm, groups),
        in_specs=[row, mod_spec, wgu_spec, wd_spec, full(fp["router_w"]), full(fp["router_b"]),
                  full(fp["ln_g"]), full(fp["ln_b"])],
        out_specs=out_specs,
        out_shape=out_shape,
        scratch_shapes=[pltpu.VMEM((tm, D_MODEL), BF16), pltpu.VMEM((tm, D_MODEL), F32),
                        pltpu.VMEM((tm, PAD), F32)],
        compiler_params=_cparams("arbitrary", "arbitrary"),
        name="moe_ffn" if routed else "dense_ffn",
    )(x, mod, fp["w_in"], fp["w_out"], fp["router_w"], fp["router_b"], fp["ln_g"], fp["ln_b"])


def _scan_kernel(r_ref, kk_ref, w_ref, k_ref, b_ref, v_ref, s0_ref, y_ref, sfin_ref, s_ref):
    tb = r_ref.shape[1]
    n_vblocks = s_ref.shape[0]
    tile = (SUBLANES, LANES)
    backward = pl.program_id(0) % 2 == 1

    @pl.when(pl.program_id(1) == 0)
    def _():
        s_ref[...] = s0_ref[...]

    def time_of(i):
        return jnp.where(backward, tb - 1 - i, i)

    def row(ref, k, t):
        return jnp.broadcast_to(ref[k, pl.ds(t, 1), :], tile)

    per_sweep = min(SCAN_VBLOCKS, n_vblocks)
    for part in range(n_vblocks // per_sweep):
        vbs = [part * per_sweep + j for j in range(per_sweep)]

        def step(i, sa, vbs=vbs):
            t = time_of(i)
            t_next = time_of(jnp.minimum(i + 1, tb - 1))
            v8 = [jnp.concatenate([v_ref[vb * SUBLANES + j, pl.ds(t, 1), :] for j in range(SUBLANES)], axis=0)
                  for vb in vbs]
            y = [jnp.zeros(tile, F32) for _ in vbs]
            sa_next = [jnp.zeros(tile, F32) for _ in vbs]
            for k in range(RWKV_HEAD):
                w, b, kd, r = row(w_ref, k, t), row(b_ref, k, t), row(k_ref, k, t), row(r_ref, k, t)
                kap = row(kk_ref, k, t_next)
                for j, vb in enumerate(vbs):
                    s = s_ref[vb, k] * w - sa[j] * b + v8[j] * kd
                    s_ref[vb, k] = s
                    y[j] = y[j] + s * r
                    sa_next[j] = sa_next[j] + s * kap
            for j, vb in enumerate(vbs):
                y_ref[t, pl.ds(vb * SUBLANES, SUBLANES), :] = y[j]
            return tuple(sa_next)

        t0 = time_of(0)
        sa0 = [jnp.zeros(tile, F32) for _ in vbs]
        for k in range(RWKV_HEAD):
            kap = row(kk_ref, k, t0)
            for j, vb in enumerate(vbs):
                sa0[j] = sa0[j] + s_ref[vb, k] * kap
        jax.lax.fori_loop(0, tb, step, tuple(sa0))

    @pl.when(pl.program_id(1) == pl.num_programs(1) - 1)
    def _():
        sfin_ref[...] = s_ref[...]


def _scan(xk, vk, s0):
    _, n_sg, hd, t_len, _ = xk.shape
    rows = vk.shape[1]
    tb = SCAN_TBLOCK
    n_t = t_len // tb
    t_of = lambda g, t: jnp.where(g % 2 == 1, n_t - 1 - t, t)
    shared = lambda o: pl.BlockSpec((None, None, hd, tb, LANES), lambda g, t: (o, g // 2, 0, t_of(g, t), 0))
    per_dir = lambda o: pl.BlockSpec((None, None, hd, tb, LANES),
                                     lambda g, t: (o + 3 * (g % 2), g // 2, 0, t_of(g, t), 0))
    st = pl.BlockSpec((None, rows // SUBLANES, hd, SUBLANES, LANES), lambda g, t: (g, 0, 0, 0, 0))
    return pl.pallas_call(
        _scan_kernel,
        grid=(2 * n_sg, n_t),
        in_specs=[shared(0), shared(1), per_dir(2), per_dir(3), per_dir(4),
                  pl.BlockSpec((None, rows, tb, LANES), lambda g, t: (g // 2, 0, t_of(g, t), 0)), st],
        out_specs=[pl.BlockSpec((None, tb, rows, LANES), lambda g, t: (g, t_of(g, t), 0, 0)), st],
        out_shape=[jax.ShapeDtypeStruct((2 * n_sg, t_len, rows, LANES), F32),
                   jax.ShapeDtypeStruct(s0.shape, F32)],
        scratch_shapes=[pltpu.VMEM(s0.shape[1:], F32)],
        compiler_params=_cparams("arbitrary", "arbitrary"),
        name="wkv_scan",
    )(xk, xk, xk, xk, xk, vk, s0)


def _to_chains_kernel(lane_parts, *refs):
    src_refs, o_ref = refs[:-1], refs[-1]

    def heads_of(c):
        rows = [src[pl.ds(pl.multiple_of(c * N_RWKV_HEADS, N_RWKV_HEADS), N_RWKV_HEADS), :] for src in src_refs]
        return jnp.concatenate(rows, axis=0)

    def per_channel(c, carry):
        parts = {off: heads_of(off + c) for off in set(lane_parts)}
        o_ref[c] = jnp.concatenate([parts[off] for off in lane_parts], axis=0).T
        return carry

    jax.lax.fori_loop(0, o_ref.shape[0], per_channel, 0, unroll=4)


def _to_chains(src, stream_map, n_streams, lane_parts, seq_len, n_seq_blk, first_blk, n_blk):
    tb = RELAYOUT_TBLOCK
    n_t = seq_len // tb
    channels = RWKV_HEAD // len(set(lane_parts))

    def seq_spec(j):
        return pl.BlockSpec((None, D_RWKV, tb),
                            lambda s, g, t: (stream_map(s), 0, ((first_blk + g) * n_seq_blk + j) * n_t + t))

    return pl.pallas_call(
        functools.partial(_to_chains_kernel, lane_parts),
        grid=(n_streams, n_blk, n_t),
        in_specs=[seq_spec(j) for j in range(n_seq_blk)],
        out_specs=pl.BlockSpec((None, None, channels, tb, LANES), lambda s, g, t: (s, g, 0, t, 0)),
        out_shape=jax.ShapeDtypeStruct((n_streams, n_blk, channels, seq_len, LANES), F32),
        compiler_params=_cparams("arbitrary", "arbitrary", "arbitrary"),
        name="to_chains",
    )(*([src] * n_seq_blk))


def _from_chains_kernel(n_vsplit, yf_ref, yb_ref, prev_ref, o_ref, z_ref):
    del prev_ref
    _, n_seq_blk, tb, _ = o_ref.shape
    rows = RWKV_HEAD // n_vsplit
    chains = LANES // n_vsplit
    for y_ref, dst in ((yf_ref, o_ref.at[0]), (yb_ref, o_ref.at[1])):
        def per_row(v, carry, y_ref=y_ref):
            z_ref[pl.ds(pl.multiple_of(v * LANES, LANES), LANES), :] = y_ref[pl.ds(v, tb, stride=rows), :].T
            return carry

        jax.lax.fori_loop(0, rows, per_row, 0, unroll=4)
        for s in range(n_seq_blk):
            pieces = [z_ref[pl.ds(v * LANES + part * chains + s * N_RWKV_HEADS, N_RWKV_HEADS), :]
                      for part in range(n_vsplit) for v in range(rows)]
            dst[s] = jnp.concatenate(pieces, axis=0).T


def _from_chains(y, n_vsplit, n_blocks, first_blk, prev):
    n_lg, seq_len, rows, _ = y.shape
    tb = RELAYOUT_TBLOCK
    n_seq_blk = LANES // n_vsplit // N_RWKV_HEADS
    y2 = y.reshape(n_lg, seq_len * rows, LANES)
    shape = (2, n_blocks, n_seq_blk, seq_len, D_RWKV)
    y_spec = lambda d: pl.BlockSpec((None, tb * rows, LANES), lambda g, t: (2 * g + d, t, 0))
    out = pl.pallas_call(
        functools.partial(_from_chains_kernel, n_vsplit),
        grid=(n_lg // 2, seq_len // tb),
        in_specs=[y_spec(0), y_spec(1), pl.BlockSpec(memory_space=pl.ANY)],
        out_specs=pl.BlockSpec((2, None, n_seq_blk, tb, D_RWKV), lambda g, t: (0, first_blk + g, 0, t, 0)),
        out_shape=jax.ShapeDtypeStruct(shape, F32),
        input_output_aliases={2: 0},
        scratch_shapes=[pltpu.VMEM((rows * LANES, tb), F32)],
        compiler_params=_cparams("arbitrary", "arbitrary"),
        name="from_chains",
    )(y2, y2, prev.reshape(shape))
    return out.reshape(prev.shape)


def _wkv_group(ops, s0, seq_len, first_seq, n_seq, n_vsplit, prev_y):
    _, d, n_tok = ops.shape
    n_seq_blk = LANES // n_vsplit // N_RWKV_HEADS
    n_sg = n_seq // n_seq_blk
    rows = RWKV_HEAD // n_vsplit
    first_blk = first_seq // n_seq_blk
    xk = _to_chains(ops, lambda s: s + jnp.where(s >= 1, 1, 0), N_SCAN_SRC - 1, (0,) * n_vsplit,
                    seq_len, n_seq_blk, first_blk, n_sg)
    vk = _to_chains(ops, lambda s: 1, 1, tuple(p * rows for p in range(n_vsplit)),
                    seq_len, n_seq_blk, first_blk, n_sg)[0]
    s0c = s0.reshape(n_sg, n_seq_blk, 2, N_RWKV_HEADS, n_vsplit, rows // SUBLANES, SUBLANES, RWKV_HEAD)
    s0c = jnp.transpose(s0c, (0, 2, 5, 7, 6, 4, 1, 3)).reshape(2 * n_sg, rows // SUBLANES, RWKV_HEAD, SUBLANES, LANES)
    y, s_fin = _scan(xk, vk, s0c)
    ys = _from_chains(y, n_vsplit, n_tok // seq_len // n_seq_blk, first_blk, prev_y)
    s_fin = s_fin.reshape(n_sg, 2, rows // SUBLANES, RWKV_HEAD, SUBLANES, n_vsplit, n_seq_blk, N_RWKV_HEADS)
    s_fin = jnp.transpose(s_fin, (0, 6, 1, 7, 5, 2, 4, 3)).reshape(n_seq, 2, N_RWKV_HEADS, RWKV_HEAD, RWKV_HEAD)
    return ys, s_fin


@functools.lru_cache(maxsize=None)
def _dft_constants(seq_len):
    n = 2 * seq_len
    idx = np.arange(seq_len)
    ang = (2.0 * np.pi / n) * ((idx[:, None] * idx[None, :]) % n)
    alt = np.where(idx % 2 == 0, 1.0, -1.0)
    f_re = np.cos(ang)
    f_im = -np.sin(ang)
    f_im[0, :] = alt
    fwd = np.concatenate([f_re, f_im], axis=0)
    c = np.full((seq_len,), 2.0)
    c[0] = 1.0
    g_re = np.cos(ang.T) * c[None, :] / n
    g_im = -2.0 * np.sin(ang.T) / n
    g_im[:, 0] = alt / n
    inv = np.concatenate([g_re, g_im], axis=1)

    def split(m):
        hi = m.astype(ml_dtypes.bfloat16)
        lo = (m - hi.astype(np.float64)).astype(ml_dtypes.bfloat16)
        return hi, lo

    return split(fwd) + split(inv)


@functools.lru_cache(maxsize=None)
def _filter_constants(seq_len):
    t = np.linspace(0.0, 1.0, seq_len)[:, None]
    f = np.linspace(1e-4, HYENA_BANDS - 1, HYENA_BANDS)
    ang = (2.0 * np.pi / seq_len) * np.arange(seq_len)[:, None] * f
    feats = np.zeros((seq_len, PAD), np.float32)
    feats[:, :HYENA_EMB] = np.concatenate([t, np.cos(ang), -np.sin(ang)], axis=-1)
    deltas = np.abs(np.linspace(math.log(HYENA_TARGET) / HYENA_SLOW_DECAY,
                                math.log(HYENA_TARGET) / HYENA_FAST_DECAY, D_HYENA))
    decay = np.exp(-t * deltas).astype(np.float32)
    return feats, np.tile(decay, (1, 2 * HYENA_ORDER))


def _filter_kernel(feats_ref, decay_ref, f1_ref, b1_ref, f2_ref, b2_ref, f3_ref, h_ref):
    hid = jnp.sin(_dot3(feats_ref[...], f1_ref[...]) + b1_ref[...])
    hid = jnp.sin(_dot3(hid, f2_ref[...]) + b2_ref[...])
    h_ref[...] = _dot3(hid, f3_ref[...]) * decay_ref[...]


def _spectrum_kernel(fh_ref, fl_ref, hf_ref, hb_ref, kf_ref):
    seq_len = hf_ref.shape[0]
    row = jax.lax.broadcasted_iota(jnp.int32, (seq_len, 1), 0)
    h_f = hf_ref[...]
    h_b = jnp.where(row == 0, 0.0, hb_ref[...])
    kf_ref[0:seq_len, :] = _dot3_lhs_split(fh_ref[0:seq_len, :], fl_ref[0:seq_len, :], h_f + h_b)
    im = _dot3_lhs_split(fh_ref[seq_len:, :], fl_ref[seq_len:, :], h_f - h_b)
    alternating = jnp.where(row % 2 == 0, 1.0, -1.0)
    nyquist_fix = 2.0 * jnp.sum(alternating * h_b, axis=0, keepdims=True)
    kf_ref[seq_len:, :] = jnp.where(row == 0, im + nyquist_fix, im)


def _hyena_filters(seq_len, lp):
    feats, decay = _filter_constants(seq_len)
    fh, fl, _, _ = _dft_constants(seq_len)
    args = [jnp.asarray(feats), jnp.asarray(decay), lp["hy_f1"], lp["hy_fb1"], lp["hy_f2"], lp["hy_fb2"],
            lp["hy_f3"]]
    h = pl.pallas_call(
        _filter_kernel,
        out_shape=jax.ShapeDtypeStruct((seq_len, 2 * HYENA_ORDER * D_HYENA), F32),
        compiler_params=_cparams(),
        name="hyena_filter_mlp",
    )(*args)
    tc = HY_COL_TILE
    per = D_HYENA // tc
    const = pl.BlockSpec((2 * seq_len, seq_len), lambda n, c: (0, 0), pipeline_mode=pl.Buffered(1))
    return pl.pallas_call(
        _spectrum_kernel,
        grid=(HYENA_ORDER, per),
        in_specs=[const, const,
                  pl.BlockSpec((seq_len, tc), lambda n, c: (0, (2 * n) * per + c)),
                  pl.BlockSpec((seq_len, tc), lambda n, c: (0, (2 * n + 1) * per + c))],
        out_specs=pl.BlockSpec((2 * seq_len, tc), lambda n, c: (0, n * per + c)),
        out_shape=jax.ShapeDtypeStruct((2 * seq_len, HYENA_ORDER * D_HYENA), F32),
        compiler_params=_cparams("arbitrary", "arbitrary"),
        name="hyena_filter_spectrum",
    )(jnp.asarray(fh), jnp.asarray(fl), h, h)


def _hyena_kernel(pz_ref, pg1_ref, pg2_ref, cwz_ref, cwg1_ref, cwg2_ref, cbz_ref, cbg1_ref, cbg2_ref,
                  kf0_ref, kf1_ref, skip_ref, f_ref, g_ref, *rest):
    o_ref = rest[-1]
    seq_len = pz_ref.shape[0]
    row = jax.lax.broadcasted_iota(jnp.int32, (seq_len, 1), 0)

    def short_conv(p_ref, cw_ref, cb_ref, cols):
        x = p_ref[:, cols]
        prev = jnp.where(row == 0, 0.0, pltpu.roll(x, 1, axis=0))
        nxt = jnp.where(row == seq_len - 1, 0.0, pltpu.roll(x, seq_len - 1, axis=0))
        return cw_ref[0:1, cols] * prev + cw_ref[1:2, cols] * x + cw_ref[2:3, cols] * nxt + cb_ref[:, cols]

    for c0 in range(0, pz_ref.shape[1], HY_COL_TILE):
        cols = slice(c0, c0 + HY_COL_TILE)
        z = short_conv(pz_ref, cwz_ref, cbz_ref, cols)
        gates = (short_conv(pg1_ref, cwg1_ref, cbg1_ref, cols), short_conv(pg2_ref, cwg2_ref, cbg2_ref, cols))
        for n, kf_ref in enumerate((kf0_ref, kf1_ref)):
            zf = _dot(f_ref[...], z.astype(BF16))
            z_re, z_im = zf[0:seq_len], zf[seq_len:]
            k_re, k_im = kf_ref[0:seq_len, cols], kf_ref[seq_len:, cols]
            p_re = jnp.where(row == 0, z_re * k_re, z_re * k_re - z_im * k_im)
            p_im = jnp.where(row == 0, z_im * k_im, z_re * k_im + z_im * k_re)
            conv = (_dot(g_ref[:, 0:seq_len], p_re.astype(BF16)) + _dot(g_ref[:, seq_len:], p_im.astype(BF16)))
            z = gates[n] * (conv + skip_ref[n:n + 1, cols] * z)
        o_ref[:, cols] = z.astype(o_ref.dtype)


def _hyena(p_h, kf, lp, n_seq, seq_len, seq_offset, prev):
    tc = HY_COL_TILE
    per = D_HYENA // tc
    fh, _, gh, _ = [jnp.asarray(a) for a in _dft_constants(seq_len)]
    once = dict(pipeline_mode=pl.Buffered(1))
    seg = lambda s: pl.BlockSpec((seq_len, tc), lambda b, c: (seq_offset + b, s * per + c))
    par = lambda rows, s: pl.BlockSpec((rows, tc), lambda b, c: (0, s * per + c))
    fconst = pl.BlockSpec((2 * seq_len, seq_len), lambda b, c: (0, 0), **once)
    gconst = pl.BlockSpec((seq_len, 2 * seq_len), lambda b, c: (0, 0), **once)
    cw, cb = lp["hy_conv_w"], lp["hy_conv_b"]
    return pl.pallas_call(
        _hyena_kernel,
        grid=(n_seq, per),
        in_specs=[seg(0), seg(1), seg(2), par(3, 0), par(3, 1), par(3, 2), par(1, 0), par(1, 1), par(1, 2),
                  pl.BlockSpec((2 * seq_len, tc), lambda b, c: (0, c), **(once if per == 1 else {})),
                  pl.BlockSpec((2 * seq_len, tc), lambda b, c: (0, per + c), **(once if per == 1 else {})),
                  pl.BlockSpec((HYENA_ORDER, tc), lambda b, c: (0, c)),
                  fconst, gconst, pl.BlockSpec(memory_space=pl.ANY)],
        out_specs=pl.BlockSpec((seq_len, tc), lambda b, c: (seq_offset + b, c)),
        out_shape=jax.ShapeDtypeStruct((p_h.shape[0], D_HYENA), BF16),
        input_output_aliases={14: 0},
        compiler_params=_cparams("arbitrary", "arbitrary"),
        name="hyena_conv",
    )(p_h, p_h, p_h, cw, cw, cw, cb, cb, cb, kf, kf, lp["hy_skip"], fh, gh, prev)


@functools.lru_cache(maxsize=None)
def _grid_pos_embed(n_tokens):
    rows = n_tokens // GRID_W
    row = np.repeat(np.arange(rows, dtype=np.float64), GRID_W)
    col = np.tile(np.arange(GRID_W, dtype=np.float64), rows)
    quarter = D_MODEL // 4
    omega = 1.0 / (POS_BASE ** (np.arange(quarter, dtype=np.float64) / quarter))

    def enc(pos):
        ang = pos[:, None] * omega
        return np.concatenate([np.sin(ang), np.cos(ang)], axis=-1)

    return np.concatenate([enc(row), enc(col)], axis=-1).astype(np.float32)


def _embed_kernel(n_ctx_tiles, xp_ref, xs_ref, pos_ref, o_ref):
    i = pl.program_id(0)

    @pl.when(i < n_ctx_tiles)
    def _():
        o_ref[...] = xp_ref[...]

    @pl.when(i >= n_ctx_tiles)
    def _():
        o_ref[...] = xs_ref[...] + pos_ref[...]


def _embed(x_prompt, x_sample):
    batch, seq, d = x_prompt.shape
    dec_batch, dec_seq, _ = x_sample.shape
    tm = ROW_TILE
    n_ctx_tiles, n_lat_tiles, per_seq = batch * seq // tm, dec_batch * dec_seq // tm, dec_seq // tm
    lat_tile = lambda i: jnp.maximum(i - n_ctx_tiles, 0)
    return pl.pallas_call(
        functools.partial(_embed_kernel, n_ctx_tiles),
        grid=(n_ctx_tiles + n_lat_tiles,),
        in_specs=[pl.BlockSpec((tm, d), lambda i: (jnp.minimum(i, n_ctx_tiles - 1), 0)),
                  pl.BlockSpec((tm, d), lambda i: (lat_tile(i), 0)),
                  pl.BlockSpec((tm, d), lambda i: (lat_tile(i) % per_seq, 0))],
        out_specs=pl.BlockSpec((tm, d), lambda i: (i, 0)),
        out_shape=jax.ShapeDtypeStruct((batch * seq + dec_batch * dec_seq, d), F32),
        compiler_params=_cparams("arbitrary"),
        name="embed",
    )(x_prompt.reshape(batch * seq, d), x_sample.reshape(dec_batch * dec_seq, d),
      jnp.asarray(_grid_pos_embed(dec_seq)))


def _block_diag2(m):
    z = jnp.zeros_like(m[0])
    return jnp.concatenate([jnp.concatenate([m[0], z], axis=1), jnp.concatenate([z, m[1]], axis=1)], axis=0)


def _interleave_heads(a, axis=-1):
    a = jnp.moveaxis(a, axis, -1)
    lead = a.shape[:-1]
    a = a.reshape(lead + (-1, N_RWKV_HEADS, RWKV_HEAD))
    a = jnp.swapaxes(a, -1, -2).reshape(lead + (-1,))
    return jnp.moveaxis(a, -1, axis)


def _pad_to(a, rows, cols):
    return jnp.pad(a, ((0, rows - a.shape[0]), (0, cols - a.shape[1])))


def kernel(x_prompt, x_sample, c, state_wkv, c_ctx, w_ada, b_ada, w_in, rwkv_mu, rwkv_w0, rwkv_w2, rwkv_a0, rwkv_a2, rwkv_g2, rwkv_k_k, rwkv_k_a, rwkv_r_k, rwkv_gn_w, rwkv_gn_b, hy_conv_w, hy_conv_b, hy_f1, hy_fb1, hy_f2, hy_fb2, hy_f3, hy_skip, w_pa, w_pb, w_o, ln_g, ln_b, ffn_w_in, ffn_w_out, router_w, router_b, exp_w_in, exp_w_out):
    batch, seq, d = x_prompt.shape
    dec_batch, dec_seq, _ = x_sample.shape
    depth = w_in.shape[0]
    alpha = (2 * depth) ** 0.25
    n_ctx, n_lat = batch * seq, dec_batch * dec_seq
    assert seq % ROW_TILE == 0 and dec_seq % ROW_TILE == 0 and n_ctx % dec_seq == 0
    assert n_ctx % FFN_ROW_TILE == 0 and n_lat % FFN_ROW_TILE == 0 and dec_seq % FFN_ROW_TILE == 0
    assert seq % RELAYOUT_TBLOCK == 0 and dec_seq % RELAYOUT_TBLOCK == 0
    assert 2 * LORA_W == LANES and 2 * LORA_A == LANES and LORA_G == LANES
    vsplit_ctx, vsplit_lat = (1 if b % 16 == 0 else 2 for b in (batch, dec_batch))
    assert batch % (16 // vsplit_ctx) == 0 and dec_batch % (16 // vsplit_lat) == 0
    assert (n_ctx // dec_seq) % (16 // vsplit_lat) == 0
    tiles_ctx, tiles_lat, n_ctx_tiles = seq // ROW_TILE, dec_seq // ROW_TILE, n_ctx // ROW_TILE
    mod_map = _mod_row_map(n_ctx_tiles, tiles_lat, dec_batch)

    x = _embed(x_prompt, x_sample)
    cond_rows = -(-(dec_batch + 1) // SUBLANES) * SUBLANES
    cond = jnp.zeros((cond_rows, d), F32).at[:dec_batch].set(c).at[dec_batch].set(c_ctx)
    mods = _adaln(cond, w_ada, b_ada).reshape(depth, cond_rows, N_MOD, d)

    il = _interleave_heads
    ffn_w_in_bf, ffn_w_out_bf = ffn_w_in.astype(BF16), ffn_w_out.astype(BF16)
    exp_w_in_bf, exp_w_out_bf = exp_w_in.astype(BF16), exp_w_out.astype(BF16)
    s0_ctx = jnp.zeros((batch, 2, N_RWKV_HEADS, RWKV_HEAD, RWKV_HEAD), F32)
    ctx_states = []
    ys = jnp.zeros((2, n_ctx + n_lat, D_RWKV), F32)
    y_h = jnp.zeros((n_ctx + n_lat, D_HYENA), BF16)
    for l in range(depth):
        lp = {
            "mu": jnp.concatenate([il(rwkv_mu[l][:, :3 * D_RWKV]), rwkv_mu[l][:, 3 * D_RWKV:]], axis=1),
            "w0": il(rwkv_w0[l]), "w2cat": _block_diag2(il(rwkv_w2[l])), "a0": il(rwkv_a0[l]),
            "a2cat": _block_diag2(il(rwkv_a2[l])), "g2": il(rwkv_g2[l]), "k_k": il(rwkv_k_k[l])[None],
            "k_a": il(rwkv_k_a[l])[None], "r_k": il(rwkv_r_k[l].reshape(1, D_RWKV)),
            "gn_w": il(rwkv_gn_w[l])[None], "gn_b": il(rwkv_gn_b[l])[None],
            "w_pa": il(w_pa[l], axis=0).astype(BF16), "w_pb": w_pb[l].astype(BF16), "w_o": w_o[l].astype(BF16),
            "ln_g1": ln_g[l, 0][None], "ln_b1": ln_b[l, 0][None],
            "hy_conv_w": hy_conv_w[l], "hy_conv_b": hy_conv_b[l][None], "hy_skip": hy_skip[l],
            "hy_f1": _pad_to(hy_f1[l], PAD, PAD), "hy_fb1": _pad_to(hy_fb1[l][None], 1, PAD),
            "hy_f2": _pad_to(hy_f2[l], PAD, PAD), "hy_fb2": _pad_to(hy_fb2[l][None], 1, PAD),
            "hy_f3": _pad_to(hy_f3[l], PAD, 2 * HYENA_ORDER * D_HYENA),
        }
        if l % 2 == 0:
            fp = {"routed": False, "layer": l // 2, "w_in": ffn_w_in_bf, "w_out": ffn_w_out_bf,
                  "router_w": jnp.zeros((d, PAD), F32), "router_b": jnp.zeros((1, PAD), F32)}
        else:
            fp = {"routed": True, "layer": l // 2, "w_in": exp_w_in_bf, "w_out": exp_w_out_bf,
                  "router_w": _pad_to(router_w[l // 2], d, PAD), "router_b": _pad_to(router_b[l // 2][None], 1, PAD)}
        fp["ln_g"], fp["ln_b"] = ln_g[l, 1][None], ln_b[l, 1][None]
        mod = mods[l]

        w_in_l = jnp.concatenate([il(w_in[l][:, :3 * D_RWKV]), w_in[l][:, 3 * D_RWKV:]], axis=1).astype(BF16)
        scan_ops, bonus, g, p_h, gates = _inproj(x, mod, w_in_l, lp, mod_map, tiles_ctx, tiles_lat, n_ctx_tiles)
        ys, s_ctx = _wkv_group(scan_ops, s0_ctx, seq, 0, batch, vsplit_ctx, ys)
        ys, _ = _wkv_group(scan_ops, state_wkv[:, l], dec_seq, n_ctx // dec_seq, dec_batch, vsplit_lat, ys)
        ctx_states.append(s_ctx)

        kf_ctx = _hyena_filters(seq, lp)
        kf_lat = kf_ctx if dec_seq == seq else _hyena_filters(dec_seq, lp)
        y_h = _hyena(p_h, kf_ctx, lp, batch, seq, 0, y_h)
        y_h = _hyena(p_h, kf_lat, lp, dec_batch, dec_seq, n_ctx // dec_seq, y_h)

        x = _tail(ys, bonus, g, y_h, gates, x, mod, lp, mod_map, alpha)
        x = _ffn(x, mod, fp, mod_map, alpha, split_rows=n_ctx if l == depth - 1 else None)

    y_prompt, y_sample = x
    return (y_prompt.reshape(batch, seq, d), y_sample.reshape(dec_batch, dec_seq, d),
            jnp.stack(ctx_states, axis=1))
```

```python
import functools
import math

import jax
import jax.numpy as jnp
import ml_dtypes
import numpy as np
from jax.experimental import pallas as pl
from jax.experimental.pallas import tpu as pltpu

F32 = jnp.float32
BF16 = jnp.bfloat16

D_MODEL = 1024
GRID_W = 64
D_RWKV = 512
RWKV_HEAD = 64
N_RWKV_HEADS = D_RWKV // RWKV_HEAD
LORA_W = 64
LORA_A = 64
LORA_G = 128
D_RWKV_PROJ = 3 * D_RWKV + 2 * LORA_W + 2 * LORA_A + LORA_G
DECAY_SCALE = math.exp(-0.5)
GN_EPS = 64e-5
D_HYENA = 512
HYENA_ORDER = 2
HYENA_EMB = 33
HYENA_BANDS = (HYENA_EMB - 1) // 2
HYENA_HIDDEN = 64
HYENA_FAST_DECAY = 0.3
HYENA_SLOW_DECAY = 1.5
HYENA_TARGET = 1e-2
D_HYENA_PROJ = (HYENA_ORDER + 1) * D_HYENA
D_IN_PROJ = D_RWKV_PROJ + D_HYENA_PROJ + 2 * D_MODEL
D_FF = 2816
N_EXPERTS = 8
TOP_K = 2
D_FF_EXPERT = 1408
N_MOD = 6
LN_EPS = 1e-5
POS_BASE = 10000.0

LANES = 128
SUBLANES = 8
VMEM_LIMIT = 56 * 1024 * 1024

ROW_TILE = 256
HALO = 16
FFN_ROW_TILE = 512
ADA_COL_TILE = 1536
MXU_COLS = 256
HY_COL_TILE = 2 * MXU_COLS
SCAN_VBLOCKS = 8
SCAN_TBLOCK = 64
N_SCAN_SRC = 9
RELAYOUT_TBLOCK = LANES
PAD = LANES


def _cparams(*sem):
    return pltpu.CompilerParams(dimension_semantics=sem, vmem_limit_bytes=VMEM_LIMIT)


def _dot(a, b):
    return jnp.dot(a, b, preferred_element_type=F32)


def _split2(x):
    hi = x.astype(BF16)
    lo = (x - hi.astype(F32)).astype(BF16)
    return hi, lo


def _dot3(a, b):
    ah, al = _split2(a)
    bh, bl = _split2(b)
    return _dot(ah, bh) + _dot(al, bh) + _dot(ah, bl)


def _dot3_short_k(a, b):
    ah, al = _split2(a)
    bh, bl = _split2(b)
    return _dot(jnp.concatenate([ah, al], axis=1), jnp.concatenate([bh, bh], axis=0)) + _dot(ah, bl)


def _dot3_lhs_split(ah, al, b):
    bh, bl = _split2(b)
    return _dot(ah, bh) + _dot(al, bh) + _dot(ah, bl)


def _sigmoid(x):
    return 1.0 / (1.0 + jnp.exp(-x))


def _silu(x):
    return x * _sigmoid(x)


def _head_sum(x):
    s = x[:, 0:LANES]
    for c in range(1, D_RWKV // LANES):
        s = s + x[:, c * LANES:(c + 1) * LANES]
    shift = LANES // 2
    while shift >= N_RWKV_HEADS:
        s = s + pltpu.roll(s, shift, axis=1)
        shift //= 2
    return jnp.concatenate([s] * (D_RWKV // LANES), axis=1)


def _layer_norm(z, g, b):
    mean = jnp.mean(z, axis=-1, keepdims=True)
    d = z - mean
    var = jnp.mean(d * d, axis=-1, keepdims=True)
    return d * jax.lax.rsqrt(var + LN_EPS) * g + b


def _ada_kernel(c_ref, w_ref, b_ref, o_ref):
    o_ref[...] = _dot3(_silu(c_ref[...]), w_ref[...]) + b_ref[...]


def _adaln(cond, w_ada, b_ada):
    depth, d, n = w_ada.shape
    rows = cond.shape[0]
    tn = ADA_COL_TILE
    return pl.pallas_call(
        _ada_kernel,
        grid=(depth, n // tn),
        in_specs=[
            pl.BlockSpec((rows, d), lambda l, j: (0, 0)),
            pl.BlockSpec((None, d, tn), lambda l, j: (l, 0, j)),
            pl.BlockSpec((None, 1, tn), lambda l, j: (l, 0, j)),
        ],
        out_specs=pl.BlockSpec((None, rows, tn), lambda l, j: (l, 0, j)),
        out_shape=jax.ShapeDtypeStruct((depth, rows, n), F32),
        compiler_params=_cparams("arbitrary", "arbitrary"),
        name="adaln",
    )(cond, w_ada, b_ada.reshape(depth, 1, n))


def _mod_row_map(n_ctx_tiles, tiles_per_seq, ctx_row):
    def index_map(i, *_):
        return (jnp.where(i < n_ctx_tiles, ctx_row, (i - n_ctx_tiles) // tiles_per_seq), 0, 0)

    return index_map


def _inproj_kernel(tiles_ctx, tiles_lat, n_ctx_tiles,
                   x_ref, xprev_ref, xnext_ref, mod_ref, w_ref, mu_ref, w0_ref, w2_ref, a0_ref, a2_ref, g2_ref,
                   kk_w_ref, ka_ref, rk_ref,
                   ops_ref, bonus_ref, g_ref, ph_ref, gates_ref):
    i = pl.program_id(0)
    tm = x_ref.shape[0]
    j = jnp.where(i < n_ctx_tiles, i % tiles_ctx, (i - n_ctx_tiles) % tiles_lat)
    per_seq = jnp.where(i < n_ctx_tiles, tiles_ctx, tiles_lat)
    x_ext = jnp.concatenate([xprev_ref[...], x_ref[...], xnext_ref[...]], axis=0)
    h_ext = (x_ext * (1.0 + mod_ref[1:2, :]) + mod_ref[0:1, :]).astype(BF16)
    h = h_ext[HALO:HALO + tm]
    c0, c1 = D_RWKV_PROJ, D_RWKV_PROJ + D_HYENA_PROJ
    p_ext = _dot(h_ext, w_ref[:, 0:c0])
    ph_ref[...] = _dot(h, w_ref[:, c0:c1])
    gates_ref[...] = _dot(h, w_ref[:, c1:D_IN_PROJ]).astype(BF16)

    x = p_ext[HALO:HALO + tm]
    rowid = jax.lax.broadcasted_iota(jnp.int32, (tm, 1), 0)
    prev_row = jnp.where(j == 0, 0.0, p_ext[HALO - 1:HALO])
    next_row = jnp.where(j == per_seq - 1, 0.0, p_ext[HALO + tm:HALO + tm + 1])
    prev = jnp.where(rowid == 0, prev_row, pltpu.roll(x, 1, axis=0))
    nxt = jnp.where(rowid == tm - 1, next_row, pltpu.roll(x, tm - 1, axis=0))
    p = x + mu_ref[0:1, :] * (prev - x) + mu_ref[1:2, :] * (nxt - x)

    d = D_RWKV
    r, k, v = p[:, 0:d], p[:, d:2 * d], p[:, 2 * d:3 * d]
    low_w = p[:, 3 * d:3 * d + LANES]
    low_a = p[:, 3 * d + LANES:3 * d + 2 * LANES]
    low_g = p[:, 3 * d + 2 * LANES:3 * d + 3 * LANES]

    ops_ref[0] = r.T
    ops_ref[1] = v.T
    kk = k * kk_w_ref[...]
    kk = kk * jax.lax.rsqrt(jnp.maximum(_head_sum(kk * kk), 1e-24))
    ops_ref[2] = kk.T
    lw = _dot3_short_k(jnp.tanh(low_w), w2_ref[...])
    la = _dot(low_a.astype(BF16), a2_ref[...].astype(BF16))
    g_ref[...] = _dot(_sigmoid(low_g).astype(BF16), g2_ref[...].astype(BF16))
    ksum = jnp.zeros_like(k)
    for dirn in range(2):
        log_w = -DECAY_SCALE * _sigmoid(w0_ref[dirn:dirn + 1, :] + lw[:, dirn * d:(dirn + 1) * d])
        a = _sigmoid(a0_ref[dirn:dirn + 1, :] + la[:, dirn * d:(dirn + 1) * d])
        k_d = k * (1.0 + (a - 1.0) * ka_ref[...])
        ops_ref[3 + 3 * dirn] = jnp.exp(log_w).T
        ops_ref[4 + 3 * dirn] = k_d.T
        ops_ref[5 + 3 * dirn] = (kk * a).T
        ksum = ksum + k_d
    bonus_ref[...] = _head_sum(r * (0.5 * ksum) * rk_ref[...]) * v


def _inproj(x, mod, w_in_bf, lp, mod_map, tiles_ctx, tiles_lat, n_ctx_tiles):
    n = x.shape[0]
    tm = ROW_TILE
    halo = tm // HALO
    n_halo = n // HALO
    row = lambda width: pl.BlockSpec((tm, width), lambda i: (i, 0))
    full = lambda a: pl.BlockSpec(a.shape, lambda i: (0,) * a.ndim)
    params = [lp[k] for k in ("mu", "w0", "w2cat", "a0", "a2cat", "g2", "k_k", "k_a", "r_k")]
    out = lambda width: jax.ShapeDtypeStruct((n, width), F32)
    return pl.pallas_call(
        functools.partial(_inproj_kernel, tiles_ctx, tiles_lat, n_ctx_tiles),
        grid=(n // tm,),
        in_specs=[
            row(D_MODEL),
            pl.BlockSpec((HALO, D_MODEL), lambda i: (jnp.maximum(i * halo - 1, 0), 0)),
            pl.BlockSpec((HALO, D_MODEL), lambda i: (jnp.minimum((i + 1) * halo, n_halo - 1), 0)),
            pl.BlockSpec((None, N_MOD, D_MODEL), mod_map),
            pl.BlockSpec((D_MODEL, D_IN_PROJ), lambda i: (0, 0)),
        ] + [full(a) for a in params],
        out_specs=[pl.BlockSpec((N_SCAN_SRC, D_RWKV, tm), lambda i: (0, 0, i)),
                   row(D_RWKV), row(D_RWKV), row(D_HYENA_PROJ), row(2 * D_MODEL)],
        out_shape=[jax.ShapeDtypeStruct((N_SCAN_SRC, D_RWKV, n), F32), out(D_RWKV), out(D_RWKV),
                   out(D_HYENA_PROJ), jax.ShapeDtypeStruct((n, 2 * D_MODEL), BF16)],
        compiler_params=_cparams("arbitrary"),
        name="inproj",
    )(x, x, x, mod, w_in_bf, *params)


def _tail_kernel(alpha, yf_ref, yb_ref, bonus_ref, g_ref, yh_ref, gate_ref, x_ref, mod_ref,
                 gnw_ref, gnb_ref, wpa_ref, wpb_ref, wo_ref, lng_ref, lnb_ref, o_ref):
    y = yf_ref[...] + yb_ref[...]
    mean = _head_sum(y) * (1.0 / RWKV_HEAD)
    d = y - mean
    var = _head_sum(d * d) * (1.0 / RWKV_HEAD)
    y_n = d * jax.lax.rsqrt(var + GN_EPS) * gnw_ref[...] + gnb_ref[...]
    y_a = ((y_n + bonus_ref[...]) * g_ref[...]).astype(BF16)
    merged = (_sigmoid(gate_ref[:, 0:D_MODEL].astype(F32)) * _dot(y_a, wpa_ref[...])
              + _sigmoid(gate_ref[:, D_MODEL:2 * D_MODEL].astype(F32)) * _dot(yh_ref[...], wpb_ref[...]))
    m = _dot(merged.astype(BF16), wo_ref[...])
    z = alpha * x_ref[...] + mod_ref[2:3, :] * m
    o_ref[...] = _layer_norm(z, lng_ref[...], lnb_ref[...])


def _tail(ys, bonus, g, y_h, gates, x, mod, lp, mod_map, alpha):
    n = x.shape[0]
    tm = ROW_TILE
    row = lambda width: pl.BlockSpec((tm, width), lambda i: (i, 0))
    scan_dir = lambda d: pl.BlockSpec((None, tm, D_RWKV), lambda i: (d, i, 0))
    full = lambda a: pl.BlockSpec(a.shape, lambda i: (0,) * a.ndim)
    params = [lp[k] for k in ("gn_w", "gn_b", "w_pa", "w_pb", "w_o", "ln_g1", "ln_b1")]
    return pl.pallas_call(
        functools.partial(_tail_kernel, alpha),
        grid=(n // tm,),
        in_specs=[scan_dir(0), scan_dir(1)] + [row(D_RWKV)] * 3
        + [row(2 * D_MODEL), row(D_MODEL), pl.BlockSpec((None, N_MOD, D_MODEL), mod_map)]
        + [full(a) for a in params],
        out_specs=row(D_MODEL),
        out_shape=jax.ShapeDtypeStruct((n, D_MODEL), F32),
        compiler_params=_cparams("arbitrary"),
        name="mixer_tail",
    )(ys, ys, bonus, g, y_h, gates, x, mod, *params)


def _ffn_kernel(routed, alpha, split_tiles, x_ref, mod_ref, wgu_ref, wd_ref, rw_ref, rb_ref,
                lng_ref, lnb_ref, *rest):
    out_refs, (h_ref, acc_ref, comb_ref) = rest[:-3], rest[-3:]
    e = pl.program_id(1)
    lane = jax.lax.broadcasted_iota(jnp.int32, comb_ref.shape, 1)

    @pl.when(e == 0)
    def _():
        h = x_ref[...] * (1.0 + mod_ref[4:5, :]) + mod_ref[3:4, :]
        h_ref[...] = h.astype(BF16)
        acc_ref[...] = jnp.zeros_like(acc_ref)
        if routed:
            logits = _dot3(h, rw_ref[...]) + rb_ref[...]
            logits = jnp.where(lane < N_EXPERTS, logits, -jnp.inf)
            ex = jnp.exp(logits - jnp.max(logits, axis=-1, keepdims=True))
            probs = ex / jnp.sum(ex, axis=-1, keepdims=True)
            p1 = jnp.max(probs, axis=-1, keepdims=True)
            i1 = jnp.min(jnp.where(probs == p1, lane, PAD), axis=-1, keepdims=True)
            rest = jnp.where(lane == i1, -1.0, probs)
            p2 = jnp.max(rest, axis=-1, keepdims=True)
            i2 = jnp.min(jnp.where(rest == p2, lane, PAD), axis=-1, keepdims=True)
            total = p1 + p2
            comb_ref[...] = jnp.where(lane == i1, p1 / total, 0.0) + jnp.where(lane == i2, p2 / total, 0.0)

    gate_up = _dot(h_ref[...], wgu_ref[...])
    width = gate_up.shape[1] // 2
    act = _silu(gate_up[:, 0:width]) * gate_up[:, width:]
    if routed:
        act = act * jnp.sum(jnp.where(lane == e, comb_ref[...], 0.0), axis=-1, keepdims=True)
    acc_ref[...] += _dot(act.astype(BF16), wd_ref[...])

    def finish(o_ref):
        z = alpha * x_ref[...] + mod_ref[5:6, :] * acc_ref[...]
        o_ref[...] = _layer_norm(z, lng_ref[...], lnb_ref[...])

    last = e == pl.num_programs(1) - 1
    if split_tiles is None:
        pl.when(last)(lambda: finish(out_refs[0]))
    else:
        first_group = pl.program_id(0) < split_tiles
        pl.when(last & first_group)(lambda: finish(out_refs[0]))
        pl.when(last & jnp.logical_not(first_group))(lambda: finish(out_refs[1]))


def _ffn(x, mod, fp, mod_map, alpha, split_rows=None):
    n = x.shape[0]
    tm = FFN_ROW_TILE
    scale = tm // ROW_TILE
    routed, layer = fp["routed"], fp["layer"]
    if routed:
        groups = N_EXPERTS
        wgu_spec = pl.BlockSpec((None, None, D_MODEL, 2 * D_FF_EXPERT), lambda i, e: (layer, e, 0, 0))
        wd_spec = pl.BlockSpec((None, None, D_FF_EXPERT, D_MODEL), lambda i, e: (layer, e, 0, 0))
    else:
        groups = 1
        once = dict(pipeline_mode=pl.Buffered(1))
        wgu_spec = pl.BlockSpec((None, D_MODEL, 2 * D_FF), lambda i, e: (layer, 0, 0), **once)
        wd_spec = pl.BlockSpec((None, D_FF, D_MODEL), lambda i, e: (layer, 0, 0), **once)
    full = lambda a: pl.BlockSpec(a.shape, lambda i, e: (0,) * a.ndim)
    row = pl.BlockSpec((tm, D_MODEL), lambda i, e: (i, 0))
    mod_spec = pl.BlockSpec((None, N_MOD, D_MODEL), lambda i, e: mod_map(i * scale))
    if split_rows is None:
        split_tiles, out_specs, out_shape = None, row, jax.ShapeDtypeStruct((n, D_MODEL), F32)
    else:
        split_tiles = split_rows // tm
        out_specs = [pl.BlockSpec((tm, D_MODEL), lambda i, e: (jnp.minimum(i, split_tiles - 1), 0)),
                     pl.BlockSpec((tm, D_MODEL), lambda i, e: (jnp.maximum(i - split_tiles, 0), 0))]
        out_shape = [jax.ShapeDtypeStruct((split_rows, D_MODEL), F32),
                     jax.ShapeDtypeStruct((n - split_rows, D_MODEL), F32)]
    return pl.pallas_call(
        functools.partial(_ffn_kernel, routed, alpha, split_tiles),
        grid=(n // tm, groups),
        in_specs=[row, mod_spec, wgu_spec, wd_spec, full(fp["router_w"]), full(fp["router_b"]),
                  full(fp["ln_g"]), full(fp["ln_b"])],
        out_specs=out_specs,
        out_shape=out_shape,
        scratch_shapes=[pltpu.VMEM((tm, D_MODEL), BF16), pltpu.VMEM((tm, D_MODEL), F32),
                        pltpu.VMEM((tm, PAD), F32)],
        compiler_params=_cparams("arbitrary", "arbitrary"),
        name="moe_ffn" if routed else "dense_ffn",
    )(x, mod, fp["w_in"], fp["w_out"], fp["router_w"], fp["router_b"], fp["ln_g"], fp["ln_b"])


def _scan_kernel(r_ref, kk_ref, w_ref, k_ref, b_ref, v_ref, s0_ref, y_ref, sfin_ref, s_ref):
    tb = r_ref.shape[1]
    n_vblocks = s_ref.shape[0]
    tile = (SUBLANES, LANES)
    backward = pl.program_id(0) % 2 == 1

    @pl.when(pl.program_id(1) == 0)
    def _():
        s_ref[...] = s0_ref[...]

    def time_of(i):
        return jnp.where(backward, tb - 1 - i, i)

    def row(ref, k, t):
        return jnp.broadcast_to(ref[k, pl.ds(t, 1), :], tile)

    per_sweep = min(SCAN_VBLOCKS, n_vblocks)
    for part in range(n_vblocks // per_sweep):
        vbs = [part * per_sweep + j for j in range(per_sweep)]

        def step(i, sa, vbs=vbs):
            t = time_of(i)
            t_next = time_of(jnp.minimum(i + 1, tb - 1))
            v8 = [jnp.concatenate([v_ref[vb * SUBLANES + j, pl.ds(t, 1), :] for j in range(SUBLANES)], axis=0)
                  for vb in vbs]
            y = [jnp.zeros(tile, F32) for _ in vbs]
            sa_next = [jnp.zeros(tile, F32) for _ in vbs]
            for k in range(RWKV_HEAD):
                w, b, kd, r = row(w_ref, k, t), row(b_ref, k, t), row(k_ref, k, t), row(r_ref, k, t)
                kap = row(kk_ref, k, t_next)
                for j, vb in enumerate(vbs):
                    s = s_ref[vb, k] * w - sa[j] * b + v8[j] * kd
                    s_ref[vb, k] = s
                    y[j] = y[j] + s * r
                    sa_next[j] = sa_next[j] + s * kap
            for j, vb in enumerate(vbs):
                y_ref[t, pl.ds(vb * SUBLANES, SUBLANES), :] = y[j]
            return tuple(sa_next)

        t0 = time_of(0)
        sa0 = [jnp.zeros(tile, F32) for _ in vbs]
        for k in range(RWKV_HEAD):
            kap = row(kk_ref, k, t0)
            for j, vb in enumerate(vbs):
                sa0[j] = sa0[j] + s_ref[vb, k] * kap
        jax.lax.fori_loop(0, tb, step, tuple(sa0))

    @pl.when(pl.program_id(1) == pl.num_programs(1) - 1)
    def _():
        sfin_ref[...] = s_ref[...]


def _scan(xk, vk, s0):
    _, n_sg, hd, t_len, _ = xk.shape
    rows = vk.shape[1]
    tb = SCAN_TBLOCK
    n_t = t_len // tb
    t_of = lambda g, t: jnp.where(g % 2 == 1, n_t - 1 - t, t)
    shared = lambda o: pl.BlockSpec((None, None, hd, tb, LANES), lambda g, t: (o, g // 2, 0, t_of(g, t), 0))
    per_dir = lambda o: pl.BlockSpec((None, None, hd, tb, LANES),
                                     lambda g, t: (o + 3 * (g % 2), g // 2, 0, t_of(g, t), 0))
    st = pl.BlockSpec((None, rows // SUBLANES, hd, SUBLANES, LANES), lambda g, t: (g, 0, 0, 0, 0))
    return pl.pallas_call(
        _scan_kernel,
        grid=(2 * n_sg, n_t),
        in_specs=[shared(0), shared(1), per_dir(2), per_dir(3), per_dir(4),
                  pl.BlockSpec((None, rows, tb, LANES), lambda g, t: (g // 2, 0, t_of(g, t), 0)), st],
        out_specs=[pl.BlockSpec((None, tb, rows, LANES), lambda g, t: (g, t_of(g, t), 0, 0)), st],
        out_shape=[jax.ShapeDtypeStruct((2 * n_sg, t_len, rows, LANES), F32),
                   jax.ShapeDtypeStruct(s0.shape, F32)],
        scratch_shapes=[pltpu.VMEM(s0.shape[1:], F32)],
        compiler_params=_cparams("arbitrary", "arbitrary"),
        name="wkv_scan",
    )(xk, xk, xk, xk, xk, vk, s0)


def _to_chains_kernel(lane_parts, *refs):
    src_refs, o_ref = refs[:-1], refs[-1]

    def heads_of(c):
        rows = [src[pl.ds(pl.multiple_of(c * N_RWKV_HEADS, N_RWKV_HEADS), N_RWKV_HEADS), :] for src in src_refs]
        return jnp.concatenate(rows, axis=0)

    def per_channel(c, carry):
        parts = {off: heads_of(off + c) for off in set(lane_parts)}
        o_ref[c] = jnp.concatenate([parts[off] for off in lane_parts], axis=0).T
        return carry

    jax.lax.fori_loop(0, o_ref.shape[0], per_channel, 0, unroll=32)


def _to_chains(src, stream_map, n_streams, lane_parts, seq_len, n_seq_blk, first_blk, n_blk):
    tb = RELAYOUT_TBLOCK
    n_t = seq_len // tb
    channels = RWKV_HEAD // len(set(lane_parts))

    def seq_spec(j):
        return pl.BlockSpec((None, D_RWKV, tb),
                            lambda s, g, t: (stream_map(s), 0, ((first_blk + g) * n_seq_blk + j) * n_t + t))

    return pl.pallas_call(
        functools.partial(_to_chains_kernel, lane_parts),
        grid=(n_streams, n_blk, n_t),
        in_specs=[seq_spec(j) for j in range(n_seq_blk)],
        out_specs=pl.BlockSpec((None, None, channels, tb, LANES), lambda s, g, t: (s, g, 0, t, 0)),
        out_shape=jax.ShapeDtypeStruct((n_streams, n_blk, channels, seq_len, LANES), F32),
        compiler_params=_cparams("arbitrary", "arbitrary", "arbitrary"),
        name="to_chains",
    )(*([src] * n_seq_blk))


def _from_chains_kernel(n_vsplit, yf_ref, yb_ref, prev_ref, o_ref, z_ref):
    del prev_ref
    _, n_seq_blk, tb, _ = o_ref.shape
    rows = RWKV_HEAD // n_vsplit
    chains = LANES // n_vsplit
    for y_ref, dst in ((yf_ref, o_ref.at[0]), (yb_ref, o_ref.at[1])):
        def per_row(v, carry, y_ref=y_ref):
            z_ref[pl.ds(pl.multiple_of(v * LANES, LANES), LANES), :] = y_ref[pl.ds(v, tb, stride=rows), :].T
            return carry

        jax.lax.fori_loop(0, rows, per_row, 0, unroll=8)
        for s in range(n_seq_blk):
            pieces = [z_ref[pl.ds(v * LANES + part * chains + s * N_RWKV_HEADS, N_RWKV_HEADS), :]
                      for part in range(n_vsplit) for v in range(rows)]
            dst[s] = jnp.concatenate(pieces, axis=0).T


def _from_chains(y, n_vsplit, n_blocks, first_blk, prev):
    n_lg, seq_len, rows, _ = y.shape
    tb = RELAYOUT_TBLOCK
    n_seq_blk = LANES // n_vsplit // N_RWKV_HEADS
    y2 = y.reshape(n_lg, seq_len * rows, LANES)
    shape = (2, n_blocks, n_seq_blk, seq_len, D_RWKV)
    y_spec = lambda d: pl.BlockSpec((None, tb * rows, LANES), lambda g, t: (2 * g + d, t, 0))
    out = pl.pallas_call(
        functools.partial(_from_chains_kernel, n_vsplit),
        grid=(n_lg // 2, seq_len // tb),
        in_specs=[y_spec(0), y_spec(1), pl.BlockSpec(memory_space=pl.ANY)],
        out_specs=pl.BlockSpec((2, None, n_seq_blk, tb, D_RWKV), lambda g, t: (0, first_blk + g, 0, t, 0)),
        out_shape=jax.ShapeDtypeStruct(shape, F32),
        input_output_aliases={2: 0},
        scratch_shapes=[pltpu.VMEM((rows * LANES, tb), F32)],
        compiler_params=_cparams("arbitrary", "arbitrary"),
        name="from_chains",
    )(y2, y2, prev.reshape(shape))
    return out.reshape(prev.shape)


def _wkv_group(ops, s0, seq_len, first_seq, n_seq, n_vsplit, prev_y):
    _, d, n_tok = ops.shape
    n_seq_blk = LANES // n_vsplit // N_RWKV_HEADS
    n_sg = n_seq // n_seq_blk
    rows = RWKV_HEAD // n_vsplit
    first_blk = first_seq // n_seq_blk
    xk = _to_chains(ops, lambda s: s + jnp.where(s >= 1, 1, 0), N_SCAN_SRC - 1, (0,) * n_vsplit,
                    seq_len, n_seq_blk, first_blk, n_sg)
    vk = _to_chains(ops, lambda s: 1, 1, tuple(p * rows for p in range(n_vsplit)),
                    seq_len, n_seq_blk, first_blk, n_sg)[0]
    s0c = s0.reshape(n_sg, n_seq_blk, 2, N_RWKV_HEADS, n_vsplit, rows // SUBLANES, SUBLANES, RWKV_HEAD)
    s0c = jnp.transpose(s0c, (0, 2, 5, 7, 6, 4, 1, 3)).reshape(2 * n_sg, rows // SUBLANES, RWKV_HEAD, SUBLANES, LANES)
    y, s_fin = _scan(xk, vk, s0c)
    ys = _from_chains(y, n_vsplit, n_tok // seq_len // n_seq_blk, first_blk, prev_y)
    s_fin = s_fin.reshape(n_sg, 2, rows // SUBLANES, RWKV_HEAD, SUBLANES, n_vsplit, n_seq_blk, N_RWKV_HEADS)
    s_fin = jnp.transpose(s_fin, (0, 6, 1, 7, 5, 2, 4, 3)).reshape(n_seq, 2, N_RWKV_HEADS, RWKV_HEAD, RWKV_HEAD)
    return ys, s_fin


@functools.lru_cache(maxsize=None)
def _dft_constants(seq_len):
    n = 2 * seq_len
    idx = np.arange(seq_len)
    ang = (2.0 * np.pi / n) * ((idx[:, None] * idx[None, :]) % n)
    alt = np.where(idx % 2 == 0, 1.0, -1.0)
    f_re = np.cos(ang)
    f_im = -np.sin(ang)
    f_im[0, :] = alt
    fwd = np.concatenate([f_re, f_im], axis=0)
    c = np.full((seq_len,), 2.0)
    c[0] = 1.0
    g_re = np.cos(ang.T) * c[None, :] / n
    g_im = -2.0 * np.sin(ang.T) / n
    g_im[:, 0] = alt / n
    inv = np.concatenate([g_re, g_im], axis=1)

    def split(m):
        hi = m.astype(ml_dtypes.bfloat16)
        lo = (m - hi.astype(np.float64)).astype(ml_dtypes.bfloat16)
        return hi, lo

    return split(fwd) + split(inv)


@functools.lru_cache(maxsize=None)
def _filter_constants(seq_len):
    t = np.linspace(0.0, 1.0, seq_len)[:, None]
    f = np.linspace(1e-4, HYENA_BANDS - 1, HYENA_BANDS)
    ang = (2.0 * np.pi / seq_len) * np.arange(seq_len)[:, None] * f
    feats = np.zeros((seq_len, PAD), np.float32)
    feats[:, :HYENA_EMB] = np.concatenate([t, np.cos(ang), -np.sin(ang)], axis=-1)
    deltas = np.abs(np.linspace(math.log(HYENA_TARGET) / HYENA_SLOW_DECAY,
                                math.log(HYENA_TARGET) / HYENA_FAST_DECAY, D_HYENA))
    decay = np.exp(-t * deltas).astype(np.float32)
    return feats, np.tile(decay, (1, 2 * HYENA_ORDER))


def _filter_kernel(feats_ref, decay_ref, f1_ref, b1_ref, f2_ref, b2_ref, f3_ref, h_ref):
    hid = jnp.sin(_dot3(feats_ref[...], f1_ref[...]) + b1_ref[...])
    hid = jnp.sin(_dot3(hid, f2_ref[...]) + b2_ref[...])
    h_ref[...] = _dot3(hid, f3_ref[...]) * decay_ref[...]


def _spectrum_kernel(fh_ref, fl_ref, hf_ref, hb_ref, kf_ref):
    seq_len = hf_ref.shape[0]
    row = jax.lax.broadcasted_iota(jnp.int32, (seq_len, 1), 0)
    h_f = hf_ref[...]
    h_b = jnp.where(row == 0, 0.0, hb_ref[...])
    kf_ref[0:seq_len, :] = _dot3_lhs_split(fh_ref[0:seq_len, :], fl_ref[0:seq_len, :], h_f + h_b)
    im = _dot3_lhs_split(fh_ref[seq_len:, :], fl_ref[seq_len:, :], h_f - h_b)
    alternating = jnp.where(row % 2 == 0, 1.0, -1.0)
    nyquist_fix = 2.0 * jnp.sum(alternating * h_b, axis=0, keepdims=True)
    kf_ref[seq_len:, :] = jnp.where(row == 0, im + nyquist_fix, im)


def _hyena_filters(seq_len, lp):
    feats, decay = _filter_constants(seq_len)
    fh, fl, _, _ = _dft_constants(seq_len)
    args = [jnp.asarray(feats), jnp.asarray(decay), lp["hy_f1"], lp["hy_fb1"], lp["hy_f2"], lp["hy_fb2"],
            lp["hy_f3"]]
    h = pl.pallas_call(
        _filter_kernel,
        out_shape=jax.ShapeDtypeStruct((seq_len, 2 * HYENA_ORDER * D_HYENA), F32),
        compiler_params=_cparams(),
        name="hyena_filter_mlp",
    )(*args)
    tc = HY_COL_TILE
    per = D_HYENA // tc
    const = pl.BlockSpec((2 * seq_len, seq_len), lambda n, c: (0, 0), pipeline_mode=pl.Buffered(1))
    return pl.pallas_call(
        _spectrum_kernel,
        grid=(HYENA_ORDER, per),
        in_specs=[const, const,
                  pl.BlockSpec((seq_len, tc), lambda n, c: (0, (2 * n) * per + c)),
                  pl.BlockSpec((seq_len, tc), lambda n, c: (0, (2 * n + 1) * per + c))],
        out_specs=pl.BlockSpec((2 * seq_len, tc), lambda n, c: (0, n * per + c)),
        out_shape=jax.ShapeDtypeStruct((2 * seq_len, HYENA_ORDER * D_HYENA), F32),
        compiler_params=_cparams("arbitrary", "arbitrary"),
        name="hyena_filter_spectrum",
    )(jnp.asarray(fh), jnp.asarray(fl), h, h)


def _hyena_kernel(pz_ref, pg1_ref, pg2_ref, cwz_ref, cwg1_ref, cwg2_ref, cbz_ref, cbg1_ref, cbg2_ref,
                  kf0_ref, kf1_ref, skip_ref, f_ref, g_ref, *rest):
    o_ref = rest[-1]
    seq_len = pz_ref.shape[0]
    row = jax.lax.broadcasted_iota(jnp.int32, (seq_len, 1), 0)

    def short_conv(p_ref, cw_ref, cb_ref, cols):
        x = p_ref[:, cols]
        prev = jnp.where(row == 0, 0.0, pltpu.roll(x, 1, axis=0))
        nxt = jnp.where(row == seq_len - 1, 0.0, pltpu.roll(x, seq_len - 1, axis=0))
        return cw_ref[0:1, cols] * prev + cw_ref[1:2, cols] * x + cw_ref[2:3, cols] * nxt + cb_ref[:, cols]

    for c0 in range(0, pz_ref.shape[1], HY_COL_TILE):
        cols = slice(c0, c0 + HY_COL_TILE)
        z = short_conv(pz_ref, cwz_ref, cbz_ref, cols)
        gates = (short_conv(pg1_ref, cwg1_ref, cbg1_ref, cols), short_conv(pg2_ref, cwg2_ref, cbg2_ref, cols))
        for n, kf_ref in enumerate((kf0_ref, kf1_ref)):
            zf = _dot(f_ref[...], z.astype(BF16))
            z_re, z_im = zf[0:seq_len], zf[seq_len:]
            k_re, k_im = kf_ref[0:seq_len, cols], kf_ref[seq_len:, cols]
            p_re = jnp.where(row == 0, z_re * k_re, z_re * k_re - z_im * k_im)
            p_im = jnp.where(row == 0, z_im * k_im, z_re * k_im + z_im * k_re)
            conv = (_dot(g_ref[:, 0:seq_len], p_re.astype(BF16)) + _dot(g_ref[:, seq_len:], p_im.astype(BF16)))
            z = gates[n] * (conv + skip_ref[n:n + 1, cols] * z)
        o_ref[:, cols] = z.astype(o_ref.dtype)


def _hyena(p_h, kf, lp, n_seq, seq_len, seq_offset, prev):
    tc = HY_COL_TILE
    per = D_HYENA // tc
    fh, _, gh, _ = [jnp.asarray(a) for a in _dft_constants(seq_len)]
    once = dict(pipeline_mode=pl.Buffered(1))
    seg = lambda s: pl.BlockSpec((seq_len, tc), lambda b, c: (seq_offset + b, s * per + c))
    par = lambda rows, s: pl.BlockSpec((rows, tc), lambda b, c: (0, s * per + c))
    fconst = pl.BlockSpec((2 * seq_len, seq_len), lambda b, c: (0, 0), **once)
    gconst = pl.BlockSpec((seq_len, 2 * seq_len), lambda b, c: (0, 0), **once)
    cw, cb = lp["hy_conv_w"], lp["hy_conv_b"]
    return pl.pallas_call(
        _hyena_kernel,
        grid=(n_seq, per),
        in_specs=[seg(0), seg(1), seg(2), par(3, 0), par(3, 1), par(3, 2), par(1, 0), par(1, 1), par(1, 2),
                  pl.BlockSpec((2 * seq_len, tc), lambda b, c: (0, c), **(once if per == 1 else {})),
                  pl.BlockSpec((2 * seq_len, tc), lambda b, c: (0, per + c), **(once if per == 1 else {})),
                  pl.BlockSpec((HYENA_ORDER, tc), lambda b, c: (0, c)),
                  fconst, gconst, pl.BlockSpec(memory_space=pl.ANY)],
        out_specs=pl.BlockSpec((seq_len, tc), lambda b, c: (seq_offset + b, c)),
        out_shape=jax.ShapeDtypeStruct((p_h.shape[0], D_HYENA), BF16),
        input_output_aliases={14: 0},
        compiler_params=_cparams("arbitrary", "arbitrary"),
        name="hyena_conv",
    )(p_h, p_h, p_h, cw, cw, cw, cb, cb, cb, kf, kf, lp["hy_skip"], fh, gh, prev)


@functools.lru_cache(maxsize=None)
def _grid_pos_embed(n_tokens):
    rows = n_tokens // GRID_W
    row = np.repeat(np.arange(rows, dtype=np.float64), GRID_W)
    col = np.tile(np.arange(GRID_W, dtype=np.float64), rows)
    quarter = D_MODEL // 4
    omega = 1.0 / (POS_BASE ** (np.arange(quarter, dtype=np.float64) / quarter))

    def enc(pos):
        ang = pos[:, None] * omega
        return np.concatenate([np.sin(ang), np.cos(ang)], axis=-1)

    return np.concatenate([enc(row), enc(col)], axis=-1).astype(np.float32)


def _embed_kernel(n_ctx_tiles, xp_ref, xs_ref, pos_ref, o_ref):
    i = pl.program_id(0)

    @pl.when(i < n_ctx_tiles)
    def _():
        o_ref[...] = xp_ref[...]

    @pl.when(i >= n_ctx_tiles)
    def _():
        o_ref[...] = xs_ref[...] + pos_ref[...]


def _embed(x_prompt, x_sample):
    batch, seq, d = x_prompt.shape
    dec_batch, dec_seq, _ = x_sample.shape
    tm = ROW_TILE
    n_ctx_tiles, n_lat_tiles, per_seq = batch * seq // tm, dec_batch * dec_seq // tm, dec_seq // tm
    lat_tile = lambda i: jnp.maximum(i - n_ctx_tiles, 0)
    return pl.pallas_call(
        functools.partial(_embed_kernel, n_ctx_tiles),
        grid=(n_ctx_tiles + n_lat_tiles,),
        in_specs=[pl.BlockSpec((tm, d), lambda i: (jnp.minimum(i, n_ctx_tiles - 1), 0)),
                  pl.BlockSpec((tm, d), lambda i: (lat_tile(i), 0)),
                  pl.BlockSpec((tm, d), lambda i: (lat_tile(i) % per_seq, 0))],
        out_specs=pl.BlockSpec((tm, d), lambda i: (i, 0)),
        out_shape=jax.ShapeDtypeStruct((batch * seq + dec_batch * dec_seq, d), F32),
        compiler_params=_cparams("arbitrary"),
        name="embed",
    )(x_prompt.reshape(batch * seq, d), x_sample.reshape(dec_batch * dec_seq, d),
      jnp.asarray(_grid_pos_embed(dec_seq)))


def _block_diag2(m):
    z = jnp.zeros_like(m[0])
    return jnp.concatenate([jnp.concatenate([m[0], z], axis=1), jnp.concatenate([z, m[1]], axis=1)], axis=0)


def _interleave_heads(a, axis=-1):
    a = jnp.moveaxis(a, axis, -1)
    lead = a.shape[:-1]
    a = a.reshape(lead + (-1, N_RWKV_HEADS, RWKV_HEAD))
    a = jnp.swapaxes(a, -1, -2).reshape(lead + (-1,))
    return jnp.moveaxis(a, -1, axis)


def _pad_to(a, rows, cols):
    return jnp.pad(a, ((0, rows - a.shape[0]), (0, cols - a.shape[1])))


def kernel(x_prompt, x_sample, c, state_wkv, c_ctx, w_ada, b_ada, w_in, rwkv_mu, rwkv_w0, rwkv_w2, rwkv_a0, rwkv_a2, rwkv_g2, rwkv_k_k, rwkv_k_a, rwkv_r_k, rwkv_gn_w, rwkv_gn_b, hy_conv_w, hy_conv_b, hy_f1, hy_fb1, hy_f2, hy_fb2, hy_f3, hy_skip, w_pa, w_pb, w_o, ln_g, ln_b, ffn_w_in, ffn_w_out, router_w, router_b, exp_w_in, exp_w_out):
    batch, seq, d = x_prompt.shape
    dec_batch, dec_seq, _ = x_sample.shape
    depth = w_in.shape[0]
    alpha = (2 * depth) ** 0.25
    n_ctx, n_lat = batch * seq, dec_batch * dec_seq
    assert seq % ROW_TILE == 0 and dec_seq % ROW_TILE == 0 and n_ctx % dec_seq == 0
    assert n_ctx % FFN_ROW_TILE == 0 and n_lat % FFN_ROW_TILE == 0 and dec_seq % FFN_ROW_TILE == 0
    assert seq % RELAYOUT_TBLOCK == 0 and dec_seq % RELAYOUT_TBLOCK == 0
    assert 2 * LORA_W == LANES and 2 * LORA_A == LANES and LORA_G == LANES
    vsplit_ctx, vsplit_lat = (1 if b % 16 == 0 else 2 for b in (batch, dec_batch))
    assert batch % (16 // vsplit_ctx) == 0 and dec_batch % (16 // vsplit_lat) == 0
    assert (n_ctx // dec_seq) % (16 // vsplit_lat) == 0
    tiles_ctx, tiles_lat, n_ctx_tiles = seq // ROW_TILE, dec_seq // ROW_TILE, n_ctx // ROW_TILE
    mod_map = _mod_row_map(n_ctx_tiles, tiles_lat, dec_batch)

    x = _embed(x_prompt, x_sample)
    cond_rows = -(-(dec_batch + 1) // SUBLANES) * SUBLANES
    cond = jnp.zeros((cond_rows, d), F32).at[:dec_batch].set(c).at[dec_batch].set(c_ctx)
    mods = _adaln(cond, w_ada, b_ada).reshape(depth, cond_rows, N_MOD, d)

    il = _interleave_heads
    ffn_w_in_bf, ffn_w_out_bf = ffn_w_in.astype(BF16), ffn_w_out.astype(BF16)
    exp_w_in_bf, exp_w_out_bf = exp_w_in.astype(BF16), exp_w_out.astype(BF16)
    s0_ctx = jnp.zeros((batch, 2, N_RWKV_HEADS, RWKV_HEAD, RWKV_HEAD), F32)
    ctx_states = []
    ys = jnp.zeros((2, n_ctx + n_lat, D_RWKV), F32)
    y_h = jnp.zeros((n_ctx + n_lat, D_HYENA), BF16)
    for l in range(depth):
        lp = {
            "mu": jnp.concatenate([il(rwkv_mu[l][:, :3 * D_RWKV]), rwkv_mu[l][:, 3 * D_RWKV:]], axis=1),
            "w0": il(rwkv_w0[l]), "w2cat": _block_diag2(il(rwkv_w2[l])), "a0": il(rwkv_a0[l]),
            "a2cat": _block_diag2(il(rwkv_a2[l])), "g2": il(rwkv_g2[l]), "k_k": il(rwkv_k_k[l])[None],
            "k_a": il(rwkv_k_a[l])[None], "r_k": il(rwkv_r_k[l].reshape(1, D_RWKV)),
            "gn_w": il(rwkv_gn_w[l])[None], "gn_b": il(rwkv_gn_b[l])[None],
            "w_pa": il(w_pa[l], axis=0).astype(BF16), "w_pb": w_pb[l].astype(BF16), "w_o": w_o[l].astype(BF16),
            "ln_g1": ln_g[l, 0][None], "ln_b1": ln_b[l, 0][None],
            "hy_conv_w": hy_conv_w[l], "hy_conv_b": hy_conv_b[l][None], "hy_skip": hy_skip[l],
            "hy_f1": _pad_to(hy_f1[l], PAD, PAD), "hy_fb1": _pad_to(hy_fb1[l][None], 1, PAD),
            "hy_f2": _pad_to(hy_f2[l], PAD, PAD), "hy_fb2": _pad_to(hy_fb2[l][None], 1, PAD),
            "hy_f3": _pad_to(hy_f3[l], PAD, 2 * HYENA_ORDER * D_HYENA),
        }
        if l % 2 == 0:
            fp = {"routed": False, "layer": l // 2, "w_in": ffn_w_in_bf, "w_out": ffn_w_out_bf,
                  "router_w": jnp.zeros((d, PAD), F32), "router_b": jnp.zeros((1, PAD), F32)}
        else:
            fp = {"routed": True, "layer": l // 2, "w_in": exp_w_in_bf, "w_out": exp_w_out_bf,
                  "router_w": _pad_to(router_w[l // 2], d, PAD), "router_b": _pad_to(router_b[l // 2][None], 1, PAD)}
        fp["ln_g"], fp["ln_b"] = ln_g[l, 1][None], ln_b[l, 1][None]
        mod = mods[l]

        w_in_l = jnp.concatenate([il(w_in[l][:, :3 * D_RWKV]), w_in[l][:, 3 * D_RWKV:]], axis=1).astype(BF16)
        scan_ops, bonus, g, p_h, gates = _inproj(x, mod, w_in_l, lp, mod_map, tiles_ctx, tiles_lat, n_ctx_tiles)
        ys, s_ctx = _wkv_group(scan_ops, s0_ctx, seq, 0, batch, vsplit_ctx, ys)
        ys, _ = _wkv_group(scan_ops, state_wkv[:, l], dec_seq, n_ctx // dec_seq, dec_batch, vsplit_lat, ys)
        ctx_states.append(s_ctx)

        kf_ctx = _hyena_filters(seq, lp)
        kf_lat = kf_ctx if dec_seq == seq else _hyena_filters(dec_seq, lp)
        y_h = _hyena(p_h, kf_ctx, lp, batch, seq, 0, y_h)
        y_h = _hyena(p_h, kf_lat, lp, dec_batch, dec_seq, n_ctx // dec_seq, y_h)

        x = _tail(ys, bonus, g, y_h, gates, x, mod, lp, mod_map, alpha)
        x = _ffn(x, mod, fp, mod_map, alpha, split_rows=n_ctx if l == depth - 1 else None)

    y_prompt, y_sample = x
    return (y_prompt.reshape(batch, seq, d), y_sample.reshape(dec_batch, dec_seq, d),
            jnp.stack(ctx_states, axis=1))
```

```python
import functools
import math

import jax
import jax.numpy as jnp
import ml_dtypes
import numpy as np
from jax.experimental import pallas as pl
from jax.experimental.pallas import tpu as pltpu

F32 = jnp.float32
BF16 = jnp.bfloat16

D_MODEL = 1024
GRID_W = 64
D_RWKV = 512
RWKV_HEAD = 64
N_RWKV_HEADS = D_RWKV // RWKV_HEAD
LORA_W = 64
LORA_A = 64
LORA_G = 128
D_RWKV_PROJ = 3 * D_RWKV + 2 * LORA_W + 2 * LORA_A + LORA_G
DECAY_SCALE = math.exp(-0.5)
GN_EPS = 64e-5
D_HYENA = 512
HYENA_ORDER = 2
HYENA_EMB = 33
HYENA_BANDS = (HYENA_EMB - 1) // 2
HYENA_HIDDEN = 64
HYENA_FAST_DECAY = 0.3
HYENA_SLOW_DECAY = 1.5
HYENA_TARGET = 1e-2
D_HYENA_PROJ = (HYENA_ORDER + 1) * D_HYENA
D_IN_PROJ = D_RWKV_PROJ + D_HYENA_PROJ + 2 * D_MODEL
D_FF = 2816
N_EXPERTS = 8
TOP_K = 2
D_FF_EXPERT = 1408
N_MOD = 6
LN_EPS = 1e-5
POS_BASE = 10000.0

LANES = 128
SUBLANES = 8
VMEM_LIMIT = 56 * 1024 * 1024

ROW_TILE = 256
HALO = 16
FFN_ROW_TILE = 512
ADA_COL_TILE = 1536
MXU_COLS = 256
HY_COL_TILE = 2 * MXU_COLS
SCAN_VBLOCKS = 8
SCAN_TBLOCK = 64
N_SCAN_SRC = 9
RELAYOUT_TBLOCK = LANES
PAD = LANES


def _cparams(*sem):
    return pltpu.CompilerParams(dimension_semantics=sem, vmem_limit_bytes=VMEM_LIMIT)


def _dot(a, b):
    return jnp.dot(a, b, preferred_element_type=F32)


def _split2(x):
    hi = x.astype(BF16)
    lo = (x - hi.astype(F32)).astype(BF16)
    return hi, lo


def _dot3(a, b):
    ah, al = _split2(a)
    bh, bl = _split2(b)
    return _dot(ah, bh) + _dot(al, bh) + _dot(ah, bl)


def _dot3_short_k(a, b):
    ah, al = _split2(a)
    bh, bl = _split2(b)
    return _dot(jnp.concatenate([ah, al], axis=1), jnp.concatenate([bh, bh], axis=0)) + _dot(ah, bl)


def _dot3_lhs_split(ah, al, b):
    bh, bl = _split2(b)
    return _dot(ah, bh) + _dot(al, bh) + _dot(ah, bl)


def _sigmoid(x):
    return 1.0 / (1.0 + jnp.exp(-x))


def _silu(x):
    return x * _sigmoid(x)


def _head_sum(x):
    s = x[:, 0:LANES]
    for c in range(1, D_RWKV // LANES):
        s = s + x[:, c * LANES:(c + 1) * LANES]
    shift = LANES // 2
    while shift >= N_RWKV_HEADS:
        s = s + pltpu.roll(s, shift, axis=1)
        shift //= 2
    return jnp.concatenate([s] * (D_RWKV // LANES), axis=1)


def _layer_norm(z, g, b):
    mean = jnp.mean(z, axis=-1, keepdims=True)
    d = z - mean
    var = jnp.mean(d * d, axis=-1, keepdims=True)
    return d * jax.lax.rsqrt(var + LN_EPS) * g + b


def _ada_kernel(c_ref, w_ref, b_ref, o_ref):
    o_ref[...] = _dot3(_silu(c_ref[...]), w_ref[...]) + b_ref[...]


def _adaln(cond, w_ada, b_ada):
    depth, d, n = w_ada.shape
    rows = cond.shape[0]
    tn = ADA_COL_TILE
    return pl.pallas_call(
        _ada_kernel,
        grid=(depth, n // tn),
        in_specs=[
            pl.BlockSpec((rows, d), lambda l, j: (0, 0)),
            pl.BlockSpec((None, d, tn), lambda l, j: (l, 0, j)),
            pl.BlockSpec((None, 1, tn), lambda l, j: (l, 0, j)),
        ],
        out_specs=pl.BlockSpec((None, rows, tn), lambda l, j: (l, 0, j)),
        out_shape=jax.ShapeDtypeStruct((depth, rows, n), F32),
        compiler_params=_cparams("arbitrary", "arbitrary"),
        name="adaln",
    )(cond, w_ada, b_ada.reshape(depth, 1, n))


def _mod_row_map(n_ctx_tiles, tiles_per_seq, ctx_row):
    def index_map(i, *_):
        return (jnp.where(i < n_ctx_tiles, ctx_row, (i - n_ctx_tiles) // tiles_per_seq), 0, 0)

    return index_map


def _inproj_kernel(tiles_ctx, tiles_lat, n_ctx_tiles,
                   x_ref, xprev_ref, xnext_ref, mod_ref, w_ref, mu_ref, w0_ref, w2_ref, a0_ref, a2_ref, g2_ref,
                   kk_w_ref, ka_ref, rk_ref,
                   ops_ref, bonus_ref, g_ref, ph_ref, gates_ref):
    i = pl.program_id(0)
    tm = x_ref.shape[0]
    j = jnp.where(i < n_ctx_tiles, i % tiles_ctx, (i - n_ctx_tiles) % tiles_lat)
    per_seq = jnp.where(i < n_ctx_tiles, tiles_ctx, tiles_lat)
    x_ext = jnp.concatenate([xprev_ref[...], x_ref[...], xnext_ref[...]], axis=0)
    h_ext = (x_ext * (1.0 + mod_ref[1:2, :]) + mod_ref[0:1, :]).astype(BF16)
    h = h_ext[HALO:HALO + tm]
    c0, c1 = D_RWKV_PROJ, D_RWKV_PROJ + D_HYENA_PROJ
    p_ext = _dot(h_ext, w_ref[:, 0:c0])
    ph_ref[...] = _dot(h, w_ref[:, c0:c1])
    gates_ref[...] = _dot(h, w_ref[:, c1:D_IN_PROJ]).astype(BF16)

    x = p_ext[HALO:HALO + tm]
    rowid = jax.lax.broadcasted_iota(jnp.int32, (tm, 1), 0)
    prev_row = jnp.where(j == 0, 0.0, p_ext[HALO - 1:HALO])
    next_row = jnp.where(j == per_seq - 1, 0.0, p_ext[HALO + tm:HALO + tm + 1])
    prev = jnp.where(rowid == 0, prev_row, pltpu.roll(x, 1, axis=0))
    nxt = jnp.where(rowid == tm - 1, next_row, pltpu.roll(x, tm - 1, axis=0))
    p = x + mu_ref[0:1, :] * (prev - x) + mu_ref[1:2, :] * (nxt - x)

    d = D_RWKV
    r, k, v = p[:, 0:d], p[:, d:2 * d], p[:, 2 * d:3 * d]
    low_w = p[:, 3 * d:3 * d + LANES]
    low_a = p[:, 3 * d + LANES:3 * d + 2 * LANES]
    low_g = p[:, 3 * d + 2 * LANES:3 * d + 3 * LANES]

    ops_ref[0] = r.T
    ops_ref[1] = v.T
    kk = k * kk_w_ref[...]
    kk = kk * jax.lax.rsqrt(jnp.maximum(_head_sum(kk * kk), 1e-24))
    ops_ref[2] = kk.T
    lw = _dot3_short_k(jnp.tanh(low_w), w2_ref[...])
    la = _dot(low_a.astype(BF16), a2_ref[...].astype(BF16))
    g_ref[...] = _dot(_sigmoid(low_g).astype(BF16), g2_ref[...].astype(BF16))
    ksum = jnp.zeros_like(k)
    for dirn in range(2):
        log_w = -DECAY_SCALE * _sigmoid(w0_ref[dirn:dirn + 1, :] + lw[:, dirn * d:(dirn + 1) * d])
        a = _sigmoid(a0_ref[dirn:dirn + 1, :] + la[:, dirn * d:(dirn + 1) * d])
        k_d = k * (1.0 + (a - 1.0) * ka_ref[...])
        ops_ref[3 + 3 * dirn] = jnp.exp(log_w).T
        ops_ref[4 + 3 * dirn] = k_d.T
        ops_ref[5 + 3 * dirn] = (kk * a).T
        ksum = ksum + k_d
    bonus_ref[...] = _head_sum(r * (0.5 * ksum) * rk_ref[...]) * v


def _inproj(x, mod, w_in_bf, lp, mod_map, tiles_ctx, tiles_lat, n_ctx_tiles):
    n = x.shape[0]
    tm = ROW_TILE
    halo = tm // HALO
    n_halo = n // HALO
    row = lambda width: pl.BlockSpec((tm, width), lambda i: (i, 0))
    full = lambda a: pl.BlockSpec(a.shape, lambda i: (0,) * a.ndim)
    params = [lp[k] for k in ("mu", "w0", "w2cat", "a0", "a2cat", "g2", "k_k", "k_a", "r_k")]
    out = lambda width: jax.ShapeDtypeStruct((n, width), F32)
    return pl.pallas_call(
        functools.partial(_inproj_kernel, tiles_ctx, tiles_lat, n_ctx_tiles),
        grid=(n // tm,),
        in_specs=[
            row(D_MODEL),
            pl.BlockSpec((HALO, D_MODEL), lambda i: (jnp.maximum(i * halo - 1, 0), 0)),
            pl.BlockSpec((HALO, D_MODEL), lambda i: (jnp.minimum((i + 1) * halo, n_halo - 1), 0)),
            pl.BlockSpec((None, N_MOD, D_MODEL), mod_map),
            pl.BlockSpec((D_MODEL, D_IN_PROJ), lambda i: (0, 0)),
        ] + [full(a) for a in params],
        out_specs=[pl.BlockSpec((N_SCAN_SRC, D_RWKV, tm), lambda i: (0, 0, i)),
                   row(D_RWKV), row(D_RWKV), row(D_HYENA_PROJ), row(2 * D_MODEL)],
        out_shape=[jax.ShapeDtypeStruct((N_SCAN_SRC, D_RWKV, n), F32), out(D_RWKV), out(D_RWKV),
                   out(D_HYENA_PROJ), jax.ShapeDtypeStruct((n, 2 * D_MODEL), BF16)],
        compiler_params=_cparams("arbitrary"),
        name="inproj",
    )(x, x, x, mod, w_in_bf, *params)


def _tail_kernel(alpha, yf_ref, yb_ref, bonus_ref, g_ref, yh_ref, gate_ref, x_ref, mod_ref,
                 gnw_ref, gnb_ref, wpa_ref, wpb_ref, wo_ref, lng_ref, lnb_ref, o_ref):
    y = yf_ref[...] + yb_ref[...]
    mean = _head_sum(y) * (1.0 / RWKV_HEAD)
    d = y - mean
    var = _head_sum(d * d) * (1.0 / RWKV_HEAD)
    y_n = d * jax.lax.rsqrt(var + GN_EPS) * gnw_ref[...] + gnb_ref[...]
    y_a = ((y_n + bonus_ref[...]) * g_ref[...]).astype(BF16)
    merged = (_sigmoid(gate_ref[:, 0:D_MODEL].astype(F32)) * _dot(y_a, wpa_ref[...])
              + _sigmoid(gate_ref[:, D_MODEL:2 * D_MODEL].astype(F32)) * _dot(yh_ref[...], wpb_ref[...]))
    m = _dot(merged.astype(BF16), wo_ref[...])
    z = alpha * x_ref[...] + mod_ref[2:3, :] * m
    o_ref[...] = _layer_norm(z, lng_ref[...], lnb_ref[...])


def _tail(ys, bonus, g, y_h, gates, x, mod, lp, mod_map, alpha):
    n = x.shape[0]
    tm = ROW_TILE
    row = lambda width: pl.BlockSpec((tm, width), lambda i: (i, 0))
    scan_dir = lambda d: pl.BlockSpec((None, tm, D_RWKV), lambda i: (d, i, 0))
    full = lambda a: pl.BlockSpec(a.shape, lambda i: (0,) * a.ndim)
    params = [lp[k] for k in ("gn_w", "gn_b", "w_pa", "w_pb", "w_o", "ln_g1", "ln_b1")]
    return pl.pallas_call(
        functools.partial(_tail_kernel, alpha),
        grid=(n // tm,),
        in_specs=[scan_dir(0), scan_dir(1)] + [row(D_RWKV)] * 3
        + [row(2 * D_MODEL), row(D_MODEL), pl.BlockSpec((None, N_MOD, D_MODEL), mod_map)]
        + [full(a) for a in params],
        out_specs=row(D_MODEL),
        out_shape=jax.ShapeDtypeStruct((n, D_MODEL), F32),
        compiler_params=_cparams("arbitrary"),
        name="mixer_tail",
    )(ys, ys, bonus, g, y_h, gates, x, mod, *params)


def _ffn_kernel(routed, alpha, split_tiles, x_ref, mod_ref, wgu_ref, wd_ref, rw_ref, rb_ref,
                lng_ref, lnb_ref, *rest):
    out_refs, (h_ref, acc_ref, comb_ref) = rest[:-3], rest[-3:]
    e = pl.program_id(1)
    lane = jax.lax.broadcasted_iota(jnp.int32, comb_ref.shape, 1)

    @pl.when(e == 0)
    def _():
        h = x_ref[...] * (1.0 + mod_ref[4:5, :]) + mod_ref[3:4, :]
        h_ref[...] = h.astype(BF16)
        acc_ref[...] = jnp.zeros_like(acc_ref)
        if routed:
            logits = _dot3(h, rw_ref[...]) + rb_ref[...]
            logits = jnp.where(lane < N_EXPERTS, logits, -jnp.inf)
            ex = jnp.exp(logits - jnp.max(logits, axis=-1, keepdims=True))
            probs = ex / jnp.sum(ex, axis=-1, keepdims=True)
            p1 = jnp.max(probs, axis=-1, keepdims=True)
            i1 = jnp.min(jnp.where(probs == p1, lane, PAD), axis=-1, keepdims=True)
            rest = jnp.where(lane == i1, -1.0, probs)
            p2 = jnp.max(rest, axis=-1, keepdims=True)
            i2 = jnp.min(jnp.where(rest == p2, lane, PAD), axis=-1, keepdims=True)
            total = p1 + p2
            comb_ref[...] = jnp.where(lane == i1, p1 / total, 0.0) + jnp.where(lane == i2, p2 / total, 0.0)

    gate_up = _dot(h_ref[...], wgu_ref[...])
    width = gate_up.shape[1] // 2
    act = _silu(gate_up[:, 0:width]) * gate_up[:, width:]
    if routed:
        act = act * jnp.sum(jnp.where(lane == e, comb_ref[...], 0.0), axis=-1, keepdims=True)
    acc_ref[...] += _dot(act.astype(BF16), wd_ref[...])

    def finish(o_ref):
        z = alpha * x_ref[...] + mod_ref[5:6, :] * acc_ref[...]
        o_ref[...] = _layer_norm(z, lng_ref[...], lnb_ref[...])

    last = e == pl.num_programs(1) - 1
    if split_tiles is None:
        pl.when(last)(lambda: finish(out_refs[0]))
    else:
        first_group = pl.program_id(0) < split_tiles
        pl.when(last & first_group)(lambda: finish(out_refs[0]))
        pl.when(last & jnp.logical_not(first_group))(lambda: finish(out_refs[1]))


def _ffn(x, mod, fp, mod_map, alpha, split_rows=None):
    n = x.shape[0]
    tm = FFN_ROW_TILE
    scale = tm // ROW_TILE
    routed, layer = fp["routed"], fp["layer"]
    if routed:
        groups = N_EXPERTS
        wgu_spec = pl.BlockSpec((None, None, D_MODEL, 2 * D_FF_EXPERT), lambda i, e: (layer, e, 0, 0))
        wd_spec = pl.BlockSpec((None, None, D_FF_EXPERT, D_MODEL), lambda i, e: (layer, e, 0, 0))
    else:
        groups = 1
        once = dict(pipeline_mode=pl.Buffered(1))
        wgu_spec = pl.BlockSpec((None, D_MODEL, 2 * D_FF), lambda i, e: (layer, 0, 0), **once)
        wd_spec = pl.BlockSpec((None, D_FF, D_MODEL), lambda i, e: (layer, 0, 0), **once)
    full = lambda a: pl.BlockSpec(a.shape, lambda i, e: (0,) * a.ndim)
    row = pl.BlockSpec((tm, D_MODEL), lambda i, e: (i, 0))
    mod_spec = pl.BlockSpec((None, N_MOD, D_MODEL), lambda i, e: mod_map(i * scale))
    if split_rows is None:
        split_tiles, out_specs, out_shape = None, row, jax.ShapeDtypeStruct((n, D_MODEL), F32)
    else:
        split_tiles = split_rows // tm
        out_specs = [pl.BlockSpec((tm, D_MODEL), lambda i, e: (jnp.minimum(i, split_tiles - 1), 0)),
                     pl.BlockSpec((tm, D_MODEL), lambda i, e: (jnp.maximum(i - split_tiles, 0), 0))]
        out_shape = [jax.ShapeDtypeStruct((split_rows, D_MODEL), F32),
                     jax.ShapeDtypeStruct((n - split_rows, D_MODEL), F32)]
    return pl.pallas_call(
        functools.partial(_ffn_kernel, routed, alpha, split_tiles),
        grid=(n // tm, groups),
        in_specs=[row, mod_spec, wgu_spec, wd_spec, full(fp["router_w"]), full(fp["router_b"]),
                  full(fp["ln_g"]), full(fp["ln_b"])],
        out_specs=out_specs,
        out_shape=out_shape,
        scratch_shapes=[pltpu.VMEM((tm, D_MODEL), BF16), pltpu.VMEM((tm, D_MODEL), F32),
                        pltpu.VMEM((tm, PAD), F32)],
        compiler_params=_cparams("arbitrary", "arbitrary"),
        name="moe_ffn" if routed else "dense_ffn",
    )(x, mod, fp["w_in"], fp["w_out"], fp["router_w"], fp["router_b"], fp["ln_g"], fp["ln_b"])


def _scan_kernel(r_ref, kk_ref, w_ref, k_ref, b_ref, v_ref, s0_ref, y_ref, sfin_ref, s_ref):
    tb = r_ref.shape[1]
    n_vblocks = s_ref.shape[0]
    tile = (SUBLANES, LANES)
    backward = pl.program_id(0) % 2 == 1

    @pl.when(pl.program_id(1) == 0)
    def _():
        s_ref[...] = s0_ref[...]

    def time_of(i):
        return jnp.where(backward, tb - 1 - i, i)

    def row(ref, k, t):
        return jnp.broadcast_to(ref[k, pl.ds(t, 1), :], tile)

    per_sweep = min(SCAN_VBLOCKS, n_vblocks)
    for part in range(n_vblocks // per_sweep):
        vbs = [part * per_sweep + j for j in range(per_sweep)]

        def step(i, sa, vbs=vbs):
            t = time_of(i)
            t_next = time_of(jnp.minimum(i + 1, tb - 1))
            v8 = [jnp.concatenate([v_ref[vb * SUBLANES + j, pl.ds(t, 1), :] for j in range(SUBLANES)], axis=0)
                  for vb in vbs]
            y = [jnp.zeros(tile, F32) for _ in vbs]
            sa_next = [jnp.zeros(tile, F32) for _ in vbs]
            for k in range(RWKV_HEAD):
                w, b, kd, r = row(w_ref, k, t), row(b_ref, k, t), row(k_ref, k, t), row(r_ref, k, t)
                kap = row(kk_ref, k, t_next)
                for j, vb in enumerate(vbs):
                    s = s_ref[vb, k] * w - sa[j] * b + v8[j] * kd
                    s_ref[vb, k] = s
                    y[j] = y[j] + s * r
                    sa_next[j] = sa_next[j] + s * kap
            for j, vb in enumerate(vbs):
                y_ref[t, pl.ds(vb * SUBLANES, SUBLANES), :] = y[j]
            return tuple(sa_next)

        t0 = time_of(0)
        sa0 = [jnp.zeros(tile, F32) for _ in vbs]
        for k in range(RWKV_HEAD):
            kap = row(kk_ref, k, t0)
            for j, vb in enumerate(vbs):
                sa0[j] = sa0[j] + s_ref[vb, k] * kap
        jax.lax.fori_loop(0, tb, step, tuple(sa0))

    @pl.when(pl.program_id(1) == pl.num_programs(1) - 1)
    def _():
        sfin_ref[...] = s_ref[...]


def _scan(xk, vk, s0):
    _, n_sg, hd, t_len, _ = xk.shape
    rows = vk.shape[1]
    tb = SCAN_TBLOCK
    n_t = t_len // tb
    t_of = lambda g, t: jnp.where(g % 2 == 1, n_t - 1 - t, t)
    shared = lambda o: pl.BlockSpec((None, None, hd, tb, LANES), lambda g, t: (o, g // 2, 0, t_of(g, t), 0))
    per_dir = lambda o: pl.BlockSpec((None, None, hd, tb, LANES),
                                     lambda g, t: (o + 3 * (g % 2), g // 2, 0, t_of(g, t), 0))
    st = pl.BlockSpec((None, rows // SUBLANES, hd, SUBLANES, LANES), lambda g, t: (g, 0, 0, 0, 0))
    return pl.pallas_call(
        _scan_kernel,
        grid=(2 * n_sg, n_t),
        in_specs=[shared(0), shared(1), per_dir(2), per_dir(3), per_dir(4),
                  pl.BlockSpec((None, rows, tb, LANES), lambda g, t: (g // 2, 0, t_of(g, t), 0)), st],
        out_specs=[pl.BlockSpec((None, tb, rows, LANES), lambda g, t: (g, t_of(g, t), 0, 0)), st],
        out_shape=[jax.ShapeDtypeStruct((2 * n_sg, t_len, rows, LANES), F32),
                   jax.ShapeDtypeStruct(s0.shape, F32)],
        scratch_shapes=[pltpu.VMEM(s0.shape[1:], F32)],
        compiler_params=_cparams("arbitrary", "arbitrary"),
        name="wkv_scan",
    )(xk, xk, xk, xk, xk, vk, s0)


def _to_chains_kernel(lane_parts, *refs):
    src_refs, o_ref = refs[:-1], refs[-1]

    def heads_of(c):
        rows = [src[pl.ds(pl.multiple_of(c * N_RWKV_HEADS, N_RWKV_HEADS), N_RWKV_HEADS), :] for src in src_refs]
        return jnp.concatenate(rows, axis=0)

    def per_channel(c, carry):
        parts = {off: heads_of(off + c) for off in set(lane_parts)}
        o_ref[c] = jnp.concatenate([parts[off] for off in lane_parts], axis=0).T
        return carry

    jax.lax.fori_loop(0, o_ref.shape[0], per_channel, 0, unroll=32)


def _to_chains(src, stream_map, n_streams, lane_parts, seq_len, n_seq_blk, first_blk, n_blk):
    tb = RELAYOUT_TBLOCK
    n_t = seq_len // tb
    channels = RWKV_HEAD // len(set(lane_parts))

    def seq_spec(j):
        return pl.BlockSpec((None, D_RWKV, tb),
                            lambda s, g, t: (stream_map(s), 0, ((first_blk + g) * n_seq_blk + j) * n_t + t))

    return pl.pallas_call(
        functools.partial(_to_chains_kernel, lane_parts),
        grid=(n_streams, n_blk, n_t),
        in_specs=[seq_spec(j) for j in range(n_seq_blk)],
        out_specs=pl.BlockSpec((None, None, channels, tb, LANES), lambda s, g, t: (s, g, 0, t, 0)),
        out_shape=jax.ShapeDtypeStruct((n_streams, n_blk, channels, seq_len, LANES), F32),
        compiler_params=_cparams("arbitrary", "arbitrary", "arbitrary"),
        name="to_chains",
    )(*([src] * n_seq_blk))


def _from_chains_kernel(n_vsplit, yf_ref, yb_ref, prev_ref, o_ref, z_ref):
    del prev_ref
    _, n_seq_blk, tb, _ = o_ref.shape
    rows = RWKV_HEAD // n_vsplit
    chains = LANES // n_vsplit
    for y_ref, dst in ((yf_ref, o_ref.at[0]), (yb_ref, o_ref.at[1])):
        def per_row(v, carry, y_ref=y_ref):
            z_ref[pl.ds(pl.multiple_of(v * LANES, LANES), LANES), :] = y_ref[pl.ds(v, tb, stride=rows), :].T
            return carry

        jax.lax.fori_loop(0, rows, per_row, 0, unroll=8)
        for s in range(n_seq_blk):
            pieces = [z_ref[pl.ds(v * LANES + part * chains + s * N_RWKV_HEADS, N_RWKV_HEADS), :]
                      for part in range(n_vsplit) for v in range(rows)]
            dst[s] = jnp.concatenate(pieces, axis=0).T


def _from_chains(y, n_vsplit, n_blocks, first_blk, prev):
    n_lg, seq_len, rows, _ = y.shape
    tb = RELAYOUT_TBLOCK
    n_seq_blk = LANES // n_vsplit // N_RWKV_HEADS
    y2 = y.reshape(n_lg, seq_len * rows, LANES)
    shape = (2, n_blocks, n_seq_blk, seq_len, D_RWKV)
    y_spec = lambda d: pl.BlockSpec((None, tb * rows, LANES), lambda g, t: (2 * g + d, t, 0))
    out = pl.pallas_call(
        functools.partial(_from_chains_kernel, n_vsplit),
        grid=(n_lg // 2, seq_len // tb),
        in_specs=[y_spec(0), y_spec(1), pl.BlockSpec(memory_space=pl.ANY)],
        out_specs=pl.BlockSpec((2, None, n_seq_blk, tb, D_RWKV), lambda g, t: (0, first_blk + g, 0, t, 0)),
        out_shape=jax.ShapeDtypeStruct(shape, F32),
        input_output_aliases={2: 0},
        scratch_shapes=[pltpu.VMEM((rows * LANES, tb), F32)],
        compiler_params=_cparams("arbitrary", "arbitrary"),
        name="from_chains",
    )(y2, y2, prev.reshape(shape))
    return out.reshape(prev.shape)


def _wkv_group(ops, s0, seq_len, first_seq, n_seq, n_vsplit, prev_y):
    _, d, n_tok = ops.shape
    n_seq_blk = LANES // n_vsplit // N_RWKV_HEADS
    n_sg = n_seq // n_seq_blk
    rows = RWKV_HEAD // n_vsplit
    first_blk = first_seq // n_seq_blk
    xk = _to_chains(ops, lambda s: s + jnp.where(s >= 1, 1, 0), N_SCAN_SRC - 1, (0,) * n_vsplit,
                    seq_len, n_seq_blk, first_blk, n_sg)
    vk = _to_chains(ops, lambda s: 1, 1, tuple(p * rows for p in range(n_vsplit)),
                    seq_len, n_seq_blk, first_blk, n_sg)[0]
    s0c = s0.reshape(n_sg, n_seq_blk, 2, N_RWKV_HEADS, n_vsplit, rows // SUBLANES, SUBLANES, RWKV_HEAD)
    s0c = jnp.transpose(s0c, (0, 2, 5, 7, 6, 4, 1, 3)).reshape(2 * n_sg, rows // SUBLANES, RWKV_HEAD, SUBLANES, LANES)
    y, s_fin = _scan(xk, vk, s0c)
    ys = _from_chains(y, n_vsplit, n_tok // seq_len // n_seq_blk, first_blk, prev_y)
    s_fin = s_fin.reshape(n_sg, 2, rows // SUBLANES, RWKV_HEAD, SUBLANES, n_vsplit, n_seq_blk, N_RWKV_HEADS)
    s_fin = jnp.transpose(s_fin, (0, 6, 1, 7, 5, 2, 4, 3)).reshape(n_seq, 2, N_RWKV_HEADS, RWKV_HEAD, RWKV_HEAD)
    return ys, s_fin


@functools.lru_cache(maxsize=None)
def _dft_constants(seq_len):
    n = 2 * seq_len
    idx = np.arange(seq_len)
    ang = (2.0 * np.pi / n) * ((idx[:, None] * idx[None, :]) % n)
    alt = np.where(idx % 2 == 0, 1.0, -1.0)
    f_re = np.cos(ang)
    f_im = -np.sin(ang)
    f_im[0, :] = alt
    fwd = np.concatenate([f_re, f_im], axis=0)
    c = np.full((seq_len,), 2.0)
    c[0] = 1.0
    g_re = np.cos(ang.T) * c[None, :] / n
    g_im = -2.0 * np.sin(ang.T) / n
    g_im[:, 0] = alt / n
    inv = np.concatenate([g_re, g_im], axis=1)

    def split(m):
        hi = m.astype(ml_dtypes.bfloat16)
        lo = (m - hi.astype(np.float64)).astype(ml_dtypes.bfloat16)
        return hi, lo

    return split(fwd) + split(inv)


@functools.lru_cache(maxsize=None)
def _filter_constants(seq_len):
    t = np.linspace(0.0, 1.0, seq_len)[:, None]
    f = np.linspace(1e-4, HYENA_BANDS - 1, HYENA_BANDS)
    ang = (2.0 * np.pi / seq_len) * np.arange(seq_len)[:, None] * f
    feats = np.zeros((seq_len, PAD), np.float32)
    feats[:, :HYENA_EMB] = np.concatenate([t, np.cos(ang), -np.sin(ang)], axis=-1)
    deltas = np.abs(np.linspace(math.log(HYENA_TARGET) / HYENA_SLOW_DECAY,
                                math.log(HYENA_TARGET) / HYENA_FAST_DECAY, D_HYENA))
    decay = np.exp(-t * deltas).astype(np.float32)
    return feats, np.tile(decay, (1, 2 * HYENA_ORDER))


def _filter_kernel(feats_ref, decay_ref, f1_ref, b1_ref, f2_ref, b2_ref, f3_ref, h_ref):
    hid = jnp.sin(_dot3(feats_ref[...], f1_ref[...]) + b1_ref[...])
    hid = jnp.sin(_dot3(hid, f2_ref[...]) + b2_ref[...])
    h_ref[...] = _dot3(hid, f3_ref[...]) * decay_ref[...]


def _spectrum_kernel(fh_ref, fl_ref, hf_ref, hb_ref, kf_ref):
    seq_len = hf_ref.shape[0]
    row = jax.lax.broadcasted_iota(jnp.int32, (seq_len, 1), 0)
    h_f = hf_ref[...]
    h_b = jnp.where(row == 0, 0.0, hb_ref[...])
    kf_ref[0:seq_len, :] = _dot3_lhs_split(fh_ref[0:seq_len, :], fl_ref[0:seq_len, :], h_f + h_b)
    im = _dot3_lhs_split(fh_ref[seq_len:, :], fl_ref[seq_len:, :], h_f - h_b)
    alternating = jnp.where(row % 2 == 0, 1.0, -1.0)
    nyquist_fix = 2.0 * jnp.sum(alternating * h_b, axis=0, keepdims=True)
    kf_ref[seq_len:, :] = jnp.where(row == 0, im + nyquist_fix, im)


def _hyena_filters(seq_len, lp):
    feats, decay = _filter_constants(seq_len)
    fh, fl, _, _ = _dft_constants(seq_len)
    args = [jnp.asarray(feats), jnp.asarray(decay), lp["hy_f1"], lp["hy_fb1"], lp["hy_f2"], lp["hy_fb2"],
            lp["hy_f3"]]
    h = pl.pallas_call(
        _filter_kernel,
        out_shape=jax.ShapeDtypeStruct((seq_len, 2 * HYENA_ORDER * D_HYENA), F32),
        compiler_params=_cparams(),
        name="hyena_filter_mlp",
    )(*args)
    tc = HY_COL_TILE
    per = D_HYENA // tc
    const = pl.BlockSpec((2 * seq_len, seq_len), lambda n, c: (0, 0), pipeline_mode=pl.Buffered(1))
    return pl.pallas_call(
        _spectrum_kernel,
        grid=(HYENA_ORDER, per),
        in_specs=[const, const,
                  pl.BlockSpec((seq_len, tc), lambda n, c: (0, (2 * n) * per + c)),
                  pl.BlockSpec((seq_len, tc), lambda n, c: (0, (2 * n + 1) * per + c))],
        out_specs=pl.BlockSpec((2 * seq_len, tc), lambda n, c: (0, n * per + c)),
        out_shape=jax.ShapeDtypeStruct((2 * seq_len, HYENA_ORDER * D_HYENA), F32),
        compiler_params=_cparams("arbitrary", "arbitrary"),
        name="hyena_filter_spectrum",
    )(jnp.asarray(fh), jnp.asarray(fl), h, h)


def _hyena_kernel(pz_ref, pg1_ref, pg2_ref, cwz_ref, cwg1_ref, cwg2_ref, cbz_ref, cbg1_ref, cbg2_ref,
                  kf0_ref, kf1_ref, skip_ref, f_ref, g_ref, *rest):
    o_ref = rest[-1]
    seq_len = pz_ref.shape[0]
    row = jax.lax.broadcasted_iota(jnp.int32, (seq_len, 1), 0)

    def short_conv(p_ref, cw_ref, cb_ref, cols):
        x = p_ref[:, cols]
        prev = jnp.where(row == 0, 0.0, pltpu.roll(x, 1, axis=0))
        nxt = jnp.where(row == seq_len - 1, 0.0, pltpu.roll(x, seq_len - 1, axis=0))
        return cw_ref[0:1, cols] * prev + cw_ref[1:2, cols] * x + cw_ref[2:3, cols] * nxt + cb_ref[:, cols]

    for c0 in range(0, pz_ref.shape[1], HY_COL_TILE):
        cols = slice(c0, c0 + HY_COL_TILE)
        z = short_conv(pz_ref, cwz_ref, cbz_ref, cols)
        gates = (short_conv(pg1_ref, cwg1_ref, cbg1_ref, cols), short_conv(pg2_ref, cwg2_ref, cbg2_ref, cols))
        for n, kf_ref in enumerate((kf0_ref, kf1_ref)):
            zf = _dot(f_ref[...], z.astype(BF16))
            z_re, z_im = zf[0:seq_len], zf[seq_len:]
            k_re, k_im = kf_ref[0:seq_len, cols], kf_ref[seq_len:, cols]
            p_re = jnp.where(row == 0, z_re * k_re, z_re * k_re - z_im * k_im)
            p_im = jnp.where(row == 0, z_im * k_im, z_re * k_im + z_im * k_re)
            conv = (_dot(g_ref[:, 0:seq_len], p_re.astype(BF16)) + _dot(g_ref[:, seq_len:], p_im.astype(BF16)))
            z = gates[n] * (conv + skip_ref[n:n + 1, cols] * z)
        o_ref[:, cols] = z.astype(o_ref.dtype)


def _hyena(p_h, kf, lp, n_seq, seq_len, seq_offset, prev):
    tc = HY_COL_TILE
    per = D_HYENA // tc
    fh, _, gh, _ = [jnp.asarray(a) for a in _dft_constants(seq_len)]
    once = dict(pipeline_mode=pl.Buffered(1))
    seg = lambda s: pl.BlockSpec((seq_len, tc), lambda b, c: (seq_offset + b, s * per + c))
    par = lambda rows, s: pl.BlockSpec((rows, tc), lambda b, c: (0, s * per + c))
    fconst = pl.BlockSpec((2 * seq_len, seq_len), lambda b, c: (0, 0), **once)
    gconst = pl.BlockSpec((seq_len, 2 * seq_len), lambda b, c: (0, 0), **once)
    cw, cb = lp["hy_conv_w"], lp["hy_conv_b"]
    return pl.pallas_call(
        _hyena_kernel,
        grid=(n_seq, per),
        in_specs=[seg(0), seg(1), seg(2), par(3, 0), par(3, 1), par(3, 2), par(1, 0), par(1, 1), par(1, 2),
                  pl.BlockSpec((2 * seq_len, tc), lambda b, c: (0, c), **(once if per == 1 else {})),
                  pl.BlockSpec((2 * seq_len, tc), lambda b, c: (0, per + c), **(once if per == 1 else {})),
                  pl.BlockSpec((HYENA_ORDER, tc), lambda b, c: (0, c)),
                  fconst, gconst, pl.BlockSpec(memory_space=pl.ANY)],
        out_specs=pl.BlockSpec((seq_len, tc), lambda b, c: (seq_offset + b, c)),
        out_shape=jax.ShapeDtypeStruct((p_h.shape[0], D_HYENA), BF16),
        input_output_aliases={14: 0},
        compiler_params=_cparams("arbitrary", "arbitrary"),
        name="hyena_conv",
    )(p_h, p_h, p_h, cw, cw, cw, cb, cb, cb, kf, kf, lp["hy_skip"], fh, gh, prev)


@functools.lru_cache(maxsize=None)
def _grid_pos_embed(n_tokens):
    rows = n_tokens // GRID_W
    row = np.repeat(np.arange(rows, dtype=np.float64), GRID_W)
    col = np.tile(np.arange(GRID_W, dtype=np.float64), rows)
    quarter = D_MODEL // 4
    omega = 1.0 / (POS_BASE ** (np.arange(quarter, dtype=np.float64) / quarter))

    def enc(pos):
        ang = pos[:, None] * omega
        return np.concatenate([np.sin(ang), np.cos(ang)], axis=-1)

    return np.concatenate([enc(row), enc(col)], axis=-1).astype(np.float32)


def _embed_kernel(n_ctx_tiles, xp_ref, xs_ref, pos_ref, o_ref):
    i = pl.program_id(0)

    @pl.when(i < n_ctx_tiles)
    def _():
        o_ref[...] = xp_ref[...]

    @pl.when(i >= n_ctx_tiles)
    def _():
        o_ref[...] = xs_ref[...] + pos_ref[...]


def _embed(x_prompt, x_sample):
    batch, seq, d = x_prompt.shape
    dec_batch, dec_seq, _ = x_sample.shape
    tm = math.gcd(batch * seq, dec_seq, 4 * ROW_TILE)
    n_ctx_tiles, n_lat_tiles, per_seq = batch * seq // tm, dec_batch * dec_seq // tm, dec_seq // tm
    lat_tile = lambda i: jnp.maximum(i - n_ctx_tiles, 0)
    return pl.pallas_call(
        functools.partial(_embed_kernel, n_ctx_tiles),
        grid=(n_ctx_tiles + n_lat_tiles,),
        in_specs=[pl.BlockSpec((tm, d), lambda i: (jnp.minimum(i, n_ctx_tiles - 1), 0)),
                  pl.BlockSpec((tm, d), lambda i: (lat_tile(i), 0)),
                  pl.BlockSpec((tm, d), lambda i: (lat_tile(i) % per_seq, 0))],
        out_specs=pl.BlockSpec((tm, d), lambda i: (i, 0)),
        out_shape=jax.ShapeDtypeStruct((batch * seq + dec_batch * dec_seq, d), F32),
        compiler_params=_cparams("arbitrary"),
        name="embed",
    )(x_prompt.reshape(batch * seq, d), x_sample.reshape(dec_batch * dec_seq, d),
      jnp.asarray(_grid_pos_embed(dec_seq)))


def _block_diag2(m):
    z = jnp.zeros_like(m[0])
    return jnp.concatenate([jnp.concatenate([m[0], z], axis=1), jnp.concatenate([z, m[1]], axis=1)], axis=0)


def _interleave_heads(a, axis=-1):
    a = jnp.moveaxis(a, axis, -1)
    lead = a.shape[:-1]
    a = a.reshape(lead + (-1, N_RWKV_HEADS, RWKV_HEAD))
    a = jnp.swapaxes(a, -1, -2).reshape(lead + (-1,))
    return jnp.moveaxis(a, -1, axis)


def _pad_to(a, rows, cols):
    return jnp.pad(a, ((0, rows - a.shape[0]), (0, cols - a.shape[1])))


def kernel(x_prompt, x_sample, c, state_wkv, c_ctx, w_ada, b_ada, w_in, rwkv_mu, rwkv_w0, rwkv_w2, rwkv_a0, rwkv_a2, rwkv_g2, rwkv_k_k, rwkv_k_a, rwkv_r_k, rwkv_gn_w, rwkv_gn_b, hy_conv_w, hy_conv_b, hy_f1, hy_fb1, hy_f2, hy_fb2, hy_f3, hy_skip, w_pa, w_pb, w_o, ln_g, ln_b, ffn_w_in, ffn_w_out, router_w, router_b, exp_w_in, exp_w_out):
    batch, seq, d = x_prompt.shape
    dec_batch, dec_seq, _ = x_sample.shape
    depth = w_in.shape[0]
    alpha = (2 * depth) ** 0.25
    n_ctx, n_lat = batch * seq, dec_batch * dec_seq
    assert seq % ROW_TILE == 0 and dec_seq % ROW_TILE == 0 and n_ctx % dec_seq == 0
    assert n_ctx % FFN_ROW_TILE == 0 and n_lat % FFN_ROW_TILE == 0 and dec_seq % FFN_ROW_TILE == 0
    assert seq % RELAYOUT_TBLOCK == 0 and dec_seq % RELAYOUT_TBLOCK == 0
    assert 2 * LORA_W == LANES and 2 * LORA_A == LANES and LORA_G == LANES
    vsplit_ctx, vsplit_lat = (1 if b % 16 == 0 else 2 for b in (batch, dec_batch))
    assert batch % (16 // vsplit_ctx) == 0 and dec_batch % (16 // vsplit_lat) == 0
    assert (n_ctx // dec_seq) % (16 // vsplit_lat) == 0
    tiles_ctx, tiles_lat, n_ctx_tiles = seq // ROW_TILE, dec_seq // ROW_TILE, n_ctx // ROW_TILE
    mod_map = _mod_row_map(n_ctx_tiles, tiles_lat, dec_batch)

    x = _embed(x_prompt, x_sample)
    cond_rows = -(-(dec_batch + 1) // SUBLANES) * SUBLANES
    cond = jnp.zeros((cond_rows, d), F32).at[:dec_batch].set(c).at[dec_batch].set(c_ctx)
    mods = _adaln(cond, w_ada, b_ada).reshape(depth, cond_rows, N_MOD, d)

    il = _interleave_heads
    ffn_w_in_bf, ffn_w_out_bf = ffn_w_in.astype(BF16), ffn_w_out.astype(BF16)
    exp_w_in_bf, exp_w_out_bf = exp_w_in.astype(BF16), exp_w_out.astype(BF16)
    s0_ctx = jnp.zeros((batch, 2, N_RWKV_HEADS, RWKV_HEAD, RWKV_HEAD), F32)
    ctx_states = []
    ys = jnp.zeros((2, n_ctx + n_lat, D_RWKV), F32)
    y_h = jnp.zeros((n_ctx + n_lat, D_HYENA), BF16)
    for l in range(depth):
        lp = {
            "mu": jnp.concatenate([il(rwkv_mu[l][:, :3 * D_RWKV]), rwkv_mu[l][:, 3 * D_RWKV:]], axis=1),
            "w0": il(rwkv_w0[l]), "w2cat": _block_diag2(il(rwkv_w2[l])), "a0": il(rwkv_a0[l]),
            "a2cat": _block_diag2(il(rwkv_a2[l])), "g2": il(rwkv_g2[l]), "k_k": il(rwkv_k_k[l])[None],
            "k_a": il(rwkv_k_a[l])[None], "r_k": il(rwkv_r_k[l].reshape(1, D_RWKV)),
            "gn_w": il(rwkv_gn_w[l])[None], "gn_b": il(rwkv_gn_b[l])[None],
            "w_pa": il(w_pa[l], axis=0).astype(BF16), "w_pb": w_pb[l].astype(BF16), "w_o": w_o[l].astype(BF16),
            "ln_g1": ln_g[l, 0][None], "ln_b1": ln_b[l, 0][None],
            "hy_conv_w": hy_conv_w[l], "hy_conv_b": hy_conv_b[l][None], "hy_skip": hy_skip[l],
            "hy_f1": _pad_to(hy_f1[l], PAD, PAD), "hy_fb1": _pad_to(hy_fb1[l][None], 1, PAD),
            "hy_f2": _pad_to(hy_f2[l], PAD, PAD), "hy_fb2": _pad_to(hy_fb2[l][None], 1, PAD),
            "hy_f3": _pad_to(hy_f3[l], PAD, 2 * HYENA_ORDER * D_HYENA),
        }
        if l % 2 == 0:
            fp = {"routed": False, "layer": l // 2, "w_in": ffn_w_in_bf, "w_out": ffn_w_out_bf,
                  "router_w": jnp.zeros((d, PAD), F32), "router_b": jnp.zeros((1, PAD), F32)}
        else:
            fp = {"routed": True, "layer": l // 2, "w_in": exp_w_in_bf, "w_out": exp_w_out_bf,
                  "router_w": _pad_to(router_w[l // 2], d, PAD), "router_b": _pad_to(router_b[l // 2][None], 1, PAD)}
        fp["ln_g"], fp["ln_b"] = ln_g[l, 1][None], ln_b[l, 1][None]
        mod = mods[l]

        w_in_l = jnp.concatenate([il(w_in[l][:, :3 * D_RWKV]), w_in[l][:, 3 * D_RWKV:]], axis=1).astype(BF16)
        scan_ops, bonus, g, p_h, gates = _inproj(x, mod, w_in_l, lp, mod_map, tiles_ctx, tiles_lat, n_ctx_tiles)
        ys, s_ctx = _wkv_group(scan_ops, s0_ctx, seq, 0, batch, vsplit_ctx, ys)
        ys, _ = _wkv_group(scan_ops, state_wkv[:, l], dec_seq, n_ctx // dec_seq, dec_batch, vsplit_lat, ys)
        ctx_states.append(s_ctx)

        kf_ctx = _hyena_filters(seq, lp)
        kf_lat = kf_ctx if dec_seq == seq else _hyena_filters(dec_seq, lp)
        y_h = _hyena(p_h, kf_ctx, lp, batch, seq, 0, y_h)
        y_h = _hyena(p_h, kf_lat, lp, dec_batch, dec_seq, n_ctx // dec_seq, y_h)

        x = _tail(ys, bonus, g, y_h, gates, x, mod, lp, mod_map, alpha)
        x = _ffn(x, mod, fp, mod_map, alpha, split_rows=n_ctx if l == depth - 1 else None)

    y_prompt, y_sample = x
    return (y_prompt.reshape(batch, seq, d), y_sample.reshape(dec_batch, dec_seq, d),
            jnp.stack(ctx_states, axis=1))
```
